```python
import math
import jax, jax.numpy as jnp
from jax import lax
import numpy as np

D_MODEL = 2048
BATCH = 8
SEQ = 8192
DEPTH = 4

CHUNK = 64
D_MIX = 2 * D_MODEL
SSD_WIDTH = D_MIX // 2
SBA_WIDTH = D_MIX - SSD_WIDTH
SSD_HEAD_DIM = 64
SSD_HEADS = SSD_WIDTH // SSD_HEAD_DIM
SSD_GROUPS = 4
SSD_STATE = 128
SSD_CONV = 4
SBA_HEAD_DIM = 128
SBA_HEADS = SBA_WIDTH // SBA_HEAD_DIM
Q_BLOCK = 128
EPS = 1e-6
CONV_DIM = SSD_WIDTH + 2 * SSD_GROUPS * SSD_STATE
IN_SPLITS = (
    SSD_WIDTH,
    SSD_WIDTH + CONV_DIM,
    SSD_WIDTH + CONV_DIM + SSD_HEADS,
    SSD_WIDTH + CONV_DIM + SSD_HEADS + SBA_WIDTH,
    SSD_WIDTH + CONV_DIM + SSD_HEADS + 2 * SBA_WIDTH,
    SSD_WIDTH + CONV_DIM + SSD_HEADS + 3 * SBA_WIDTH,
)
IN_COLS = SSD_WIDTH + CONV_DIM + SSD_HEADS + 4 * SBA_WIDTH

kernel_name = "hybrid_ssd_stickbreaking_parallel_heads"


def rmsnorm(x, w):
    xf = x.astype(jnp.float32)
    y = xf * lax.rsqrt(jnp.mean(xf * xf, axis=-1, keepdims=True) + EPS)
    return (y * w.astype(jnp.float32)).astype(x.dtype)


def causal_depthwise_conv(u, w, b):
    k_taps = w.shape[0]
    seq = u.shape[1]
    up = jnp.pad(u, ((0, 0), (k_taps - 1, 0), (0, 0)))
    out = b
    for j in range(k_taps):
        out = out + w[j] * up[:, j:j + seq]
    return out


def ssd_chunked_scan(x, dt, a_head, bm, cm):
    bsz, seq, n_heads, hd = x.shape
    g, n = bm.shape[2], bm.shape[3]
    r = n_heads // g
    nc = seq // CHUNK
    xs = (x.astype(jnp.float32) * dt[..., None]).reshape(bsz, nc, CHUNK, g, r, hd)
    da = (dt * a_head).reshape(bsz, nc, CHUNK, g, r).transpose(0, 3, 4, 1, 2)
    bc = bm.astype(jnp.float32).reshape(bsz, nc, CHUNK, g, n)
    cc = cm.astype(jnp.float32).reshape(bsz, nc, CHUNK, g, n)
    a_cum = jnp.cumsum(da, axis=-1)
    tri = jnp.tril(jnp.ones((CHUNK, CHUNK), dtype=bool))
    seg = a_cum[..., :, None] - a_cum[..., None, :]
    decay_in = jnp.exp(jnp.where(tri, seg, -jnp.inf))
    cb = jnp.einsum('bclgn,bcsgn->bgcls', cc, bc)
    y_diag = jnp.einsum('bgcls,bgrcls,bcsgrp->bclgrp', cb, decay_in, xs)
    decay_states = jnp.exp(a_cum[..., -1:] - a_cum)
    states = jnp.einsum('bclgn,bgrcl,bclgrp->cbgrpn', bc, decay_states, xs)
    chunk_decay = jnp.exp(a_cum[..., -1]).transpose(3, 0, 1, 2)

    def step(h, inp):
        s_c, d_c = inp
        return h * d_c[..., None, None] + s_c, h

    _, prev = lax.scan(step, jnp.zeros(states.shape[1:], jnp.float32), (states, chunk_decay))
    y_off = jnp.einsum('bclgn,cbgrpn,bgrcl->bclgrp', cc, prev, jnp.exp(a_cum))
    return (y_diag + y_off).reshape(bsz, seq, n_heads * hd)


def stick_breaking_attention(q, k, v):
    bsz, n_heads, seq, hd = q.shape
    n_blocks = seq // Q_BLOCK
    scale = 1.0 / math.sqrt(hd)
    kf = k.astype(jnp.float32)
    vf = v.astype(jnp.float32)
    key_pos = jnp.arange(seq)

    def block(i):
        start = i * Q_BLOCK
        qb = lax.dynamic_slice_in_dim(q, start, Q_BLOCK, axis=2).astype(jnp.float32)
        z = jnp.einsum('bhqd,bhkd->bhqk', qb, kf) * scale
        qpos = start + jnp.arange(Q_BLOCK)
        earlier = key_pos[None, :] < qpos[:, None]
        log_keep = jnp.where(earlier, jax.nn.log_sigmoid(-z), 0.0)
        later_sum = lax.cumsum(log_keep, axis=3, reverse=True) - log_keep
        weight = jnp.where(earlier, jnp.exp(jax.nn.log_sigmoid(z) + later_sum), 0.0)
        return jnp.einsum('bhqk,bhkd->bhqd', weight, vf)

    out = lax.map(block, jnp.arange(n_blocks))
    return out.transpose(1, 0, 3, 2, 4).reshape(bsz, seq, n_heads * hd)


def hybrid_layer(x, norm_w, w_in, conv_w, conv_b, dt_bias, a_log, d_skip, ssd_norm_w, w_out):
    bsz, seq, _ = x.shape
    h = rmsnorm(x, norm_w)
    proj = h @ w_in
    z, xbc, dt_raw, q, k, v, g = jnp.split(proj, IN_SPLITS, axis=-1)

    xbc = jax.nn.silu(causal_depthwise_conv(xbc, conv_w, conv_b))
    gn = SSD_GROUPS * SSD_STATE
    xs, bm, cm = jnp.split(xbc, (SSD_WIDTH, SSD_WIDTH + gn), axis=-1)
    xs = xs.reshape(bsz, seq, SSD_HEADS, SSD_HEAD_DIM)
    bm = bm.reshape(bsz, seq, SSD_GROUPS, SSD_STATE)
    cm = cm.reshape(bsz, seq, SSD_GROUPS, SSD_STATE)
    dt = jax.nn.softplus(dt_raw.astype(jnp.float32) + dt_bias.astype(jnp.float32))
    a_head = -jnp.exp(a_log.astype(jnp.float32))
    y = ssd_chunked_scan(xs, dt, a_head, bm, cm)
    y = y + (d_skip.astype(jnp.float32)[:, None] * xs.astype(jnp.float32)).reshape(bsz, seq, SSD_WIDTH)
    yg = (y * jax.nn.silu(z.astype(jnp.float32))).reshape(bsz, seq, SSD_GROUPS, SSD_WIDTH // SSD_GROUPS)
    yg = yg * lax.rsqrt(jnp.mean(yg * yg, axis=-1, keepdims=True) + EPS)
    y_ssd = yg.reshape(bsz, seq, SSD_WIDTH) * ssd_norm_w.astype(jnp.float32)

    def heads(t):
        return t.reshape(bsz, seq, SBA_HEADS, SBA_HEAD_DIM).transpose(0, 2, 1, 3)
    o = stick_breaking_attention(heads(q), heads(k), heads(v))
    y_sba = o * jax.nn.silu(g.astype(jnp.float32))

    mix = jnp.concatenate([y_ssd, y_sba], axis=-1).astype(x.dtype)
    return x + mix @ w_out


def _fwd_setup_inputs(seed: int = 0) -> dict:
    key = jax.random.key(seed)
    ks = jax.random.split(key, 12)
    f32 = jnp.float32
    x = jax.random.normal(ks[0], (BATCH, SEQ, D_MODEL), f32)
    norm_w = 1.0 + 0.02 * jax.random.normal(ks[1], (DEPTH, D_MODEL), f32)
    w_in = jax.random.normal(ks[2], (DEPTH, D_MODEL, IN_COLS), f32) * D_MODEL ** -0.5
    conv_w = jax.random.normal(ks[3], (DEPTH, SSD_CONV, CONV_DIM), f32) * SSD_CONV ** -0.5
    conv_b = 0.02 * jax.random.normal(ks[4], (DEPTH, CONV_DIM), f32)
    dt0 = jnp.exp(jax.random.uniform(ks[5], (DEPTH, SSD_HEADS), f32,
                                     math.log(1e-3), math.log(1e-1)))
    dt_bias = dt0 + jnp.log(-jnp.expm1(-dt0))
    a_log = jnp.log(jax.random.uniform(ks[6], (DEPTH, SSD_HEADS), f32, 1.0, 16.0))
    d_skip = 1.0 + 0.02 * jax.random.normal(ks[7], (DEPTH, SSD_HEADS), f32)
    ssd_norm_w = 1.0 + 0.02 * jax.random.normal(ks[8], (DEPTH, SSD_WIDTH), f32)
    w_out = jax.random.normal(ks[9], (DEPTH, D_MIX, D_MODEL), f32) * D_MIX ** -0.5
    final_norm_w = 1.0 + 0.02 * jax.random.normal(ks[10], (D_MODEL,), f32)
    return {"x": x, "norm_w": norm_w, "w_in": w_in, "conv_w": conv_w, "conv_b": conv_b,
            "dt_bias": dt_bias, "a_log": a_log, "d_skip": d_skip, "ssd_norm_w": ssd_norm_w,
            "w_out": w_out, "final_norm_w": final_norm_w}


def _fwd_reference(x, norm_w, w_in, conv_w, conv_b, dt_bias, a_log, d_skip, ssd_norm_w, w_out, final_norm_w):
    h = x
    for i in range(DEPTH):
        h = hybrid_layer(h, norm_w[i], w_in[i], conv_w[i], conv_b[i], dt_bias[i], a_log[i],
                         d_skip[i], ssd_norm_w[i], w_out[i])
    return rmsnorm(h, final_norm_w)


import jax as _jax
import jax.numpy as _jnp

TWIN_FORMAT = 'train_step'
FWD_PARAMS = ['x', 'norm_w', 'w_in', 'conv_w', 'conv_b', 'dt_bias', 'a_log', 'd_skip', 'ssd_norm_w', 'w_out', 'final_norm_w']
TWIN_WEIGHTS = ['norm_w', 'w_in', 'conv_w', 'conv_b', 'dt_bias', 'a_log', 'd_skip', 'ssd_norm_w', 'w_out', 'final_norm_w']
TWIN_DIFF_INPUT = 'x'
TWIN_INPUTS = ['x', 'norm_w', 'w_in', 'conv_w', 'conv_b', 'dt_bias', 'a_log', 'd_skip', 'ssd_norm_w', 'w_out', 'final_norm_w', 'loss_target', 'm_norm_w', 'm_w_in', 'm_conv_w', 'm_conv_b', 'm_dt_bias', 'm_a_log', 'm_d_skip', 'm_ssd_norm_w', 'm_w_out', 'm_final_norm_w', 'v_norm_w', 'v_w_in', 'v_conv_w', 'v_conv_b', 'v_dt_bias', 'v_a_log', 'v_d_skip', 'v_ssd_norm_w', 'v_w_out', 'v_final_norm_w']
TWIN_OUTPUTS = ['loss', 'grad_x', 'grad_norm_w', 'grad_w_in', 'grad_conv_w', 'grad_conv_b', 'grad_dt_bias', 'grad_a_log', 'grad_d_skip', 'grad_ssd_norm_w', 'grad_w_out', 'grad_final_norm_w', 'delta_norm_w', 'delta_w_in', 'delta_conv_w', 'delta_conv_b', 'delta_dt_bias', 'delta_a_log', 'delta_d_skip', 'delta_ssd_norm_w', 'delta_w_out', 'delta_final_norm_w', 'new_m_norm_w', 'new_m_w_in', 'new_m_conv_w', 'new_m_conv_b', 'new_m_dt_bias', 'new_m_a_log', 'new_m_d_skip', 'new_m_ssd_norm_w', 'new_m_w_out', 'new_m_final_norm_w', 'new_v_norm_w', 'new_v_w_in', 'new_v_conv_w', 'new_v_conv_b', 'new_v_dt_bias', 'new_v_a_log', 'new_v_d_skip', 'new_v_ssd_norm_w', 'new_v_w_out', 'new_v_final_norm_w']
TWIN_LEAF_KINDS = {'loss': 'loss', 'grad_x': 'grad_x', 'grad_norm_w': 'grad_w', 'grad_w_in': 'grad_w', 'grad_conv_w': 'grad_w', 'grad_conv_b': 'grad_w', 'grad_dt_bias': 'grad_w', 'grad_a_log': 'grad_w', 'grad_d_skip': 'grad_w', 'grad_ssd_norm_w': 'grad_w', 'grad_w_out': 'grad_w', 'grad_final_norm_w': 'grad_w', 'delta_norm_w': 'delta_w', 'delta_w_in': 'delta_w', 'delta_conv_w': 'delta_w', 'delta_conv_b': 'delta_w', 'delta_dt_bias': 'delta_w', 'delta_a_log': 'delta_w', 'delta_d_skip': 'delta_w', 'delta_ssd_norm_w': 'delta_w', 'delta_w_out': 'delta_w', 'delta_final_norm_w': 'delta_w', 'new_m_norm_w': 'new_m', 'new_m_w_in': 'new_m', 'new_m_conv_w': 'new_m', 'new_m_conv_b': 'new_m', 'new_m_dt_bias': 'new_m', 'new_m_a_log': 'new_m', 'new_m_d_skip': 'new_m', 'new_m_ssd_norm_w': 'new_m', 'new_m_w_out': 'new_m', 'new_m_final_norm_w': 'new_m', 'new_v_norm_w': 'new_v', 'new_v_w_in': 'new_v', 'new_v_conv_w': 'new_v', 'new_v_conv_b': 'new_v', 'new_v_dt_bias': 'new_v', 'new_v_a_log': 'new_v', 'new_v_d_skip': 'new_v', 'new_v_ssd_norm_w': 'new_v', 'new_v_w_out': 'new_v', 'new_v_final_norm_w': 'new_v'}


def _forward(args):
    return _fwd_reference(*[args[k] for k in FWD_PARAMS])


def _output_shape():
    def fwd():
        inp = _fwd_setup_inputs(0)
        return _fwd_reference(*[inp[k] for k in FWD_PARAMS])
    out = _jax.eval_shape(fwd)
    return out.shape, out.dtype

N_MICROBATCH = 1
ADAM_LR = 0.001
ADAM_B1 = 0.9
ADAM_B2 = 0.999
ADAM_EPS = 1e-08
ADAM_WD = 0.01
ADAM_STEP = 10
PER_EXAMPLE_BATCH_AXIS = {'x': 0, 'loss_target': 0}
SHARED_INPUTS = []
_WEIGHT_DTYPES = {'norm_w': _jnp.float32, 'w_in': _jnp.float32, 'conv_w': _jnp.float32, 'conv_b': _jnp.float32, 'dt_bias': _jnp.float32, 'a_log': _jnp.float32, 'd_skip': _jnp.float32, 'ssd_norm_w': _jnp.float32, 'w_out': _jnp.float32, 'final_norm_w': _jnp.float32}
MOMENT_SCALE = {'norm_w': 1.217941e-01, 'w_in': 4.738122e-02, 'conv_w': 6.494020e-02, 'conv_b': 8.858756e-02, 'dt_bias': 1.449403e-01, 'a_log': 2.128553e-01, 'd_skip': 3.794916e-01, 'ssd_norm_w': 7.617850e-02, 'w_out': 7.972036e-02, 'final_norm_w': 3.203760e+01}


def _to_microbatches(a, axis):
    t = _jnp.moveaxis(a, axis, 0)
    t = t.reshape((N_MICROBATCH, t.shape[0] // N_MICROBATCH) + t.shape[1:])
    return _jnp.moveaxis(t, 1, axis + 1)


def setup_inputs(seed: int = 0) -> dict:
    inp = _fwd_setup_inputs(seed)
    key = _jax.random.fold_in(_jax.random.key(seed), 7919)
    shape, _ = _output_shape()
    out = dict(inp)
    out["loss_target"] = _jax.random.normal(_jax.random.fold_in(key, 0), shape, _jnp.float32)
    for i, name in enumerate(TWIN_WEIGHTS):
        w = inp[name].astype(_jnp.float32)
        if MOMENT_SCALE is None:
            s = _jnp.sqrt(_jnp.mean(_jnp.square(w)) + 1e-30)
        else:
            s = MOMENT_SCALE[name]
        km, kv = _jax.random.split(_jax.random.fold_in(key, i + 1))
        out[name] = w
        out["m_" + name] = s * _jax.random.normal(km, w.shape, _jnp.float32)
        out["v_" + name] = (s * s) * _jax.random.uniform(kv, w.shape, _jnp.float32, 0.5, 1.5)
    if N_MICROBATCH > 1:
        for name, axis in PER_EXAMPLE_BATCH_AXIS.items():
            out[name] = _to_microbatches(out[name], axis)
    return {'x': out['x'], 'norm_w': out['norm_w'], 'w_in': out['w_in'], 'conv_w': out['conv_w'], 'conv_b': out['conv_b'], 'dt_bias': out['dt_bias'], 'a_log': out['a_log'], 'd_skip': out['d_skip'], 'ssd_norm_w': out['ssd_norm_w'], 'w_out': out['w_out'], 'final_norm_w': out['final_norm_w'], 'loss_target': out['loss_target'], 'm_norm_w': out['m_norm_w'], 'm_w_in': out['m_w_in'], 'm_conv_w': out['m_conv_w'], 'm_conv_b': out['m_conv_b'], 'm_dt_bias': out['m_dt_bias'], 'm_a_log': out['m_a_log'], 'm_d_skip': out['m_d_skip'], 'm_ssd_norm_w': out['m_ssd_norm_w'], 'm_w_out': out['m_w_out'], 'm_final_norm_w': out['m_final_norm_w'], 'v_norm_w': out['v_norm_w'], 'v_w_in': out['v_w_in'], 'v_conv_w': out['v_conv_w'], 'v_conv_b': out['v_conv_b'], 'v_dt_bias': out['v_dt_bias'], 'v_a_log': out['v_a_log'], 'v_d_skip': out['v_d_skip'], 'v_ssd_norm_w': out['v_ssd_norm_w'], 'v_w_out': out['v_w_out'], 'v_final_norm_w': out['v_final_norm_w']}


def _loss(weights, diff, rest, loss_target):
    with _jax.named_scope("forward"):
        args = {**rest, TWIN_DIFF_INPUT: diff, **{k: w.astype(_WEIGHT_DTYPES[k]) for k, w in weights.items()}}
        y = _forward(args)
    with _jax.named_scope("loss_head"):
        err = _jnp.square(y.astype(_jnp.float32) - loss_target)
        return 0.5 * _jnp.sum(_jnp.mean(err, axis=-1)) if err.ndim else 0.5 * err


def _adamw(w, g, m, v):
    m = ADAM_B1 * m + (1.0 - ADAM_B1) * g
    v = ADAM_B2 * v + (1.0 - ADAM_B2) * _jnp.square(g)
    m_hat = m / (1.0 - ADAM_B1 ** ADAM_STEP)
    v_hat = v / (1.0 - ADAM_B2 ** ADAM_STEP)
    delta = -ADAM_LR * (m_hat / (_jnp.sqrt(v_hat) + ADAM_EPS) + ADAM_WD * w)
    return delta, m, v


def reference(x, norm_w, w_in, conv_w, conv_b, dt_bias, a_log, d_skip, ssd_norm_w, w_out, final_norm_w, loss_target, m_norm_w, m_w_in, m_conv_w, m_conv_b, m_dt_bias, m_a_log, m_d_skip, m_ssd_norm_w, m_w_out, m_final_norm_w, v_norm_w, v_w_in, v_conv_w, v_conv_b, v_dt_bias, v_a_log, v_d_skip, v_ssd_norm_w, v_w_out, v_final_norm_w):
    given = dict(x=x, norm_w=norm_w, w_in=w_in, conv_w=conv_w, conv_b=conv_b, dt_bias=dt_bias, a_log=a_log, d_skip=d_skip, ssd_norm_w=ssd_norm_w, w_out=w_out, final_norm_w=final_norm_w, loss_target=loss_target, m_norm_w=m_norm_w, m_w_in=m_w_in, m_conv_w=m_conv_w, m_conv_b=m_conv_b, m_dt_bias=m_dt_bias, m_a_log=m_a_log, m_d_skip=m_d_skip, m_ssd_norm_w=m_ssd_norm_w, m_w_out=m_w_out, m_final_norm_w=m_final_norm_w, v_norm_w=v_norm_w, v_w_in=v_w_in, v_conv_w=v_conv_w, v_conv_b=v_conv_b, v_dt_bias=v_dt_bias, v_a_log=v_a_log, v_d_skip=v_d_skip, v_ssd_norm_w=v_ssd_norm_w, v_w_out=v_w_out, v_final_norm_w=v_final_norm_w)
    weights = {n: given[n] for n in TWIN_WEIGHTS}
    shared = {n: given[n] for n in SHARED_INPUTS}
    per_example = {n: given[n] for n in ['x']}
    grad_fn = _jax.value_and_grad(_loss, argnums=(0, 1))

    def one_microbatch(ex, loss_target):
        ex = dict(ex)
        diff = ex.pop(TWIN_DIFF_INPUT)
        return grad_fn(weights, diff, {**shared, **ex}, loss_target)

    if N_MICROBATCH == 1:
        loss, (grad_w, grad_x) = one_microbatch(per_example, given["loss_target"])
    else:
        def body(carry, xs):
            loss_sum, grad_sum = carry
            l_k, (gw_k, gx_k) = one_microbatch(xs[0], xs[1])
            with _jax.named_scope("update"):
                return (loss_sum + l_k, _jax.tree.map(_jnp.add, grad_sum, gw_k)), gx_k

        init = (_jnp.zeros((), _jnp.float32), _jax.tree.map(_jnp.zeros_like, weights))
        (loss, grad_w), grad_x = _jax.lax.scan(body, init, (per_example, given["loss_target"]))
    with _jax.named_scope("update"):
        delta_w, new_m, new_v = {}, {}, {}
        for n in TWIN_WEIGHTS:
            delta_w[n], new_m[n], new_v[n] = _adamw(weights[n], grad_w[n], given["m_" + n], given["v_" + n])
    return (loss, grad_x, *[grad_w[n] for n in TWIN_WEIGHTS], *[delta_w[n] for n in TWIN_WEIGHTS],
            *[new_m[n] for n in TWIN_WEIGHTS], *[new_v[n] for n in TWIN_WEIGHTS])
```

```python
import functools
import math
from typing import NamedTuple

import jax
import jax.numpy as jnp
from jax import lax
from jax.experimental import pallas as pl
from jax.experimental.pallas import tpu as pltpu

F32, BF16 = jnp.float32, jnp.bfloat16
SDS = jax.ShapeDtypeStruct
EPS = 1e-6
LANES = 128
SUBLANES = 8
VMEM_BYTES = 64 * 2 ** 20
N_DEV = 8
SSD_HEAD_DIM = 64
SSD_STATE = 128
SSD_CONV = 4
SBA_HEAD_DIM = 128
ADAM_LR, ADAM_B1, ADAM_B2, ADAM_EPS, ADAM_WD, ADAM_STEP = 0.001, 0.9, 0.999, 1e-08, 0.01, 10
SKIP_SUM = 110.0
HIGHEST = lax.Precision.HIGHEST
MESH = pl.DeviceIdType.MESH


class Dims(NamedTuple):
    L: int
    D: int
    SW: int
    G: int
    AW: int

    @property
    def NH(self): return self.SW // SSD_HEAD_DIM
    @property
    def R(self): return self.NH // self.G
    @property
    def GW(self): return self.SW // self.G
    @property
    def CD(self): return self.SW + 2 * self.G * SSD_STATE
    @property
    def AH(self): return self.AW // SBA_HEAD_DIM
    @property
    def q_off(self): return self.SW + self.CD
    @property
    def k_off(self): return self.q_off + self.AW
    @property
    def v_off(self): return self.q_off + 2 * self.AW
    @property
    def g_off(self): return self.q_off + 3 * self.AW
    @property
    def dt_off(self): return self.q_off + 4 * self.AW
    @property
    def NP(self): return self.dt_off + LANES


def _pick(n, target, mult):
    t = (min(target, n) // mult) * mult
    while t >= mult:
        if n % t == 0:
            return t
        t -= mult
    return n


def _cp(sem, vmem_est):
    limit = int(min(max(vmem_est * 5 // 4 + (4 << 20), 32 << 20), VMEM_BYTES - (8 << 20)))
    return pltpu.CompilerParams(dimension_semantics=sem, vmem_limit_bytes=limit)


def _sigmoid(x):
    return 1.0 / (1.0 + jnp.exp(-x))


def _softplus(x):
    return jnp.maximum(x, 0.0) + jnp.log(1.0 + jnp.exp(-jnp.abs(x)))


def _nbytes(shape, dtype):
    return math.prod(shape) * jnp.dtype(dtype).itemsize


_DOT_DIMS = {"nn": ((1,), (0,)), "nt": ((1,), (1,)), "tn": ((0,), (0,))}


def mm(a, b, mode, *, tm, tn, tk, name, res=None, b_outer=False):
    if mode == "nn":
        (M, K), N = a.shape, b.shape[1]
    elif mode == "nt":
        (M, K), N = a.shape, b.shape[0]
    else:
        (K, M), N = a.shape, b.shape[1]
    tm, tn, tk = _pick(M, tm, LANES), _pick(N, tn, LANES), _pick(K, tk, LANES)
    nk = K // tk

    def ij(p0, p1):
        return (p1, p0) if b_outer else (p0, p1)

    if mode == "tn":
        a_spec = pl.BlockSpec((tk, tm), lambda p0, p1, k: (k, ij(p0, p1)[0]))
    else:
        a_spec = pl.BlockSpec((tm, tk), lambda p0, p1, k: (ij(p0, p1)[0], k))
    if mode == "nt":
        b_spec = pl.BlockSpec((tn, tk), lambda p0, p1, k: (ij(p0, p1)[1], k))
    else:
        b_spec = pl.BlockSpec((tk, tn), lambda p0, p1, k: (k, ij(p0, p1)[1]))
    o_spec = pl.BlockSpec((tm, tn), lambda p0, p1, k: ij(p0, p1))
    dims = (_DOT_DIMS[mode], ((), ()))

    def body(*refs):
        a_ref, b_ref, o_ref = refs[0], refs[1], refs[-1]
        part = lax.dot_general(a_ref[...].astype(BF16), b_ref[...].astype(BF16), dims,
                               preferred_element_type=F32)
        if res is not None:
            first = part + refs[2][...]
        else:
            first = part
        if nk == 1:
            o_ref[...] = first
        else:
            k = pl.program_id(2)

            @pl.when(k == 0)
            def _():
                o_ref[...] = first

            @pl.when(k > 0)
            def _():
                o_ref[...] += part

    grid = (N // tn, M // tm, nk) if b_outer else (M // tm, N // tn, nk)
    ins, specs = [a, b], [a_spec, b_spec]
    if res is not None:
        ins.append(res)
        specs.append(o_spec)
    est = 2 * (tm * tk * a.dtype.itemsize + tk * tn * b.dtype.itemsize + tm * tn * 4 * (2 if res is not None else 1))
    est += tm * tk * 2 + tk * tn * 2 + tm * tn * 4
    return pl.pallas_call(
        body, name=name, grid=grid, in_specs=specs, out_specs=o_spec,
        out_shape=SDS((M, N), F32),
        compiler_params=_cp(("arbitrary", "arbitrary", "arbitrary"), est))(*ins)


def rms_fwd(x, nw, name):
    L, D = x.shape
    tm = _pick(L, 512, SUBLANES)

    def body(x_ref, w_ref, h_ref):
        xx = x_ref[...]
        r = lax.rsqrt(jnp.mean(xx * xx, axis=-1, keepdims=True) + EPS)
        h_ref[...] = (xx * r * w_ref[...]).astype(BF16)

    return pl.pallas_call(
        body, name=name, grid=(L // tm,),
        in_specs=[pl.BlockSpec((tm, D), lambda i: (i, 0)), pl.BlockSpec((1, D), lambda i: (0, 0))],
        out_specs=pl.BlockSpec((tm, D), lambda i: (i, 0)), out_shape=SDS((L, D), BF16),
        compiler_params=_cp(("arbitrary",), 2 * tm * D * 6))(x, nw)


def rms_bwd(dh, x, nw, dres, name):
    L, D = x.shape
    tm = _pick(L, 256, SUBLANES)

    def body(dh_ref, x_ref, w_ref, dr_ref, dx_ref, dw_ref):
        @pl.when(pl.program_id(0) == 0)
        def _():
            dw_ref[...] = jnp.zeros_like(dw_ref)

        xx, d = x_ref[...], dh_ref[...]
        r = lax.rsqrt(jnp.mean(xx * xx, axis=-1, keepdims=True) + EPS)
        xh = xx * r
        dw_ref[0:1, :] += jnp.sum(d * xh, axis=0, keepdims=True)
        dxh = d * w_ref[...]
        dx_ref[...] = dr_ref[...] + r * (dxh - xh * jnp.mean(dxh * xh, axis=-1, keepdims=True))

    row = pl.BlockSpec((tm, D), lambda i: (i, 0))
    return pl.pallas_call(
        body, name=name, grid=(L // tm,),
        in_specs=[row, row, pl.BlockSpec((1, D), lambda i: (0, 0)), row],
        out_specs=[row, pl.BlockSpec((SUBLANES, D), lambda i: (0, 0))],
        out_shape=[SDS((L, D), F32), SDS((SUBLANES, D), F32)],
        compiler_params=_cp(("arbitrary",), 2 * tm * D * 16))(dh, x, nw, dres)


def loss_head(h, fw, tgt, name):
    L, D = h.shape
    tm = _pick(L, 256, SUBLANES)

    def body(h_ref, w_ref, t_ref, dh_ref, dw_ref, ls_ref):
        @pl.when(pl.program_id(0) == 0)
        def _():
            dw_ref[...] = jnp.zeros_like(dw_ref)
            ls_ref[...] = jnp.zeros_like(ls_ref)

        xx = h_ref[...]
        r = lax.rsqrt(jnp.mean(xx * xx, axis=-1, keepdims=True) + EPS)
        xh = xx * r
        err = xh * w_ref[...] - t_ref[...]
        per_tok = jnp.mean(err * err, axis=-1, keepdims=True)
        ls_ref[...] += jnp.broadcast_to(0.5 * jnp.sum(per_tok, axis=0, keepdims=True), ls_ref.shape)
        dy = err * (1.0 / D)
        dw_ref[0:1, :] += jnp.sum(dy * xh, axis=0, keepdims=True)
        dxh = dy * w_ref[...]
        dh_ref[...] = r * (dxh - xh * jnp.mean(dxh * xh, axis=-1, keepdims=True))

    row = pl.BlockSpec((tm, D), lambda i: (i, 0))
    return pl.pallas_call(
        body, name=name, grid=(L // tm,),
        in_specs=[row, pl.BlockSpec((1, D), lambda i: (0, 0)), row],
        out_specs=[row, pl.BlockSpec((SUBLANES, D), lambda i: (0, 0)),
                   pl.BlockSpec((SUBLANES, LANES), lambda i: (0, 0))],
        out_shape=[SDS((L, D), F32), SDS((SUBLANES, D), F32), SDS((SUBLANES, LANES), F32)],
        compiler_params=_cp(("arbitrary",), 2 * tm * D * 12))(h, fw, tgt)


def _shifted(u, edge, s, back):
    n = u.shape[0]
    row = lax.broadcasted_iota(jnp.int32, (SUBLANES, u.shape[1]), 0)
    if back:
        r = pltpu.roll(u, s, 0)
        head = jnp.where(row < s, pltpu.roll(edge, s, 0), r[0:SUBLANES])
        return jnp.concatenate([head, r[SUBLANES:]], axis=0)
    r = pltpu.roll(u, n - s, 0)
    tail = jnp.where(row >= SUBLANES - s, pltpu.roll(edge, SUBLANES - s, 0), r[n - SUBLANES:])
    return jnp.concatenate([r[:n - SUBLANES], tail], axis=0)


def _conv_pre(u, prev, w, b):
    acc = b + w[SSD_CONV - 1:SSD_CONV] * u
    taps = [u]
    for s in range(1, SSD_CONV):
        us = _shifted(u, prev, s, True)
        taps.append(us)
        acc = acc + w[SSD_CONV - 1 - s:SSD_CONV - s] * us
    return acc, taps


def _conv_specs(dm, tm, tc, col0):
    rb = tm // SUBLANES
    u_spec = pl.BlockSpec((tm, tc), lambda j, i: (i, col0 + j))
    prev_spec = pl.BlockSpec((SUBLANES, tc), lambda j, i: (jnp.maximum(i * rb - 1, 0), col0 + j))
    w_spec = pl.BlockSpec((SSD_CONV, tc), lambda j, i: (0, j))
    b_spec = pl.BlockSpec((1, tc), lambda j, i: (0, j))
    return u_spec, prev_spec, w_spec, b_spec


def conv_fwd(proj, cw, cb, dm, name):
    L, CD = dm.L, dm.CD
    tm, tc = _pick(L, 512, SUBLANES), _pick(math.gcd(CD, dm.SW), 512, LANES)
    u_spec, prev_spec, w_spec, b_spec = _conv_specs(dm, tm, tc, dm.SW // tc)

    def body(u_ref, p_ref, w_ref, b_ref, o_ref):
        prev = jnp.where(pl.program_id(1) == 0, 0.0, p_ref[...])
        c, _ = _conv_pre(u_ref[...], prev, w_ref[...], b_ref[...])
        o_ref[...] = c * _sigmoid(c)

    return pl.pallas_call(
        body, name=name, grid=(CD // tc, L // tm),
        in_specs=[u_spec, prev_spec, w_spec, b_spec],
        out_specs=pl.BlockSpec((tm, tc), lambda j, i: (i, j)), out_shape=SDS((L, CD), F32),
        compiler_params=_cp(("arbitrary", "arbitrary"), 12 * tm * tc * 4))(proj, proj, cw, cb)


def conv_bwd_pre(proj, dxa, cw, cb, dm, name):
    L, CD = dm.L, dm.CD
    tm, tc = _pick(L, 512, SUBLANES), _pick(math.gcd(CD, dm.SW), 512, LANES)
    u_spec, prev_spec, w_spec, b_spec = _conv_specs(dm, tm, tc, dm.SW // tc)

    def body(u_ref, p_ref, d_ref, w_ref, b_ref, dc_ref, g_ref):
        @pl.when(pl.program_id(1) == 0)
        def _():
            g_ref[...] = jnp.zeros_like(g_ref)

        prev = jnp.where(pl.program_id(1) == 0, 0.0, p_ref[...])
        c, taps = _conv_pre(u_ref[...], prev, w_ref[...], b_ref[...])
        sg = _sigmoid(c)
        dc = d_ref[...] * (sg * (1.0 + c * (1.0 - sg)))
        dc_ref[...] = dc
        for s in range(SSD_CONV):
            g_ref[SSD_CONV - 1 - s:SSD_CONV - s, :] += jnp.sum(dc * taps[s], axis=0, keepdims=True)
        g_ref[SSD_CONV:SSD_CONV + 1, :] += jnp.sum(dc, axis=0, keepdims=True)

    blk = pl.BlockSpec((tm, tc), lambda j, i: (i, j))
    return pl.pallas_call(
        body, name=name, grid=(CD // tc, L // tm),
        in_specs=[u_spec, prev_spec, blk, w_spec, b_spec],
        out_specs=[blk, pl.BlockSpec((SUBLANES, tc), lambda j, i: (0, j))],
        out_shape=[SDS((L, CD), F32), SDS((SUBLANES, CD), F32)],
        compiler_params=_cp(("arbitrary", "arbitrary"), 16 * tm * tc * 4))(proj, proj, dxa, cw, cb)


def conv_bwd_in(dc, cw, dm, name):
    L, CD = dm.L, dm.CD
    tm, tc = _pick(L, 512, SUBLANES), _pick(CD, 512, LANES)
    rb, nrow = tm // SUBLANES, L // SUBLANES
    ni = L // tm

    def body(d_ref, n_ref, w_ref, o_ref):
        nxt = jnp.where(pl.program_id(1) == ni - 1, 0.0, n_ref[...])
        dc_, w = d_ref[...], w_ref[...]
        acc = w[SSD_CONV - 1:SSD_CONV] * dc_
        for s in range(1, SSD_CONV):
            acc = acc + w[SSD_CONV - 1 - s:SSD_CONV - s] * _shifted(dc_, nxt, s, False)
        o_ref[...] = acc.astype(BF16)

    blk = pl.BlockSpec((tm, tc), lambda j, i: (i, j))
    return pl.pallas_call(
        body, name=name, grid=(CD // tc, ni),
        in_specs=[blk, pl.BlockSpec((SUBLANES, tc), lambda j, i: (jnp.minimum((i + 1) * rb, nrow - 1), j)),
                  pl.BlockSpec((SSD_CONV, tc), lambda j, i: (0, j))],
        out_specs=blk, out_shape=SDS((L, CD), BF16),
        compiler_params=_cp(("arbitrary", "arbitrary"), 10 * tm * tc * 4))(dc, dc, cw)


def _ssd_chunk(dm):
    return _pick(dm.L, 256, LANES)


def _ssd_common(xa_refs, dtr_ref, par_ref, Q):
    x_ref, b_ref, c_ref = xa_refs
    par = par_ref[0]
    dt = _softplus(dtr_ref[0] + par[0:1])
    a = -jnp.exp(par[1:2])
    da = dt * a
    rows = lax.broadcasted_iota(jnp.int32, (Q, Q), 0)
    cols = lax.broadcasted_iota(jnp.int32, (Q, Q), 1)
    tri = rows >= cols
    trif = tri.astype(F32)
    acum = jnp.dot(trif, da, precision=HIGHEST, preferred_element_type=F32)
    triu = (rows <= cols).astype(F32)
    acum_t = lax.dot_general(da, triu, (_DOT_DIMS["tn"], ((), ())), precision=HIGHEST,
                             preferred_element_type=F32)
    bm, cm = b_ref[...].astype(BF16), c_ref[...].astype(BF16)
    gm = lax.dot_general(cm, bm, (_DOT_DIMS["nt"], ((), ())), preferred_element_type=F32)
    return par, dt, a, tri, triu, acum, acum_t, bm, cm, gm


def _ssd_head(r, Q, x, dt, tri, acum, acum_t, gm):
    P = SSD_HEAD_DIM
    col = acum[:, r:r + 1]
    row = acum_t[r:r + 1, :]
    lam = jnp.where(tri, jnp.exp(jnp.minimum(col - row, 0.0)), 0.0)
    m = gm * lam
    xh = x[:, r * P:(r + 1) * P]
    xs = xh * dt[:, r:r + 1]
    a_last = acum_t[r:r + 1, Q - 1:Q]
    return col, lam, m, xh, xs, a_last


def _ssd_specs(dm, Q, rev):
    nc = dm.L // Q
    cc = (lambda c: nc - 1 - c) if rev else (lambda c: c)
    nb = dm.SW // SSD_STATE
    x_spec = pl.BlockSpec((Q, dm.GW), lambda g, c: (cc(c), g))
    b_spec = pl.BlockSpec((Q, SSD_STATE), lambda g, c: (cc(c), nb + g))
    c_spec = pl.BlockSpec((Q, SSD_STATE), lambda g, c: (cc(c), nb + dm.G + g))
    dt_spec = pl.BlockSpec((1, Q, LANES), lambda g, c: (g, cc(c), 0))
    par_spec = pl.BlockSpec((1, SUBLANES, LANES), lambda g, c: (g, 0, 0))
    h_spec = pl.BlockSpec((1, 1, dm.GW, SSD_STATE), lambda g, c: (cc(c), g, 0, 0))
    return x_spec, b_spec, c_spec, dt_spec, par_spec, h_spec


def ssd_fwd(xa, dtr, par, dm, name):
    Q = _ssd_chunk(dm)
    nc, P, R = dm.L // Q, SSD_HEAD_DIM, dm.R
    x_spec, b_spec, c_spec, dt_spec, par_spec, h_spec = _ssd_specs(dm, Q, False)

    def body(x_ref, b_ref, c_ref, dtr_ref, par_ref, y_ref, hp_ref, h_scr):
        @pl.when(pl.program_id(1) == 0)
        def _():
            h_scr[...] = jnp.zeros_like(h_scr)

        par, dt, a, tri, triu, acum, acum_t, bm, cm, gm = _ssd_common((x_ref, b_ref, c_ref), dtr_ref, par_ref, Q)
        x = x_ref[...]
        hp = h_scr[...]
        hp_ref[0, 0] = hp
        for r in range(R):
            col, lam, m, xh, xs, a_last = _ssd_head(r, Q, x, dt, tri, acum, acum_t, gm)
            hpr = hp[r * P:(r + 1) * P, :]
            ydiag = jnp.dot(m.astype(BF16), xs.astype(BF16), preferred_element_type=F32)
            yoff = jnp.exp(col) * lax.dot_general(cm, hpr.astype(BF16), (_DOT_DIMS["nt"], ((), ())),
                                                  preferred_element_type=F32)
            dte = jnp.exp(a_last - col)
            st = lax.dot_general((xs * dte).astype(BF16), bm, (_DOT_DIMS["tn"], ((), ())),
                                 preferred_element_type=F32)
            h_scr[r * P:(r + 1) * P, :] = jnp.exp(a_last) * hpr + st
            y_ref[:, r * P:(r + 1) * P] = ydiag + yoff + par[2:3, r:r + 1] * xh

    return pl.pallas_call(
        body, name=name, grid=(dm.G, nc),
        in_specs=[x_spec, b_spec, c_spec, dt_spec, par_spec],
        out_specs=[x_spec, h_spec],
        out_shape=[SDS((dm.L, dm.SW), F32), SDS((nc, dm.G, dm.GW, SSD_STATE), F32)],
        scratch_shapes=[pltpu.VMEM((dm.GW, SSD_STATE), F32)],
        compiler_params=_cp(("arbitrary", "arbitrary"), 24 << 20))(xa, xa, xa, dtr, par)


def ssd_bwd(xa, dtr, par, hprev, dyo, dm, name):
    Q = _ssd_chunk(dm)
    nc, P, R, N = dm.L // Q, SSD_HEAD_DIM, dm.R, SSD_STATE
    x_spec, b_spec, c_spec, dt_spec, par_spec, h_spec = _ssd_specs(dm, Q, True)
    bc_spec = pl.BlockSpec((Q, N), lambda g, c: (nc - 1 - c, g))
    NT, TN = (_DOT_DIMS["nt"], ((), ())), (_DOT_DIMS["tn"], ((), ()))

    def body(x_ref, b_ref, c_ref, dtr_ref, par_ref, hp_ref, dy_ref,
             dx_ref, db_ref, dc_ref, ddt_ref, gs_ref, dh_scr):
        @pl.when(pl.program_id(1) == 0)
        def _():
            dh_scr[...] = jnp.zeros_like(dh_scr)
            gs_ref[...] = jnp.zeros_like(gs_ref)

        par, dt, a, tri, triu, acum, acum_t, bm, cm, gm = _ssd_common((x_ref, b_ref, c_ref), dtr_ref, par_ref, Q)
        x, dy, hp, dhn = x_ref[...], dy_ref[...], hp_ref[0, 0], dh_scr[...]
        lane = lax.broadcasted_iota(jnp.int32, (Q, LANES), 1)
        rowi = lax.broadcasted_iota(jnp.int32, (Q, LANES), 0)
        d_acum = jnp.zeros((Q, LANES), F32)
        d_dt = jnp.zeros((Q, LANES), F32)
        d_skip = jnp.zeros((1, LANES), F32)
        dgsum = jnp.zeros((Q, Q), F32)
        dye_all, xse_all, span_rows = [], [], []
        rq = lax.broadcasted_iota(jnp.int32, (Q, Q), 0)
        cq = lax.broadcasted_iota(jnp.int32, (Q, Q), 1)
        ue = (rq < cq).astype(BF16)
        for r in range(R):
            col, lam, m, xh, xs, a_last = _ssd_head(r, Q, x, dt, tri, acum, acum_t, gm)
            sl = slice(r * P, (r + 1) * P)
            dyh, hpr, dhr = dy[:, sl], hp[sl, :], dhn[sl, :]
            mb, xsb, dyb = m.astype(BF16), xs.astype(BF16), dyh.astype(BF16)
            e_a = jnp.exp(col)
            dte = jnp.exp(a_last - col)
            yoff = e_a * lax.dot_general(cm, hpr.astype(BF16), NT, preferred_element_type=F32)
            bdh = dte * lax.dot_general(bm, dhr.astype(BF16), NT, preferred_element_type=F32)
            dxs = lax.dot_general(mb, dyb, TN, preferred_element_type=F32) + bdh
            dm_ = lax.dot_general(dyb, xsb, NT, preferred_element_type=F32)
            dgsum = dgsum + dm_ * lam
            dye, xse = (dyh * e_a).astype(BF16), (xs * dte).astype(BF16)
            dye_all.append(dye)
            xse_all.append(xse)
            dh_scr[sl, :] = (jnp.exp(a_last) * dhr
                             + lax.dot_general(dye, cm, TN, preferred_element_type=F32))
            before = jnp.dot((dm_ * m).astype(BF16), ue, preferred_element_type=F32)
            span_rows.append(jnp.sum(jnp.where(tri, before, 0.0), axis=0, keepdims=True))
            da_col = jnp.sum(dyh * yoff - xs * bdh, axis=1, keepdims=True)
            da_last = (jnp.sum(xs * bdh, axis=(0, 1), keepdims=True)
                       + jnp.exp(a_last) * jnp.sum(dhr * hpr, axis=(0, 1), keepdims=True))
            d_acum = d_acum + jnp.where(lane == r, da_col + jnp.where(rowi == Q - 1, da_last, 0.0), 0.0)
            d_dt = d_dt + jnp.where(lane == r, jnp.sum(dxs * xh, axis=1, keepdims=True), 0.0)
            d_skip = d_skip + jnp.where(lane[0:1] == r, jnp.sum(dyh * xh, axis=(0, 1), keepdims=True), 0.0)
            dx_ref[:, sl] = dxs * dt[:, r:r + 1] + par[2:3, r:r + 1] * dyh
        dgb = dgsum.astype(BF16)
        dye_c = jnp.concatenate(dye_all, axis=1)
        xse_c = jnp.concatenate(xse_all, axis=1)
        dc_ref[...] = (jnp.dot(dgb, bm, preferred_element_type=F32)
                       + jnp.dot(dye_c, hp.astype(BF16), preferred_element_type=F32))
        db_ref[...] = (lax.dot_general(dgb, cm, TN, preferred_element_type=F32)
                       + jnp.dot(xse_c, dhn.astype(BF16), preferred_element_type=F32))
        span = jnp.concatenate(span_rows + [jnp.zeros((LANES - R, Q), F32)], axis=0)
        d_da = (jnp.dot(triu, d_acum, precision=HIGHEST, preferred_element_type=F32)
                + lax.dot_general((rq == cq).astype(F32), span, NT, precision=HIGHEST,
                                  preferred_element_type=F32))
        d_dt = d_dt + d_da * a
        d_raw = d_dt * _sigmoid(dtr_ref[0] + par[0:1])
        ddt_ref[0] = d_raw
        gs_ref[0, 0:1, :] += jnp.sum(d_raw, axis=0, keepdims=True)
        gs_ref[0, 1:2, :] += jnp.sum(d_da * dt, axis=0, keepdims=True) * a
        gs_ref[0, 2:3, :] += d_skip

    return pl.pallas_call(
        body, name=name, grid=(dm.G, nc),
        in_specs=[x_spec, b_spec, c_spec, dt_spec, par_spec, h_spec, x_spec],
        out_specs=[x_spec, bc_spec, bc_spec, dt_spec, par_spec],
        out_shape=[SDS((dm.L, dm.SW), F32), SDS((dm.L, dm.G * N), F32), SDS((dm.L, dm.G * N), F32),
                   SDS((dm.G, dm.L, LANES), F32), SDS((dm.G, SUBLANES, LANES), F32)],
        scratch_shapes=[pltpu.VMEM((dm.GW, N), F32)],
        compiler_params=_cp(("arbitrary", "arbitrary"), 28 << 20))(xa, xa, xa, dtr, par, hprev, dyo)


def _attn_tile(dm):
    return _pick(dm.L, 256, LANES)


def _split(v):
    hi = v.astype(BF16)
    return hi, (v - hi.astype(F32)).astype(BF16)


def attn_fwd(proj, dm, name):
    L, T, DH, AH = dm.L, _attn_tile(dm), SBA_HEAD_DIM, dm.AH
    nq = L // T
    scale = 1.0 / math.sqrt(DH)
    NT = (_DOT_DIMS["nt"], ((), ()))

    def body(q_ref, k_ref, v_ref, o_ref, tot_ref, nb_ref, ks, vs, o_scr, acc_scr):
        h, i = pl.program_id(0), pl.program_id(1)

        @pl.when(i == 0)
        def _():
            ks[...] = k_ref[...].astype(BF16)
            vs[...] = v_ref[...].astype(BF16)

        qb = q_ref[...].astype(BF16)
        rows = lax.broadcasted_iota(jnp.int32, (T, T), 0)
        cols = lax.broadcasted_iota(jnp.int32, (T, T), 1)
        causal = cols < rows
        u_rev = (rows >= cols).astype(BF16)
        o_scr[...] = jnp.zeros_like(o_scr)
        acc_scr[...] = jnp.zeros_like(acc_scr)

        def step(j, masked):
            sl = pl.ds(pl.multiple_of(j * T, T), T)
            z = lax.dot_general(qb, ks[sl, :], NT, preferred_element_type=F32) * scale
            sp = _softplus(z)
            if masked:
                sp = jnp.where(causal, sp, 0.0)
            hi, lo = _split(sp)
            cs = (jnp.dot(hi, u_rev, preferred_element_type=F32)
                  + jnp.dot(lo, u_rev, preferred_element_type=F32))
            acc = acc_scr[...]
            w = jnp.exp(z - cs - acc)
            if masked:
                w = jnp.where(causal, w, 0.0)
            o_scr[...] += jnp.dot(w.astype(BF16), vs[sl, :], preferred_element_type=F32)
            acc = acc + cs[:, 0:1]
            acc_scr[...] = acc
            return jnp.min(acc)

        low = step(i, True)

        def cond(c):
            return jnp.logical_and(c[0] >= 0, c[1] < SKIP_SUM)

        def loop(c):
            return c[0] - 1, step(c[0], False)

        j_end, _ = lax.while_loop(cond, loop, (i - 1, low))
        o_ref[...] = o_scr[...]
        tot_ref[0] = jnp.broadcast_to(acc_scr[...], (T, LANES))
        nb_ref[h, i] = i - j_end

    kv = lambda off: pl.BlockSpec((L, DH), lambda h, i: (0, off // DH + h))
    est = 2 * 2 * L * DH * 4 + 2 * L * DH * 2 + 12 * T * T * 4
    return pl.pallas_call(
        body, name=name, grid=(AH, nq),
        in_specs=[pl.BlockSpec((T, DH), lambda h, i: (i, dm.q_off // DH + h)), kv(dm.k_off), kv(dm.v_off)],
        out_specs=[pl.BlockSpec((T, DH), lambda h, i: (i, h)),
                   pl.BlockSpec((1, T, LANES), lambda h, i: (h, i, 0)),
                   pl.BlockSpec(memory_space=pltpu.SMEM)],
        out_shape=[SDS((L, dm.AW), F32), SDS((AH, L, LANES), F32), SDS((AH, nq), jnp.int32)],
        scratch_shapes=[pltpu.VMEM((L, DH), BF16), pltpu.VMEM((L, DH), BF16),
                        pltpu.VMEM((T, DH), F32), pltpu.VMEM((T, 1), F32)],
        compiler_params=_cp(("arbitrary", "arbitrary"), est))(proj, proj, proj)


def attn_bwd(proj, dyo, tot, nblk, dm, name):
    L, T, DH, AH = dm.L, _attn_tile(dm), SBA_HEAD_DIM, dm.AH
    nq = L // T
    scale = 1.0 / math.sqrt(DH)
    NT, TN = (_DOT_DIMS["nt"], ((), ())), (_DOT_DIMS["tn"], ((), ()))

    def body(nb_ref, q_ref, k_ref, v_ref, do_ref, tot_ref, dq_ref, dk_ref, dv_ref, ks, vs, dq_scr, p_scr, e_scr):
        h, i = pl.program_id(0), pl.program_id(1)

        @pl.when(i == 0)
        def _():
            ks[...] = k_ref[...].astype(BF16)
            vs[...] = v_ref[...].astype(BF16)
            dk_ref[...] = jnp.zeros_like(dk_ref)
            dv_ref[...] = jnp.zeros_like(dv_ref)

        qb = q_ref[...].astype(BF16)
        dob = do_ref[...].astype(BF16)
        tot_c = tot_ref[0][:, 0:1]
        rows = lax.broadcasted_iota(jnp.int32, (T, T), 0)
        cols = lax.broadcasted_iota(jnp.int32, (T, T), 1)
        causal = cols < rows
        u_fwd = (rows <= cols).astype(BF16)
        dq_scr[...] = jnp.zeros_like(dq_scr)
        p_scr[...] = jnp.zeros_like(p_scr)
        e_scr[...] = jnp.zeros_like(e_scr)

        def step(j, masked):
            sl = pl.ds(pl.multiple_of(j * T, T), T)
            kb, vb = ks[sl, :], vs[sl, :]
            z = lax.dot_general(qb, kb, NT, preferred_element_type=F32) * scale
            sp = _softplus(z)
            if masked:
                sp = jnp.where(causal, sp, 0.0)
            hi, lo = _split(sp)
            pin = (jnp.dot(hi, u_fwd, preferred_element_type=F32)
                   + jnp.dot(lo, u_fwd, preferred_element_type=F32))
            cs = (tot_c - p_scr[...]) - (pin - sp)
            w = jnp.exp(z - cs)
            if masked:
                w = jnp.where(causal, w, 0.0)
            e = lax.dot_general(dob, vb, NT, preferred_element_type=F32) * w
            ehi, elo = _split(e)
            fin = (jnp.dot(ehi, u_fwd, preferred_element_type=F32)
                   + jnp.dot(elo, u_fwd, preferred_element_type=F32))
            dz = (e - jnp.exp(z - sp) * (e_scr[...] + fin)) * scale
            if masked:
                dz = jnp.where(causal, dz, 0.0)
            dzb = dz.astype(BF16)
            dq_scr[...] += jnp.dot(dzb, kb, preferred_element_type=F32)
            dk_ref[sl, :] += lax.dot_general(dzb, qb, TN, preferred_element_type=F32)
            dv_ref[sl, :] += lax.dot_general(w.astype(BF16), dob, TN, preferred_element_type=F32)
            p_scr[...] += pin[:, T - 1:T]
            e_scr[...] += fin[:, T - 1:T]

        def loop(j, carry):
            step(j, False)
            return carry

        lax.fori_loop(i - nb_ref[h, i] + 1, i, loop, 0)
        step(i, True)
        dq_ref[...] = dq_scr[...]

    kv = lambda off: pl.BlockSpec((L, DH), lambda h, i, nb: (0, off // DH + h))
    qblk = lambda off: pl.BlockSpec((T, DH), lambda h, i, nb: (i, off // DH + h))
    acc = pl.BlockSpec((L, DH), lambda h, i, nb: (0, h))
    est = 2 * 2 * L * DH * 4 * 2 + 2 * L * DH * 2 + 16 * T * T * 4
    grid_spec = pltpu.PrefetchScalarGridSpec(
        num_scalar_prefetch=1, grid=(AH, nq),
        in_specs=[qblk(dm.q_off), kv(dm.k_off), kv(dm.v_off), qblk(dm.SW),
                  pl.BlockSpec((1, T, LANES), lambda h, i, nb: (h, i, 0))],
        out_specs=[qblk(0), acc, acc],
        scratch_shapes=[pltpu.VMEM((L, DH), BF16), pltpu.VMEM((L, DH), BF16),
                        pltpu.VMEM((T, DH), F32), pltpu.VMEM((T, 1), F32), pltpu.VMEM((T, 1), F32)])
    return pl.pallas_call(
        body, name=name, grid_spec=grid_spec,
        out_shape=[SDS((L, dm.AW), F32)] * 3,
        compiler_params=_cp(("arbitrary", "arbitrary"), est))(nblk, proj, proj, proj, dyo, tot)


def _gate_specs(dm, tm, order):
    GW, G = dm.GW, dm.G
    ix = (lambda a, b: (a, b)) if order == "ij" else (lambda a, b: (b, a))

    def spec(colfn):
        return pl.BlockSpec((tm, GW), lambda p0, p1: (ix(p0, p1)[0], colfn(ix(p0, p1)[1])))

    y_spec = spec(lambda j: jnp.minimum(j, G - 1))
    o_spec = spec(lambda j: jnp.maximum(j - G, 0))
    zg_spec = spec(lambda j: jnp.where(j < G, j, dm.g_off // GW + j - G))
    w_spec = pl.BlockSpec((1, GW), lambda p0, p1: (0, jnp.minimum(ix(p0, p1)[1], G - 1)))
    full = spec(lambda j: j)
    return y_spec, o_spec, zg_spec, w_spec, full


def gate_fwd(y, o, proj, snw, dm, name):
    L, GW, G = dm.L, dm.GW, dm.G
    tm = _pick(L, 512, SUBLANES)
    ncol = (dm.SW + dm.AW) // GW
    y_spec, o_spec, zg_spec, w_spec, full = _gate_specs(dm, tm, "ij")

    def body(y_ref, o_ref, zg_ref, w_ref, m_ref):
        j = pl.program_id(1)
        zg = zg_ref[...]
        gate = zg * _sigmoid(zg)

        @pl.when(j < G)
        def _():
            yz = y_ref[...] * gate
            r = lax.rsqrt(jnp.mean(yz * yz, axis=-1, keepdims=True) + EPS)
            m_ref[...] = (yz * r * w_ref[...]).astype(BF16)

        @pl.when(j >= G)
        def _():
            m_ref[...] = (o_ref[...] * gate).astype(BF16)

    return pl.pallas_call(
        body, name=name, grid=(L // tm, ncol),
        in_specs=[y_spec, o_spec, zg_spec, w_spec], out_specs=full,
        out_shape=SDS((L, dm.SW + dm.AW), BF16),
        compiler_params=_cp(("arbitrary", "arbitrary"), 2 * tm * GW * 16))(y, o, proj, snw)


def gate_bwd(dmix, y, o, proj, snw, dm, name):
    L, GW, G = dm.L, dm.GW, dm.G
    tm = _pick(L, 512, SUBLANES)
    W = dm.SW + dm.AW
    y_spec, o_spec, zg_spec, w_spec, full = _gate_specs(dm, tm, "ji")

    def body(d_ref, y_ref, o_ref, zg_ref, w_ref, dyo_ref, dzg_ref, dw_ref):
        j = pl.program_id(0)

        @pl.when(pl.program_id(1) == 0)
        def _():
            dw_ref[...] = jnp.zeros_like(dw_ref)

        zg, d = zg_ref[...], d_ref[...]
        sg = _sigmoid(zg)
        gate = zg * sg
        dgate = sg * (1.0 + zg * (1.0 - sg))

        @pl.when(j < G)
        def _():
            yv = y_ref[...]
            yz = yv * gate
            r = lax.rsqrt(jnp.mean(yz * yz, axis=-1, keepdims=True) + EPS)
            nrm = yz * r
            dw_ref[0:1, :] += jnp.sum(d * nrm, axis=0, keepdims=True)
            dn = d * w_ref[...]
            dyz = r * (dn - nrm * jnp.mean(dn * nrm, axis=-1, keepdims=True))
            dyo_ref[...] = dyz * gate
            dzg_ref[...] = (dyz * yv * dgate).astype(BF16)

        @pl.when(j >= G)
        def _():
            dyo_ref[...] = d * gate
            dzg_ref[...] = (d * o_ref[...] * dgate).astype(BF16)

    return pl.pallas_call(
        body, name=name, grid=(W // GW, L // tm),
        in_specs=[full, y_spec, o_spec, zg_spec, w_spec],
        out_specs=[full, full, pl.BlockSpec((SUBLANES, GW), lambda j, i: (0, j))],
        out_shape=[SDS((L, W), F32), SDS((L, W), BF16), SDS((SUBLANES, W), F32)],
        compiler_params=_cp(("arbitrary", "arbitrary"), 2 * tm * GW * 24))(dmix, y, o, proj, snw)


def adamw(parts, w, m, v, name):
    R, C = w.shape
    tr = _pick(R, max(SUBLANES, (1 << 18) // C // SUBLANES * SUBLANES), SUBLANES)
    c1, c2 = 1.0 - ADAM_B1 ** ADAM_STEP, 1.0 - ADAM_B2 ** ADAM_STEP

    def body(p_ref, w_ref, m_ref, v_ref, g_ref, d_ref, m2_ref, v2_ref):
        g = p_ref[0].astype(F32)
        for s in range(1, N_DEV):
            g = g + p_ref[s].astype(F32)
        m2 = ADAM_B1 * m_ref[...] + (1.0 - ADAM_B1) * g
        v2 = ADAM_B2 * v_ref[...] + (1.0 - ADAM_B2) * (g * g)
        g_ref[...] = g
        m2_ref[...] = m2
        v2_ref[...] = v2
        d_ref[...] = -ADAM_LR * ((m2 / c1) / (jnp.sqrt(v2 / c2) + ADAM_EPS) + ADAM_WD * w_ref[...])

    blk = pl.BlockSpec((tr, C), lambda i: (i, 0))
    return pl.pallas_call(
        body, name=name, grid=(R // tr,),
        in_specs=[pl.BlockSpec((N_DEV, tr, C), lambda i: (0, i, 0)), blk, blk, blk],
        out_specs=[blk] * 4, out_shape=[SDS((R, C), F32)] * 4,
        compiler_params=_cp(("arbitrary",), 2 * tr * C * (N_DEV * 4 + 28)))(parts, w, m, v)


def _peer(rel):
    x, y, c = lax.axis_index("x"), lax.axis_index("y"), lax.axis_index("c")
    px = x ^ ((rel >> 2) & 1)
    py = y ^ ((rel >> 1) & 1)
    pc = c ^ (rel & 1)
    return (px, py, pc), 4 * px + 2 * py + pc


def exchange(arrays, scatter, name):
    n = len(arrays)

    def body(*refs):
        srcs, dsts = refs[:n], refs[n:2 * n]
        send_sems, recv_sems, local_sems = refs[2 * n:]
        _, me = _peer(0)

        def src_of(a, idx):
            return srcs[a].at[idx] if scatter else srcs[a]

        def remote(a, rel):
            peer, pidx = _peer(rel)
            return pltpu.make_async_remote_copy(
                src_ref=src_of(a, pidx), dst_ref=dsts[a].at[me],
                send_sem=send_sems.at[a * (N_DEV - 1) + rel - 1],
                recv_sem=recv_sems.at[a * (N_DEV - 1) + rel - 1],
                device_id=peer, device_id_type=MESH)

        def arrival(a, rel):
            peer, pidx = _peer(rel)
            return pltpu.make_async_remote_copy(
                src_ref=src_of(a, me), dst_ref=dsts[a].at[pidx],
                send_sem=send_sems.at[a * (N_DEV - 1) + rel - 1],
                recv_sem=recv_sems.at[a * (N_DEV - 1) + rel - 1],
                device_id=peer, device_id_type=MESH)

        own = [pltpu.make_async_copy(src_of(a, me), dsts[a].at[me], local_sems.at[a]) for a in range(n)]
        for cp in own:
            cp.start()
        sends = [remote(a, rel) for rel in range(1, N_DEV) for a in range(n)]
        for cp in sends:
            cp.start()
        for rel in range(1, N_DEV):
            for a in range(n):
                arrival(a, rel).wait_recv()
        for cp in sends:
            cp.wait_send()
        for cp in own:
            cp.wait()

    hbm = pl.BlockSpec(memory_space=pltpu.HBM)
    if scatter:
        out_shape = [SDS(a.shape, a.dtype) for a in arrays]
    else:
        out_shape = [SDS((N_DEV,) + a.shape, a.dtype) for a in arrays]
    return pl.pallas_call(
        body, name=name, in_specs=[hbm] * n, out_specs=[hbm] * n, out_shape=out_shape,
        scratch_shapes=[pltpu.SemaphoreType.DMA((n * (N_DEV - 1),)),
                        pltpu.SemaphoreType.DMA((n * (N_DEV - 1),)),
                        pltpu.SemaphoreType.DMA((n,))],
        compiler_params=pltpu.CompilerParams(has_side_effects=True))(*arrays)


class LayerParams(NamedTuple):
    nw: jax.Array
    w_in: jax.Array
    cw: jax.Array
    cb: jax.Array
    par: jax.Array
    snw: jax.Array
    w_out: jax.Array


def _dt_by_group(cols, dm):
    t = cols.reshape(dm.L, dm.G, dm.R).transpose(1, 0, 2)
    return jnp.pad(t, ((0, 0), (0, 0), (0, LANES - dm.R)))


def layer_fwd(x, p, dm, tag):
    h = rms_fwd(x, p.nw, f"rms_fwd{tag}")
    proj = mm(h, p.w_in, "nn", tm=512, tn=1920, tk=dm.D, name=f"in_proj{tag}", b_outer=True)
    xa = conv_fwd(proj, p.cw, p.cb, dm, f"conv_fwd{tag}")
    dtr = _dt_by_group(proj[:, dm.dt_off:dm.dt_off + dm.NH], dm)
    y, hprev = ssd_fwd(xa, dtr, p.par, dm, f"ssd_fwd{tag}")
    o, tot, nblk = attn_fwd(proj, dm, f"attn_fwd{tag}")
    mix = gate_fwd(y, o, proj, p.snw, dm, f"gate_fwd{tag}")
    xn = mm(mix, p.w_out, "nn", tm=512, tn=1024, tk=dm.SW + dm.AW, name=f"out_proj{tag}", res=x)
    return xn, (x, h, proj, xa, dtr, y, hprev, o, tot, nblk, mix)


def layer_bwd(dxn, saved, p, dm, tag):
    x, h, proj, xa, dtr, y, hprev, o, tot, nblk, mix = saved
    dmix = mm(dxn, p.w_out, "nt", tm=512, tn=1024, tk=dm.D, name=f"d_mix{tag}")
    dw_out = mm(mix, dxn, "tn", tm=1024, tn=dm.D, tk=512, name=f"dw_out{tag}")
    dyo, dzg, dsnw = gate_bwd(dmix, y, o, proj, p.snw, dm, f"gate_bwd{tag}")
    dq, dk, dv = attn_bwd(proj, dyo, tot, nblk, dm, f"attn_bwd{tag}")
    dxs, db, dc, ddt, gsm = ssd_bwd(xa, dtr, p.par, hprev, dyo, dm, f"ssd_bwd{tag}")
    dxa = jnp.concatenate([dxs, db, dc], axis=1)
    dcv, gconv = conv_bwd_pre(proj, dxa, p.cw, p.cb, dm, f"conv_bwd_pre{tag}")
    dxbc = conv_bwd_in(dcv, p.cw, dm, f"conv_bwd_in{tag}")
    ddt_cols = ddt[:, :, :dm.R].transpose(1, 0, 2).reshape(dm.L, dm.NH)
    ddt_blk = jnp.pad(ddt_cols, ((0, 0), (0, LANES - dm.NH))).astype(BF16)
    dproj = jnp.concatenate(
        [dzg[:, :dm.SW], dxbc, dq.astype(BF16), dk.astype(BF16), dv.astype(BF16), dzg[:, dm.SW:], ddt_blk], axis=1)
    dh = mm(dproj, p.w_in, "nt", tm=512, tn=dm.D, tk=1920, name=f"d_h{tag}")
    dw_in = mm(h, dproj, "tn", tm=dm.D, tn=960, tk=512, name=f"dw_in{tag}")
    dx, dnw = rms_bwd(dh, x, p.nw, dxn, f"rms_bwd{tag}")
    small = dict(norm_w=dnw[0], conv_w=gconv[:SSD_CONV], conv_b=gconv[SSD_CONV],
                 dt_bias=gsm[:, 0, :dm.R].reshape(dm.NH), a_log=gsm[:, 1, :dm.R].reshape(dm.NH),
                 d_skip=gsm[:, 2, :dm.R].reshape(dm.NH), ssd_norm_w=dsnw[0, :dm.SW])
    return dx, dw_in, dw_out, small


SMALL = ("norm_w", "conv_b", "dt_bias", "a_log", "d_skip", "ssd_norm_w")


def _to_mine(w, dm):
    a, b = dm.SW + dm.CD, dm.SW + dm.CD + dm.NH
    pad = jnp.zeros((w.shape[0], LANES - dm.NH), w.dtype)
    return jnp.concatenate([w[:, :a], w[:, b:], w[:, a:b], pad], axis=1)


def _from_mine(w, dm):
    a = dm.SW + dm.CD
    return jnp.concatenate([w[:, :a], w[:, dm.dt_off:dm.dt_off + dm.NH], w[:, a:dm.dt_off]], axis=1)


def _pack(pieces):
    flat = jnp.concatenate([p.reshape(-1) for p in pieces])
    rows = -(-flat.shape[0] // LANES)
    rows = -(-rows // SUBLANES) * SUBLANES
    return jnp.pad(flat, (0, rows * LANES - flat.shape[0])).reshape(rows, LANES)


def _unpack(buf, shapes):
    flat, out, at = buf.reshape(-1), [], 0
    for s in shapes:
        n = math.prod(s)
        out.append(flat[at:at + n].reshape(s))
        at += n
    return out


def kernel(x, norm_w, w_in, conv_w, conv_b, dt_bias, a_log, d_skip, ssd_norm_w, w_out, final_norm_w, loss_target, m_norm_w, m_w_in, m_conv_w, m_conv_b, m_dt_bias, m_a_log, m_d_skip, m_ssd_norm_w, m_w_out, m_final_norm_w, v_norm_w, v_w_in, v_conv_w, v_conv_b, v_dt_bias, v_a_log, v_d_skip, v_ssd_norm_w, v_w_out, v_final_norm_w):
    depth, D = norm_w.shape
    L = x.shape[1]
    NH = dt_bias.shape[1]
    SW = NH * SSD_HEAD_DIM
    CD = conv_b.shape[1]
    dm = Dims(L=L, D=D, SW=SW, G=(CD - SW) // (2 * SSD_STATE), AW=w_out.shape[1] * N_DEV - SW)
    ncol, csh, osh = w_in.shape[2], conv_w.shape[2], w_out.shape[1]
    me = 4 * lax.axis_index("x") + 2 * lax.axis_index("y") + lax.axis_index("c")

    g_in, g_out, g_cw = exchange([w_in.astype(BF16), w_out.astype(BF16), conv_w], False, "gather_weights")
    params = []
    for l in range(depth):
        full_in = g_in[:, l].transpose(1, 0, 2).reshape(D, N_DEV * ncol)
        full_cw = g_cw[:, l].transpose(1, 0, 2).reshape(SSD_CONV, CD)
        par = jnp.stack([dt_bias[l], a_log[l], d_skip[l]]).reshape(3, dm.G, dm.R).transpose(1, 0, 2)
        par = jnp.pad(par, ((0, 0), (0, SUBLANES - 3), (0, LANES - dm.R)))
        params.append(LayerParams(
            nw=norm_w[l][None], w_in=_to_mine(full_in, dm), cw=full_cw, cb=conv_b[l][None], par=par,
            snw=ssd_norm_w[l][None], w_out=g_out[:, l].reshape(N_DEV * osh, D)))

    h = x[0]
    saved = []
    for l in range(depth):
        h, s = layer_fwd(h, params[l], dm, f"_{l}")
        saved.append(s)
    dh, dfw, ls = loss_head(h, final_norm_w[None], loss_target[0], "loss_head")
    loss = lax.psum(ls[0, 0], ("x", "y", "c"))
    gw_in, gw_out, smalls = [None] * depth, [None] * depth, [None] * depth
    for l in reversed(range(depth)):
        dh, gw_in[l], gw_out[l], smalls[l] = layer_bwd(dh, saved[l], params[l], dm, f"_{l}")
    grad_x = dh[None]

    p_in = jnp.stack([_from_mine(g, dm).astype(BF16).reshape(D, N_DEV, ncol).transpose(1, 0, 2)
                      for g in gw_in], axis=1).reshape(N_DEV, depth * D, ncol)
    p_out = jnp.stack([g.astype(BF16).reshape(N_DEV, osh, D) for g in gw_out], axis=1).reshape(N_DEV, depth * osh, D)
    rep = [jnp.stack([s[k] for s in smalls]) for k in SMALL] + [dfw[0]]
    rep_shapes = [r.shape for r in rep]
    p_rep = _pack(rep)
    p_cw = jnp.stack([s["conv_w"] for s in smalls]).reshape(depth * SSD_CONV, N_DEV, csh).transpose(1, 0, 2)
    nrep = p_rep.shape[0]
    r_in, r_out, r_rep, r_cw = exchange(
        [p_in, p_out, jnp.broadcast_to(p_rep[None], (N_DEV, nrep, LANES)), p_cw], True, "exchange_grads")

    out_in = adamw(r_in, w_in.reshape(depth * D, ncol), m_w_in.reshape(depth * D, ncol),
                   v_w_in.reshape(depth * D, ncol), "adamw_w_in")
    out_out = adamw(r_out, w_out.reshape(depth * osh, D), m_w_out.reshape(depth * osh, D),
                    v_w_out.reshape(depth * osh, D), "adamw_w_out")
    out_cw = adamw(r_cw, conv_w.reshape(depth * SSD_CONV, csh), m_conv_w.reshape(depth * SSD_CONV, csh),
                   v_conv_w.reshape(depth * SSD_CONV, csh), "adamw_conv_w")
    rep_w = [norm_w, conv_b, dt_bias, a_log, d_skip, ssd_norm_w, final_norm_w]
    rep_m = [m_norm_w, m_conv_b, m_dt_bias, m_a_log, m_d_skip, m_ssd_norm_w, m_final_norm_w]
    rep_v = [v_norm_w, v_conv_b, v_dt_bias, v_a_log, v_d_skip, v_ssd_norm_w, v_final_norm_w]
    out_rep = adamw(r_rep, _pack(rep_w), _pack(rep_m), _pack(rep_v), "adamw_replicated")

    outs = {}
    for kind, i in (("grad", 0), ("delta", 1), ("new_m", 2), ("new_v", 3)):
        r = dict(zip(SMALL + ("final_norm_w",), _unpack(out_rep[i], rep_shapes)))
        r["w_in"] = out_in[i].reshape(w_in.shape)
        r["w_out"] = out_out[i].reshape(w_out.shape)
        r["conv_w"] = out_cw[i].reshape(conv_w.shape)
        outs[kind] = r
    order = ("norm_w", "w_in", "conv_w", "conv_b", "dt_bias", "a_log", "d_skip", "ssd_norm_w", "w_out", "final_norm_w")
    return (loss, grad_x, *[outs[k][n] for k in ("grad", "delta", "new_m", "new_v") for n in order])
```

```python
import functools
import math
from typing import NamedTuple

import jax
import jax.numpy as jnp
from jax import lax
from jax.experimental import pallas as pl
from jax.experimental.pallas import tpu as pltpu

F32, BF16 = jnp.float32, jnp.bfloat16
SDS = jax.ShapeDtypeStruct
EPS = 1e-6
LANES = 128
SUBLANES = 8
VMEM_BYTES = 64 * 2 ** 20
N_DEV = 8
SSD_HEAD_DIM = 64
SSD_STATE = 128
SSD_CONV = 4
SBA_HEAD_DIM = 128
ADAM_LR, ADAM_B1, ADAM_B2, ADAM_EPS, ADAM_WD, ADAM_STEP = 0.001, 0.9, 0.999, 1e-08, 0.01, 10
SKIP_SUM = 110.0
HIGHEST = lax.Precision.HIGHEST
MESH = pl.DeviceIdType.MESH


class Dims(NamedTuple):
    L: int
    D: int
    SW: int
    G: int
    AW: int

    @property
    def NH(self): return self.SW // SSD_HEAD_DIM
    @property
    def R(self): return self.NH // self.G
    @property
    def GW(self): return self.SW // self.G
    @property
    def CD(self): return self.SW + 2 * self.G * SSD_STATE
    @property
    def AH(self): return self.AW // SBA_HEAD_DIM
    @property
    def q_off(self): return self.SW + self.CD
    @property
    def k_off(self): return self.q_off + self.AW
    @property
    def v_off(self): return self.q_off + 2 * self.AW
    @property
    def g_off(self): return self.q_off + 3 * self.AW
    @property
    def dt_off(self): return self.q_off + 4 * self.AW
    @property
    def NP(self): return self.dt_off + LANES


def _pick(n, target, mult):
    t = (min(target, n) // mult) * mult
    while t >= mult:
        if n % t == 0:
            return t
        t -= mult
    return n


def _cp(sem, vmem_est):
    limit = int(min(max(vmem_est * 5 // 4 + (4 << 20), 32 << 20), VMEM_BYTES - (8 << 20)))
    return pltpu.CompilerParams(dimension_semantics=sem, vmem_limit_bytes=limit)


def _sigmoid(x):
    return 1.0 / (1.0 + jnp.exp(-x))


def _softplus(x):
    return jnp.maximum(x, 0.0) + jnp.log(1.0 + jnp.exp(-jnp.abs(x)))


def _nbytes(shape, dtype):
    return math.prod(shape) * jnp.dtype(dtype).itemsize


_DOT_DIMS = {"nn": ((1,), (0,)), "nt": ((1,), (1,)), "tn": ((0,), (0,))}


def mm(a, b, mode, *, tm, tn, tk, name, res=None, b_outer=False):
    if mode == "nn":
        (M, K), N = a.shape, b.shape[1]
    elif mode == "nt":
        (M, K), N = a.shape, b.shape[0]
    else:
        (K, M), N = a.shape, b.shape[1]
    tm, tn, tk = _pick(M, tm, LANES), _pick(N, tn, LANES), _pick(K, tk, LANES)
    nk = K // tk

    def ij(p0, p1):
        return (p1, p0) if b_outer else (p0, p1)

    if mode == "tn":
        a_spec = pl.BlockSpec((tk, tm), lambda p0, p1, k: (k, ij(p0, p1)[0]))
    else:
        a_spec = pl.BlockSpec((tm, tk), lambda p0, p1, k: (ij(p0, p1)[0], k))
    if mode == "nt":
        b_spec = pl.BlockSpec((tn, tk), lambda p0, p1, k: (ij(p0, p1)[1], k))
    else:
        b_spec = pl.BlockSpec((tk, tn), lambda p0, p1, k: (k, ij(p0, p1)[1]))
    o_spec = pl.BlockSpec((tm, tn), lambda p0, p1, k: ij(p0, p1))
    dims = (_DOT_DIMS[mode], ((), ()))

    def body(*refs):
        a_ref, b_ref, o_ref = refs[0], refs[1], refs[-1]
        part = lax.dot_general(a_ref[...].astype(BF16), b_ref[...].astype(BF16), dims,
                               preferred_element_type=F32)
        if res is not None:
            first = part + refs[2][...]
        else:
            first = part
        if nk == 1:
            o_ref[...] = first
        else:
            k = pl.program_id(2)

            @pl.when(k == 0)
            def _():
                o_ref[...] = first

            @pl.when(k > 0)
            def _():
                o_ref[...] += part

    grid = (N // tn, M // tm, nk) if b_outer else (M // tm, N // tn, nk)
    ins, specs = [a, b], [a_spec, b_spec]
    if res is not None:
        ins.append(res)
        specs.append(o_spec)
    est = 2 * (tm * tk * a.dtype.itemsize + tk * tn * b.dtype.itemsize + tm * tn * 4 * (2 if res is not None else 1))
    est += tm * tk * 2 + tk * tn * 2 + tm * tn * 4
    return pl.pallas_call(
        body, name=name, grid=grid, in_specs=specs, out_specs=o_spec,
        out_shape=SDS((M, N), F32),
        compiler_params=_cp(("arbitrary", "arbitrary", "arbitrary"), est))(*ins)


def rms_fwd(x, nw, name):
    L, D = x.shape
    tm = _pick(L, 512, SUBLANES)

    def body(x_ref, w_ref, h_ref):
        xx = x_ref[...]
        r = lax.rsqrt(jnp.mean(xx * xx, axis=-1, keepdims=True) + EPS)
        h_ref[...] = (xx * r * w_ref[...]).astype(BF16)

    return pl.pallas_call(
        body, name=name, grid=(L // tm,),
        in_specs=[pl.BlockSpec((tm, D), lambda i: (i, 0)), pl.BlockSpec((1, D), lambda i: (0, 0))],
        out_specs=pl.BlockSpec((tm, D), lambda i: (i, 0)), out_shape=SDS((L, D), BF16),
        compiler_params=_cp(("arbitrary",), 2 * tm * D * 6))(x, nw)


def rms_bwd(dh, x, nw, dres, name):
    L, D = x.shape
    tm = _pick(L, 256, SUBLANES)

    def body(dh_ref, x_ref, w_ref, dr_ref, dx_ref, dw_ref):
        @pl.when(pl.program_id(0) == 0)
        def _():
            dw_ref[...] = jnp.zeros_like(dw_ref)

        xx, d = x_ref[...], dh_ref[...]
        r = lax.rsqrt(jnp.mean(xx * xx, axis=-1, keepdims=True) + EPS)
        xh = xx * r
        dw_ref[0:1, :] += jnp.sum(d * xh, axis=0, keepdims=True)
        dxh = d * w_ref[...]
        dx_ref[...] = dr_ref[...] + r * (dxh - xh * jnp.mean(dxh * xh, axis=-1, keepdims=True))

    row = pl.BlockSpec((tm, D), lambda i: (i, 0))
    return pl.pallas_call(
        body, name=name, grid=(L // tm,),
        in_specs=[row, row, pl.BlockSpec((1, D), lambda i: (0, 0)), row],
        out_specs=[row, pl.BlockSpec((SUBLANES, D), lambda i: (0, 0))],
        out_shape=[SDS((L, D), F32), SDS((SUBLANES, D), F32)],
        compiler_params=_cp(("arbitrary",), 2 * tm * D * 16))(dh, x, nw, dres)


def loss_head(h, fw, tgt, name):
    L, D = h.shape
    tm = _pick(L, 256, SUBLANES)

    def body(h_ref, w_ref, t_ref, dh_ref, dw_ref, ls_ref):
        @pl.when(pl.program_id(0) == 0)
        def _():
            dw_ref[...] = jnp.zeros_like(dw_ref)
            ls_ref[...] = jnp.zeros_like(ls_ref)

        xx = h_ref[...]
        r = lax.rsqrt(jnp.mean(xx * xx, axis=-1, keepdims=True) + EPS)
        xh = xx * r
        err = xh * w_ref[...] - t_ref[...]
        per_tok = jnp.mean(err * err, axis=-1, keepdims=True)
        ls_ref[...] += jnp.broadcast_to(0.5 * jnp.sum(per_tok, axis=0, keepdims=True), ls_ref.shape)
        dy = err * (1.0 / D)
        dw_ref[0:1, :] += jnp.sum(dy * xh, axis=0, keepdims=True)
        dxh = dy * w_ref[...]
        dh_ref[...] = r * (dxh - xh * jnp.mean(dxh * xh, axis=-1, keepdims=True))

    row = pl.BlockSpec((tm, D), lambda i: (i, 0))
    return pl.pallas_call(
        body, name=name, grid=(L // tm,),
        in_specs=[row, pl.BlockSpec((1, D), lambda i: (0, 0)), row],
        out_specs=[row, pl.BlockSpec((SUBLANES, D), lambda i: (0, 0)),
                   pl.BlockSpec((SUBLANES, LANES), lambda i: (0, 0))],
        out_shape=[SDS((L, D), F32), SDS((SUBLANES, D), F32), SDS((SUBLANES, LANES), F32)],
        compiler_params=_cp(("arbitrary",), 2 * tm * D * 12))(h, fw, tgt)


def _shifted(u, edge, s, back):
    n = u.shape[0]
    row = lax.broadcasted_iota(jnp.int32, (SUBLANES, u.shape[1]), 0)
    if back:
        r = pltpu.roll(u, s, 0)
        head = jnp.where(row < s, pltpu.roll(edge, s, 0), r[0:SUBLANES])
        return jnp.concatenate([head, r[SUBLANES:]], axis=0)
    r = pltpu.roll(u, n - s, 0)
    tail = jnp.where(row >= SUBLANES - s, pltpu.roll(edge, SUBLANES - s, 0), r[n - SUBLANES:])
    return jnp.concatenate([r[:n - SUBLANES], tail], axis=0)


def _conv_pre(u, prev, w, b):
    acc = b + w[SSD_CONV - 1:SSD_CONV] * u
    taps = [u]
    for s in range(1, SSD_CONV):
        us = _shifted(u, prev, s, True)
        taps.append(us)
        acc = acc + w[SSD_CONV - 1 - s:SSD_CONV - s] * us
    return acc, taps


def _conv_specs(dm, tm, tc, col0):
    rb = tm // SUBLANES
    u_spec = pl.BlockSpec((tm, tc), lambda j, i: (i, col0 + j))
    prev_spec = pl.BlockSpec((SUBLANES, tc), lambda j, i: (jnp.maximum(i * rb - 1, 0), col0 + j))
    w_spec = pl.BlockSpec((SSD_CONV, tc), lambda j, i: (0, j))
    b_spec = pl.BlockSpec((1, tc), lambda j, i: (0, j))
    return u_spec, prev_spec, w_spec, b_spec


def conv_fwd(proj, cw, cb, dm, name):
    L, CD = dm.L, dm.CD
    tm, tc = _pick(L, 512, SUBLANES), _pick(math.gcd(CD, dm.SW), 512, LANES)
    u_spec, prev_spec, w_spec, b_spec = _conv_specs(dm, tm, tc, dm.SW // tc)

    def body(u_ref, p_ref, w_ref, b_ref, o_ref):
        prev = jnp.where(pl.program_id(1) == 0, 0.0, p_ref[...])
        c, _ = _conv_pre(u_ref[...], prev, w_ref[...], b_ref[...])
        o_ref[...] = c * _sigmoid(c)

    return pl.pallas_call(
        body, name=name, grid=(CD // tc, L // tm),
        in_specs=[u_spec, prev_spec, w_spec, b_spec],
        out_specs=pl.BlockSpec((tm, tc), lambda j, i: (i, j)), out_shape=SDS((L, CD), F32),
        compiler_params=_cp(("arbitrary", "arbitrary"), 12 * tm * tc * 4))(proj, proj, cw, cb)


def conv_bwd_pre(proj, dxs, db, dcm, cw, cb, dm, name):
    L, CD = dm.L, dm.CD
    gn = dm.G * SSD_STATE
    tm, tc = _pick(L, 512, SUBLANES), _pick(math.gcd(gn, dm.SW), 512, LANES)
    u_spec, prev_spec, w_spec, b_spec = _conv_specs(dm, tm, tc, dm.SW // tc)
    nx, nb = dm.SW // tc, gn // tc

    def body(u_ref, p_ref, dx_ref, db_ref, dcm_ref, w_ref, b_ref, dc_ref, g_ref):
        j = pl.program_id(0)

        @pl.when(pl.program_id(1) == 0)
        def _():
            g_ref[...] = jnp.zeros_like(g_ref)

        prev = jnp.where(pl.program_id(1) == 0, 0.0, p_ref[...])
        c, taps = _conv_pre(u_ref[...], prev, w_ref[...], b_ref[...])
        sg = _sigmoid(c)
        d = jnp.where(j < nx, dx_ref[...], jnp.where(j < nx + nb, db_ref[...], dcm_ref[...]))
        dc = d * (sg * (1.0 + c * (1.0 - sg)))
        dc_ref[...] = dc
        for s in range(SSD_CONV):
            g_ref[SSD_CONV - 1 - s:SSD_CONV - s, :] += jnp.sum(dc * taps[s], axis=0, keepdims=True)
        g_ref[SSD_CONV:SSD_CONV + 1, :] += jnp.sum(dc, axis=0, keepdims=True)

    blk = pl.BlockSpec((tm, tc), lambda j, i: (i, j))
    part = lambda lo, n: pl.BlockSpec((tm, tc), lambda j, i: (i, jnp.clip(j - lo, 0, n - 1)))
    return pl.pallas_call(
        body, name=name, grid=(CD // tc, L // tm),
        in_specs=[u_spec, prev_spec, part(0, nx), part(nx, nb), part(nx + nb, nb), w_spec, b_spec],
        out_specs=[blk, pl.BlockSpec((SUBLANES, tc), lambda j, i: (0, j))],
        out_shape=[SDS((L, CD), F32), SDS((SUBLANES, CD), F32)],
        compiler_params=_cp(("arbitrary", "arbitrary"), 20 * tm * tc * 4))(proj, proj, dxs, db, dcm, cw, cb)


def conv_bwd_in(dc, cw, dm, name):
    L, CD = dm.L, dm.CD
    tm, tc = _pick(L, 512, SUBLANES), _pick(CD, 512, LANES)
    rb, nrow = tm // SUBLANES, L // SUBLANES
    ni = L // tm

    def body(d_ref, n_ref, w_ref, o_ref):
        nxt = jnp.where(pl.program_id(1) == ni - 1, 0.0, n_ref[...])
        dc_, w = d_ref[...], w_ref[...]
        acc = w[SSD_CONV - 1:SSD_CONV] * dc_
        for s in range(1, SSD_CONV):
            acc = acc + w[SSD_CONV - 1 - s:SSD_CONV - s] * _shifted(dc_, nxt, s, False)
        o_ref[...] = acc.astype(BF16)

    blk = pl.BlockSpec((tm, tc), lambda j, i: (i, j))
    return pl.pallas_call(
        body, name=name, grid=(CD // tc, ni),
        in_specs=[blk, pl.BlockSpec((SUBLANES, tc), lambda j, i: (jnp.minimum((i + 1) * rb, nrow - 1), j)),
                  pl.BlockSpec((SSD_CONV, tc), lambda j, i: (0, j))],
        out_specs=blk, out_shape=SDS((L, CD), BF16),
        compiler_params=_cp(("arbitrary", "arbitrary"), 10 * tm * tc * 4))(dc, dc, cw)


def _ssd_chunk(dm):
    return _pick(dm.L, 256, LANES)


def _dt_parts(dtr, par):
    return _softplus(dtr + par[0:1]), -jnp.exp(par[1:2])


def _tri(Q):
    rows = lax.broadcasted_iota(jnp.int32, (Q, Q), 0)
    cols = lax.broadcasted_iota(jnp.int32, (Q, Q), 1)
    return rows, cols


def _lanes_to_group(v, g, R, axis):
    n = v.shape[axis]
    return v if g == 0 else pltpu.roll(v, n - g * R, axis)


def ssd_prep(proj, par_all, dm, name):
    Q, G, R = _ssd_chunk(dm), dm.G, dm.R
    nc = dm.L // Q

    def body(dtr_ref, par_ref, dt_ref, ac_ref, at_ref):
        dt, a = _dt_parts(dtr_ref[...], par_ref[...])
        da = dt * a
        rows, cols = _tri(Q)
        acum = jnp.dot((rows >= cols).astype(F32), da, precision=HIGHEST, preferred_element_type=F32)
        acum_t = lax.dot_general(da, (rows <= cols).astype(F32), (_DOT_DIMS["tn"], ((), ())),
                                 precision=HIGHEST, preferred_element_type=F32)
        for g in range(G):
            dt_ref[g] = _lanes_to_group(dt, g, R, 1)
            ac_ref[g] = _lanes_to_group(acum, g, R, 1)
            at_ref[g, 0] = _lanes_to_group(acum_t, g, R, 0)[0:SUBLANES]

    lane_blk = pl.BlockSpec((G, Q, LANES), lambda c: (0, c, 0))
    return pl.pallas_call(
        body, name=name, grid=(nc,),
        in_specs=[pl.BlockSpec((Q, LANES), lambda c: (c, dm.dt_off // LANES)),
                  pl.BlockSpec((SUBLANES, LANES), lambda c: (0, 0))],
        out_specs=[lane_blk, lane_blk, pl.BlockSpec((G, 1, SUBLANES, Q), lambda c: (0, c, 0, 0))],
        out_shape=[SDS((G, dm.L, LANES), F32), SDS((G, dm.L, LANES), F32), SDS((G, nc, SUBLANES, Q), F32)],
        compiler_params=_cp(("arbitrary",), 8 << 20))(proj, par_all)


def _ssd_common(xa_refs, Q):
    _, b_ref, c_ref = xa_refs
    rows, cols = _tri(Q)
    bm, cm = b_ref[...].astype(BF16), c_ref[...].astype(BF16)
    gm = lax.dot_general(cm, bm, (_DOT_DIMS["nt"], ((), ())), preferred_element_type=F32)
    return rows >= cols, bm, cm, gm


def _ssd_head(r, Q, x, dt, tri, acum, acum_t, gm):
    P = SSD_HEAD_DIM
    col = acum[:, r:r + 1]
    row = acum_t[r:r + 1, :]
    lam = jnp.where(tri, jnp.exp(jnp.minimum(col - row, 0.0)), 0.0)
    m = gm * lam
    xh = x[:, r * P:(r + 1) * P]
    xs = xh * dt[:, r:r + 1]
    a_last = acum_t[r:r + 1, Q - 1:Q]
    return col, lam, m, xh, xs, a_last


def _ssd_specs(dm, Q, rev):
    nc = dm.L // Q
    cc = (lambda c: nc - 1 - c) if rev else (lambda c: c)
    nb = dm.SW // SSD_STATE
    x_spec = pl.BlockSpec((Q, dm.GW), lambda g, c: (cc(c), g))
    b_spec = pl.BlockSpec((Q, SSD_STATE), lambda g, c: (cc(c), nb + g))
    c_spec = pl.BlockSpec((Q, SSD_STATE), lambda g, c: (cc(c), nb + dm.G + g))
    dt_spec = pl.BlockSpec((1, Q, LANES), lambda g, c: (g, cc(c), 0))
    at_spec = pl.BlockSpec((1, 1, SUBLANES, Q), lambda g, c: (g, cc(c), 0, 0))
    par_spec = pl.BlockSpec((1, SUBLANES, LANES), lambda g, c: (g, 0, 0))
    h_spec = pl.BlockSpec((1, 1, dm.GW, SSD_STATE), lambda g, c: (cc(c), g, 0, 0))
    return x_spec, b_spec, c_spec, dt_spec, at_spec, par_spec, h_spec


def ssd_fwd(xa, prep, par, dm, name):
    Q = _ssd_chunk(dm)
    nc, P, R = dm.L // Q, SSD_HEAD_DIM, dm.R
    x_spec, b_spec, c_spec, dt_spec, at_spec, par_spec, h_spec = _ssd_specs(dm, Q, False)

    def body(x_ref, b_ref, c_ref, dt_ref, ac_ref, at_ref, par_ref, y_ref, hp_ref, h_scr):
        @pl.when(pl.program_id(1) == 0)
        def _():
            h_scr[...] = jnp.zeros_like(h_scr)

        tri, bm, cm, gm = _ssd_common((x_ref, b_ref, c_ref), Q)
        par, dt, acum, acum_t = par_ref[0], dt_ref[0], ac_ref[0], at_ref[0, 0]
        x = x_ref[...]
        hp = h_scr[...]
        hp_ref[0, 0] = hp
        for r in range(R):
            col, lam, m, xh, xs, a_last = _ssd_head(r, Q, x, dt, tri, acum, acum_t, gm)
            hpr = hp[r * P:(r + 1) * P, :]
            ydiag = jnp.dot(m.astype(BF16), xs.astype(BF16), preferred_element_type=F32)
            yoff = jnp.exp(col) * lax.dot_general(cm, hpr.astype(BF16), (_DOT_DIMS["nt"], ((), ())),
                                                  preferred_element_type=F32)
            dte = jnp.exp(a_last - col)
            st = lax.dot_general((xs * dte).astype(BF16), bm, (_DOT_DIMS["tn"], ((), ())),
                                 preferred_element_type=F32)
            h_scr[r * P:(r + 1) * P, :] = jnp.exp(a_last) * hpr + st
            y_ref[:, r * P:(r + 1) * P] = ydiag + yoff + par[2:3, r:r + 1] * xh

    return pl.pallas_call(
        body, name=name, grid=(dm.G, nc),
        in_specs=[x_spec, b_spec, c_spec, dt_spec, dt_spec, at_spec, par_spec],
        out_specs=[x_spec, h_spec],
        out_shape=[SDS((dm.L, dm.SW), F32), SDS((nc, dm.G, dm.GW, SSD_STATE), F32)],
        scratch_shapes=[pltpu.VMEM((dm.GW, SSD_STATE), F32)],
        compiler_params=_cp(("arbitrary", "arbitrary"), 24 << 20))(xa, xa, xa, *prep, par)


def ssd_bwd(xa, prep, par, hprev, dyo, dm, name):
    Q = _ssd_chunk(dm)
    nc, P, R, N = dm.L // Q, SSD_HEAD_DIM, dm.R, SSD_STATE
    x_spec, b_spec, c_spec, dt_spec, at_spec, par_spec, h_spec = _ssd_specs(dm, Q, True)
    bc_spec = pl.BlockSpec((Q, N), lambda g, c: (nc - 1 - c, g))
    NT, TN = (_DOT_DIMS["nt"], ((), ())), (_DOT_DIMS["tn"], ((), ()))

    def body(x_ref, b_ref, c_ref, dt_ref, ac_ref, at_ref, par_ref, hp_ref, dy_ref,
             dx_ref, db_ref, dc_ref, dac_ref, span_ref, ddt_ref, gs_ref, dh_scr):
        @pl.when(pl.program_id(1) == 0)
        def _():
            dh_scr[...] = jnp.zeros_like(dh_scr)
            gs_ref[...] = jnp.zeros_like(gs_ref)

        tri, bm, cm, gm = _ssd_common((x_ref, b_ref, c_ref), Q)
        par, dt, acum, acum_t = par_ref[0], dt_ref[0], ac_ref[0], at_ref[0, 0]
        x, dy, hp, dhn = x_ref[...], dy_ref[...], hp_ref[0, 0], dh_scr[...]
        lane = lax.broadcasted_iota(jnp.int32, (Q, LANES), 1)
        rowi = lax.broadcasted_iota(jnp.int32, (Q, LANES), 0)
        d_acum = jnp.zeros((Q, LANES), F32)
        d_dt = jnp.zeros((Q, LANES), F32)
        d_skip = jnp.zeros((1, LANES), F32)
        dgsum = jnp.zeros((Q, Q), F32)
        dye_all, xse_all, span_rows = [], [], []
        rq = lax.broadcasted_iota(jnp.int32, (Q, Q), 0)
        cq = lax.broadcasted_iota(jnp.int32, (Q, Q), 1)
        ue = (rq < cq).astype(BF16)
        for r in range(R):
            col, lam, m, xh, xs, a_last = _ssd_head(r, Q, x, dt, tri, acum, acum_t, gm)
            sl = slice(r * P, (r + 1) * P)
            dyh, hpr, dhr = dy[:, sl], hp[sl, :], dhn[sl, :]
            mb, xsb, dyb = m.astype(BF16), xs.astype(BF16), dyh.astype(BF16)
            e_a = jnp.exp(col)
            dte = jnp.exp(a_last - col)
            yoff = e_a * lax.dot_general(cm, hpr.astype(BF16), NT, preferred_element_type=F32)
            bdh = dte * lax.dot_general(bm, dhr.astype(BF16), NT, preferred_element_type=F32)
            dxs = lax.dot_general(mb, dyb, TN, preferred_element_type=F32) + bdh
            dm_ = lax.dot_general(dyb, xsb, NT, preferred_element_type=F32)
            dgsum = dgsum + dm_ * lam
            dye, xse = (dyh * e_a).astype(BF16), (xs * dte).astype(BF16)
            dye_all.append(dye)
            xse_all.append(xse)
            dh_scr[sl, :] = (jnp.exp(a_last) * dhr
                             + lax.dot_general(dye, cm, TN, preferred_element_type=F32))
            before = jnp.dot((dm_ * m).astype(BF16), ue, preferred_element_type=F32)
            span_rows.append(jnp.sum(jnp.where(tri, before, 0.0), axis=0, keepdims=True))
            da_col = jnp.sum(dyh * yoff - xs * bdh, axis=1, keepdims=True)
            da_last = (jnp.sum(xs * bdh, axis=(0, 1), keepdims=True)
                       + jnp.exp(a_last) * jnp.sum(dhr * hpr, axis=(0, 1), keepdims=True))
            d_acum = d_acum + jnp.where(lane == r, da_col + jnp.where(rowi == Q - 1, da_last, 0.0), 0.0)
            d_dt = d_dt + jnp.where(lane == r, jnp.sum(dxs * xh, axis=1, keepdims=True), 0.0)
            d_skip = d_skip + jnp.where(lane[0:1] == r, jnp.sum(dyh * xh, axis=(0, 1), keepdims=True), 0.0)
            dx_ref[:, sl] = dxs * dt[:, r:r + 1] + par[2:3, r:r + 1] * dyh
        dgb = dgsum.astype(BF16)
        dye_c = jnp.concatenate(dye_all, axis=1)
        xse_c = jnp.concatenate(xse_all, axis=1)
        dc_ref[...] = (jnp.dot(dgb, bm, preferred_element_type=F32)
                       + jnp.dot(dye_c, hp.astype(BF16), preferred_element_type=F32))
        db_ref[...] = (lax.dot_general(dgb, cm, TN, preferred_element_type=F32)
                       + jnp.dot(xse_c, dhn.astype(BF16), preferred_element_type=F32))
        dac_ref[0] = d_acum
        ddt_ref[0] = d_dt
        span_ref[0, 0] = jnp.concatenate(span_rows + [jnp.zeros((SUBLANES - R, Q), F32)] * (R < SUBLANES), axis=0)
        gs_ref[0, 2:3, :] += d_skip

    return pl.pallas_call(
        body, name=name, grid=(dm.G, nc),
        in_specs=[x_spec, b_spec, c_spec, dt_spec, dt_spec, at_spec, par_spec, h_spec, x_spec],
        out_specs=[x_spec, bc_spec, bc_spec, dt_spec, at_spec, dt_spec, par_spec],
        out_shape=[SDS((dm.L, dm.SW), F32), SDS((dm.L, dm.G * N), F32), SDS((dm.L, dm.G * N), F32),
                   SDS((dm.G, dm.L, LANES), F32), SDS((dm.G, nc, SUBLANES, Q), F32),
                   SDS((dm.G, dm.L, LANES), F32), SDS((dm.G, SUBLANES, LANES), F32)],
        scratch_shapes=[pltpu.VMEM((dm.GW, N), F32)],
        compiler_params=_cp(("arbitrary", "arbitrary"), 28 << 20))(xa, xa, xa, *prep, par, hprev, dyo)


def ssd_post(proj, par_all, dac, span, ddt, gs, dm, name):
    Q, G, R = _ssd_chunk(dm), dm.G, dm.R
    nc = dm.L // Q
    NT = (_DOT_DIMS["nt"], ((), ()))

    def body(dtr_ref, par_ref, dac_ref, span_ref, ddt_ref, gs_ref, out_ref, acc_ref):
        @pl.when(pl.program_id(0) == 0)
        def _():
            acc_ref[...] = jnp.zeros_like(acc_ref)

        par = par_ref[...]
        dt, a = _dt_parts(dtr_ref[...], par)
        lane = lax.broadcasted_iota(jnp.int32, (Q, LANES), 1)
        row8 = lax.broadcasted_iota(jnp.int32, (SUBLANES, Q), 0)

        def heads(v, g):
            v = jnp.where(lane[:v.shape[0]] < R, v, 0.0)
            return v if g == 0 else pltpu.roll(v, g * R, 1)

        d_acum = sum(heads(dac_ref[g], g) for g in range(G))
        d_dtx = sum(heads(ddt_ref[g], g) for g in range(G))
        d_skip = sum(heads(gs_ref[g][2:3], g) for g in range(G))
        span_t = jnp.zeros((LANES, Q), F32)
        for g in range(G):
            rows_g = jnp.concatenate([jnp.where(row8 < R, span_ref[g, 0], 0.0),
                                      jnp.zeros((LANES - SUBLANES, Q), F32)], axis=0)
            span_t = span_t + (rows_g if g == 0 else pltpu.roll(rows_g, g * R, 0))
        rq, cq = _tri(Q)
        d_da = (jnp.dot((rq <= cq).astype(F32), d_acum, precision=HIGHEST, preferred_element_type=F32)
                + lax.dot_general((rq == cq).astype(F32), span_t, NT, precision=HIGHEST,
                                  preferred_element_type=F32))
        d_raw = (d_dtx + d_da * a) * _sigmoid(dtr_ref[...] + par[0:1])
        out_ref[...] = d_raw.astype(BF16)
        acc_ref[0:1, :] += jnp.sum(d_raw, axis=0, keepdims=True)
        acc_ref[1:2, :] += jnp.sum(d_da * dt, axis=0, keepdims=True) * a
        acc_ref[2:3, :] = d_skip

    lane_blk = pl.BlockSpec((G, Q, LANES), lambda c: (0, c, 0))
    small = pl.BlockSpec((SUBLANES, LANES), lambda c: (0, 0))
    return pl.pallas_call(
        body, name=name, grid=(nc,),
        in_specs=[pl.BlockSpec((Q, LANES), lambda c: (c, dm.dt_off // LANES)), small, lane_blk,
                  pl.BlockSpec((G, 1, SUBLANES, Q), lambda c: (0, c, 0, 0)), lane_blk,
                  pl.BlockSpec((G, SUBLANES, LANES), lambda c: (0, 0, 0))],
        out_specs=[pl.BlockSpec((Q, LANES), lambda c: (c, 0)), small],
        out_shape=[SDS((dm.L, LANES), BF16), SDS((SUBLANES, LANES), F32)],
        compiler_params=_cp(("arbitrary",), 8 << 20))(proj, par_all, dac, span, ddt, gs)


def _attn_tile(dm):
    return _pick(dm.L, 256, LANES)


def _split(v):
    hi = v.astype(BF16)
    return hi, (v - hi.astype(F32)).astype(BF16)


def attn_fwd(proj, dm, name):
    L, T, DH, AH = dm.L, _attn_tile(dm), SBA_HEAD_DIM, dm.AH
    nq = L // T
    scale = 1.0 / math.sqrt(DH)
    NT = (_DOT_DIMS["nt"], ((), ()))

    def body(q_ref, k_ref, v_ref, o_ref, tot_ref, nb_ref, ks, vs, o_scr, acc_scr):
        h, i = pl.program_id(0), pl.program_id(1)

        @pl.when(i == 0)
        def _():
            ks[...] = k_ref[...].astype(BF16)
            vs[...] = v_ref[...].astype(BF16)

        qb = q_ref[...].astype(BF16)
        rows = lax.broadcasted_iota(jnp.int32, (T, T), 0)
        cols = lax.broadcasted_iota(jnp.int32, (T, T), 1)
        causal = cols < rows
        u_rev = (rows >= cols).astype(BF16)

        def scores(j, masked):
            sl = pl.ds(pl.multiple_of(j * T, T), T)
            z = lax.dot_general(qb, ks[sl, :], NT, preferred_element_type=F32) * scale
            sp = _softplus(z)
            if masked:
                sp = jnp.where(causal, sp, 0.0)
            hi, lo = _split(sp)
            cs = (jnp.dot(hi, u_rev, preferred_element_type=F32)
                  + jnp.dot(lo, u_rev, preferred_element_type=F32))
            return sl, z, cs

        def weighted(blk, acc, masked):
            sl, z, cs = blk
            w = jnp.exp(z - cs - acc)
            if masked:
                w = jnp.where(causal, w, 0.0)
            return jnp.dot(w.astype(BF16), vs[sl, :], preferred_element_type=F32), acc + cs[:, 0:1]

        zero = jnp.zeros((T, 1), F32)

        @pl.when(i == 0)
        def _():
            o_scr[...], acc_scr[...] = weighted(scores(i, True), zero, True)

        @pl.when(i > 0)
        def _():
            diag, prev = scores(i, True), scores(i - 1, False)
            pv0, acc1 = weighted(diag, zero, True)
            pv1, acc2 = weighted(prev, acc1, False)
            o_scr[...] = pv0 + pv1
            acc_scr[...] = acc2

        def cond(c):
            return jnp.logical_and(c[0] >= 0, c[1] < SKIP_SUM)

        def loop(c):
            pv, acc = weighted(scores(c[0], False), acc_scr[...], False)
            o_scr[...] += pv
            acc_scr[...] = acc
            return c[0] - 1, jnp.min(acc)

        j_end, _ = lax.while_loop(cond, loop, (jnp.where(i > 0, i - 2, -1), jnp.min(acc_scr[...])))
        o_ref[...] = o_scr[...]
        tot_ref[0] = jnp.broadcast_to(acc_scr[...], (T, LANES))
        nb_ref[h, i] = i - j_end

    kv = lambda off: pl.BlockSpec((L, DH), lambda h, i: (0, off // DH + h))
    est = 2 * 2 * L * DH * 4 + 2 * L * DH * 2 + 12 * T * T * 4
    return pl.pallas_call(
        body, name=name, grid=(AH, nq),
        in_specs=[pl.BlockSpec((T, DH), lambda h, i: (i, dm.q_off // DH + h)), kv(dm.k_off), kv(dm.v_off)],
        out_specs=[pl.BlockSpec((T, DH), lambda h, i: (i, h)),
                   pl.BlockSpec((1, T, LANES), lambda h, i: (h, i, 0)),
                   pl.BlockSpec(memory_space=pltpu.SMEM)],
        out_shape=[SDS((L, dm.AW), F32), SDS((AH, L, LANES), F32), SDS((AH, nq), jnp.int32)],
        scratch_shapes=[pltpu.VMEM((L, DH), BF16), pltpu.VMEM((L, DH), BF16),
                        pltpu.VMEM((T, DH), F32), pltpu.VMEM((T, 1), F32)],
        compiler_params=_cp(("arbitrary", "arbitrary"), est))(proj, proj, proj)


def attn_bwd(proj, dyo, tot, nblk, dm, name):
    L, T, DH, AH = dm.L, _attn_tile(dm), SBA_HEAD_DIM, dm.AH
    nq = L // T
    scale = 1.0 / math.sqrt(DH)
    NT, TN = (_DOT_DIMS["nt"], ((), ())), (_DOT_DIMS["tn"], ((), ()))

    def body(nb_ref, q_ref, k_ref, v_ref, do_ref, tot_ref, dq_ref, dk_out, dv_out,
             ks, vs, dq_scr, p_scr, e_scr, dk_ref, dv_ref):
        h, i = pl.program_id(0), pl.program_id(1)

        @pl.when(i == 0)
        def _():
            ks[...] = k_ref[...].astype(BF16)
            vs[...] = v_ref[...].astype(BF16)
            dk_ref[...] = jnp.zeros_like(dk_ref)
            dv_ref[...] = jnp.zeros_like(dv_ref)

        qb = q_ref[...].astype(BF16)
        dob = do_ref[...].astype(BF16)
        tot_c = tot_ref[0][:, 0:1]
        rows = lax.broadcasted_iota(jnp.int32, (T, T), 0)
        cols = lax.broadcasted_iota(jnp.int32, (T, T), 1)
        causal = cols < rows
        u_fwd = (rows <= cols).astype(BF16)
        dq_scr[...] = jnp.zeros_like(dq_scr)
        p_scr[...] = jnp.zeros_like(p_scr)
        e_scr[...] = jnp.zeros_like(e_scr)

        def scores(j, masked):
            sl = pl.ds(pl.multiple_of(j * T, T), T)
            z = lax.dot_general(qb, ks[sl, :], NT, preferred_element_type=F32) * scale
            sp = _softplus(z)
            if masked:
                sp = jnp.where(causal, sp, 0.0)
            hi, lo = _split(sp)
            pin = (jnp.dot(hi, u_fwd, preferred_element_type=F32)
                   + jnp.dot(lo, u_fwd, preferred_element_type=F32))
            dw = lax.dot_general(dob, vs[sl, :], NT, preferred_element_type=F32)
            return sl, z, sp, pin, dw

        def grads(blk, before, e_before, masked):
            sl, z, sp, pin, dw = blk
            cs = (tot_c - before) - (pin - sp)
            w = jnp.exp(z - cs)
            if masked:
                w = jnp.where(causal, w, 0.0)
            e = dw * w
            ehi, elo = _split(e)
            fin = (jnp.dot(ehi, u_fwd, preferred_element_type=F32)
                   + jnp.dot(elo, u_fwd, preferred_element_type=F32))
            dz = (e - jnp.exp(z - sp) * (e_before + fin)) * scale
            if masked:
                dz = jnp.where(causal, dz, 0.0)
            dzb = dz.astype(BF16)
            dq_scr[...] += jnp.dot(dzb, ks[sl, :], preferred_element_type=F32)
            dk_ref[sl, :] += lax.dot_general(dzb, qb, TN, preferred_element_type=F32)
            dv_ref[sl, :] += lax.dot_general(w.astype(BF16), dob, TN, preferred_element_type=F32)
            return before + pin[:, T - 1:T], e_before + fin[:, T - 1:T]

        def loop(j, carry):
            p_scr[...], e_scr[...] = grads(scores(j, False), p_scr[...], e_scr[...], False)
            return carry

        lax.fori_loop(i - nb_ref[h, i] + 1, i - 1, loop, 0)

        @pl.when(i == 0)
        def _():
            grads(scores(i, True), p_scr[...], e_scr[...], True)

        @pl.when(i > 0)
        def _():
            prev, diag = scores(i - 1, False), scores(i, True)
            p1, e1 = grads(prev, p_scr[...], e_scr[...], False)
            grads(diag, p1, e1, True)

        dq_ref[...] = dq_scr[...].astype(BF16)

        @pl.when(i == nq - 1)
        def _():
            dk_out[...] = dk_ref[...].astype(BF16)
            dv_out[...] = dv_ref[...].astype(BF16)

    kv = lambda off: pl.BlockSpec((L, DH), lambda h, i, nb: (0, off // DH + h))
    qblk = lambda off: pl.BlockSpec((T, DH), lambda h, i, nb: (i, off // DH + h))
    acc = pl.BlockSpec((L, DH), lambda h, i, nb: (0, h))
    est = 2 * 2 * L * DH * 4 * 2 + 2 * L * DH * 2 + 16 * T * T * 4
    grid_spec = pltpu.PrefetchScalarGridSpec(
        num_scalar_prefetch=1, grid=(AH, nq),
        in_specs=[qblk(dm.q_off), kv(dm.k_off), kv(dm.v_off), qblk(dm.SW),
                  pl.BlockSpec((1, T, LANES), lambda h, i, nb: (h, i, 0))],
        out_specs=[qblk(0), acc, acc],
        scratch_shapes=[pltpu.VMEM((L, DH), BF16), pltpu.VMEM((L, DH), BF16),
                        pltpu.VMEM((T, DH), F32), pltpu.VMEM((T, 1), F32), pltpu.VMEM((T, 1), F32),
                        pltpu.VMEM((L, DH), F32), pltpu.VMEM((L, DH), F32)])
    return pl.pallas_call(
        body, name=name, grid_spec=grid_spec,
        out_shape=[SDS((L, dm.AW), BF16)] * 3,
        compiler_params=_cp(("arbitrary", "arbitrary"), est))(nblk, proj, proj, proj, dyo, tot)


def _gate_specs(dm, tm, order):
    GW, G = dm.GW, dm.G
    ix = (lambda a, b: (a, b)) if order == "ij" else (lambda a, b: (b, a))

    def spec(colfn):
        return pl.BlockSpec((tm, GW), lambda p0, p1: (ix(p0, p1)[0], colfn(ix(p0, p1)[1])))

    y_spec = spec(lambda j: jnp.minimum(j, G - 1))
    o_spec = spec(lambda j: jnp.maximum(j - G, 0))
    zg_spec = spec(lambda j: jnp.where(j < G, j, dm.g_off // GW + j - G))
    w_spec = pl.BlockSpec((1, GW), lambda p0, p1: (0, jnp.minimum(ix(p0, p1)[1], G - 1)))
    full = spec(lambda j: j)
    return y_spec, o_spec, zg_spec, w_spec, full


def gate_fwd(y, o, proj, snw, dm, name):
    L, GW, G = dm.L, dm.GW, dm.G
    tm = _pick(L, 512, SUBLANES)
    ncol = (dm.SW + dm.AW) // GW
    y_spec, o_spec, zg_spec, w_spec, full = _gate_specs(dm, tm, "ij")

    def body(y_ref, o_ref, zg_ref, w_ref, m_ref):
        j = pl.program_id(1)
        zg = zg_ref[...]
        gate = zg * _sigmoid(zg)

        @pl.when(j < G)
        def _():
            yz = y_ref[...] * gate
            r = lax.rsqrt(jnp.mean(yz * yz, axis=-1, keepdims=True) + EPS)
            m_ref[...] = (yz * r * w_ref[...]).astype(BF16)

        @pl.when(j >= G)
        def _():
            m_ref[...] = (o_ref[...] * gate).astype(BF16)

    return pl.pallas_call(
        body, name=name, grid=(L // tm, ncol),
        in_specs=[y_spec, o_spec, zg_spec, w_spec], out_specs=full,
        out_shape=SDS((L, dm.SW + dm.AW), BF16),
        compiler_params=_cp(("arbitrary", "arbitrary"), 2 * tm * GW * 16))(y, o, proj, snw)


def gate_bwd(dmix, y, o, proj, snw, dm, name):
    L, GW, G = dm.L, dm.GW, dm.G
    tm = _pick(L, 512, SUBLANES)
    W = dm.SW + dm.AW
    y_spec, o_spec, zg_spec, w_spec, full = _gate_specs(dm, tm, "ji")

    def body(d_ref, y_ref, o_ref, zg_ref, w_ref, dyo_ref, dzg_ref, dw_ref):
        j = pl.program_id(0)

        @pl.when(pl.program_id(1) == 0)
        def _():
            dw_ref[...] = jnp.zeros_like(dw_ref)

        zg, d = zg_ref[...], d_ref[...]
        sg = _sigmoid(zg)
        gate = zg * sg
        dgate = sg * (1.0 + zg * (1.0 - sg))

        @pl.when(j < G)
        def _():
            yv = y_ref[...]
            yz = yv * gate
            r = lax.rsqrt(jnp.mean(yz * yz, axis=-1, keepdims=True) + EPS)
            nrm = yz * r
            dw_ref[0:1, :] += jnp.sum(d * nrm, axis=0, keepdims=True)
            dn = d * w_ref[...]
            dyz = r * (dn - nrm * jnp.mean(dn * nrm, axis=-1, keepdims=True))
            dyo_ref[...] = dyz * gate
            dzg_ref[...] = (dyz * yv * dgate).astype(BF16)

        @pl.when(j >= G)
        def _():
            dyo_ref[...] = d * gate
            dzg_ref[...] = (d * o_ref[...] * dgate).astype(BF16)

    return pl.pallas_call(
        body, name=name, grid=(W // GW, L // tm),
        in_specs=[full, y_spec, o_spec, zg_spec, w_spec],
        out_specs=[full, full, pl.BlockSpec((SUBLANES, GW), lambda j, i: (0, j))],
        out_shape=[SDS((L, W), F32), SDS((L, W), BF16), SDS((SUBLANES, W), F32)],
        compiler_params=_cp(("arbitrary", "arbitrary"), 2 * tm * GW * 24))(dmix, y, o, proj, snw)


def adamw(parts, w, m, v, name):
    R, C = w.shape
    n_slot = parts.shape[0]
    tr = _pick(R, max(SUBLANES, (1 << 18) // C // SUBLANES * SUBLANES), SUBLANES)
    c1, c2 = 1.0 - ADAM_B1 ** ADAM_STEP, 1.0 - ADAM_B2 ** ADAM_STEP

    def body(p_ref, w_ref, m_ref, v_ref, g_ref, d_ref, m2_ref, v2_ref):
        g = p_ref[0].astype(F32)
        for s in range(1, n_slot):
            g = g + p_ref[s].astype(F32)
        m2 = ADAM_B1 * m_ref[...] + (1.0 - ADAM_B1) * g
        v2 = ADAM_B2 * v_ref[...] + (1.0 - ADAM_B2) * (g * g)
        g_ref[...] = g
        m2_ref[...] = m2
        v2_ref[...] = v2
        d_ref[...] = -ADAM_LR * ((m2 / c1) / (jnp.sqrt(v2 / c2) + ADAM_EPS) + ADAM_WD * w_ref[...])

    blk = pl.BlockSpec((tr, C), lambda i: (i, 0))
    return pl.pallas_call(
        body, name=name, grid=(R // tr,),
        in_specs=[pl.BlockSpec((n_slot, tr, C), lambda i: (0, i, 0)), blk, blk, blk],
        out_specs=[blk] * 4, out_shape=[SDS((R, C), F32)] * 4,
        compiler_params=_cp(("arbitrary",), 2 * tr * C * (n_slot * 4 + 28)))(parts, w, m, v)


N_CHIP = 4


def _place():
    x, y, c = lax.axis_index("x"), lax.axis_index("y"), lax.axis_index("c")
    return x, y, c, [(1 - x, y), (x, 1 - y), (1 - x, 1 - y)]


def _comm_call(body, arrays, out_shape, n_sems, n_local, name):
    hbm = pl.BlockSpec(memory_space=pltpu.HBM)
    return pl.pallas_call(
        body, name=name, in_specs=[hbm] * len(arrays), out_specs=[hbm] * len(out_shape), out_shape=out_shape,
        scratch_shapes=[pltpu.SemaphoreType.DMA((n_sems,)), pltpu.SemaphoreType.DMA((n_sems,)),
                        pltpu.SemaphoreType.DMA((n_local,))],
        compiler_params=pltpu.CompilerParams(has_side_effects=True))(*arrays)


def gather_weights(arrays, name):
    n, per = len(arrays), N_DEV - 1

    def body(*refs):
        srcs, dsts = refs[:n], refs[n:2 * n]
        send_sems, recv_sems, local_sems = refs[2 * n:]
        start, finish = _gather_halves(srcs, dsts, send_sems, recv_sems, local_sems)
        start()
        finish()

    out_shape = [SDS((N_DEV,) + a.shape, a.dtype) for a in arrays]
    return _comm_call(body, arrays, out_shape, n * per, n, name)


def _gather_halves(srcs, dsts, send_sems, recv_sems, local_sems):
    n, per = len(srcs), N_DEV - 1

    def parts():
        x, y, c, chips = _place()
        me, sib = 4 * x + 2 * y + c, (x, y, 1 - c)

        def cp(a, k, block, to, src=None):
            return pltpu.make_async_remote_copy(
                src_ref=dsts[a].at[block] if src is None else src, dst_ref=dsts[a].at[block],
                send_sem=send_sems.at[a * per + k], recv_sem=recv_sems.at[a * per + k],
                device_id=to, device_id_type=MESH)

        own = [pltpu.make_async_copy(srcs[a], dsts[a].at[me], local_sems.at[a]) for a in range(n)]
        first = []
        for a in range(n):
            first.append(cp(a, 0, me, sib, src=srcs[a]))
            first += [cp(a, 1 + j, me, (px, py, c), src=srcs[a]) for j, (px, py) in enumerate(chips)]
        return x, y, c, chips, sib, cp, own, first

    def start():
        *_, own, first = parts()
        for o in own:
            o.start()
        for f in first:
            f.start()

    def finish():
        x, y, c, chips, sib, cp, own, first = parts()
        passed = []
        for j, (px, py) in enumerate(chips):
            block = 4 * px + 2 * py + c
            for a in range(n):
                cp(a, 1 + j, block, sib).wait_recv()
                fwd = cp(a, 4 + j, block, sib)
                fwd.start()
                passed.append(fwd)
        for a in range(n):
            cp(a, 0, 4 * x + 2 * y + 1 - c, sib).wait_recv()
            for j, (px, py) in enumerate(chips):
                cp(a, 4 + j, 4 * px + 2 * py + 1 - c, sib).wait_recv()
        for f in first + passed:
            f.wait_send()
        for o in own:
            o.wait()

    return start, finish


def mm_gather(a, b, shards, *, tm, tn, name):
    (M, K), N = a.shape, b.shape[1]
    tm, tn = _pick(M, tm, LANES), _pick(N, tn, LANES)
    n, per = len(shards), N_DEV - 1
    gj, gi = N // tn, M // tm

    def body(*refs):
        a_ref, b_ref, srcs = refs[0], refs[1], refs[2:2 + n]
        o_ref, dsts = refs[2 + n], refs[3 + n:3 + 2 * n]
        send_sems, recv_sems, local_sems = refs[3 + 2 * n:]
        start, finish = _gather_halves(srcs, dsts, send_sems, recv_sems, local_sems)
        j, i = pl.program_id(0), pl.program_id(1)

        @pl.when(jnp.logical_and(j == 0, i == 0))
        def _():
            start()

        o_ref[...] = jnp.dot(a_ref[...].astype(BF16), b_ref[...].astype(BF16), preferred_element_type=F32)

        @pl.when(jnp.logical_and(j == gj - 1, i == gi - 1))
        def _():
            finish()

    hbm = pl.BlockSpec(memory_space=pltpu.HBM)
    est = 2 * (tm * K * a.dtype.itemsize + K * tn * b.dtype.itemsize + tm * tn * 4) + tm * tn * 4
    out = pl.pallas_call(
        body, name=name, grid=(gj, gi),
        in_specs=[pl.BlockSpec((tm, K), lambda j, i: (i, 0)), pl.BlockSpec((K, tn), lambda j, i: (0, j))] + [hbm] * n,
        out_specs=[pl.BlockSpec((tm, tn), lambda j, i: (i, j))] + [hbm] * n,
        out_shape=[SDS((M, N), F32)] + [SDS((N_DEV,) + s.shape, s.dtype) for s in shards],
        scratch_shapes=[pltpu.SemaphoreType.DMA((n * per,)), pltpu.SemaphoreType.DMA((n * per,)),
                        pltpu.SemaphoreType.DMA((n,))],
        compiler_params=pltpu.CompilerParams(
            dimension_semantics=("arbitrary", "arbitrary"), has_side_effects=True,
            vmem_limit_bytes=int(min(max(est * 5 // 4 + (4 << 20), 32 << 20), VMEM_BYTES - (8 << 20)))))(a, b, *shards)
    return out[0], out[1:]


def pair_exchange(arrays, name):
    n = len(arrays)

    def body(*refs):
        srcs, dsts = refs[:n], refs[n:2 * n]
        send_sems, recv_sems, _ = refs[2 * n:]
        x, y, c, _chips = _place()
        sib = (x, y, 1 - c)

        def cp(a, k):
            return pltpu.make_async_remote_copy(
                src_ref=srcs[a].at[2 * k + 1 - c], dst_ref=dsts[a].at[k],
                send_sem=send_sems.at[a * N_CHIP + k], recv_sem=recv_sems.at[a * N_CHIP + k],
                device_id=sib, device_id_type=MESH)

        cps = [cp(a, k) for k in range(N_CHIP) for a in range(n)]
        for p in cps:
            p.start()
        for p in cps:
            p.wait_recv()
        for p in cps:
            p.wait_send()

    out_shape = [SDS((N_CHIP,) + a.shape[1:], a.dtype) for a in arrays]
    return _comm_call(body, arrays, out_shape, n * N_CHIP, 1, name)


def pair_add(parts, got, name):
    _, R, C = parts.shape
    tr = _pick(R, max(16, (1 << 19) // C // 16 * 16), 16)
    core = lax.axis_index("c").astype(jnp.int32).reshape(1)

    def body(c_ref, p_ref, g_ref, o_ref):
        o_ref[...] = (p_ref[...].astype(F32) + g_ref[...].astype(F32)).astype(o_ref.dtype)

    grid_spec = pltpu.PrefetchScalarGridSpec(
        num_scalar_prefetch=1, grid=(N_CHIP, R // tr),
        in_specs=[pl.BlockSpec((1, tr, C), lambda k, i, c_ref: (2 * k + c_ref[0], i, 0)),
                  pl.BlockSpec((1, tr, C), lambda k, i, c_ref: (k, i, 0))],
        out_specs=pl.BlockSpec((1, tr, C), lambda k, i, c_ref: (k, i, 0)))
    return pl.pallas_call(
        body, name=name, grid_spec=grid_spec, out_shape=SDS((N_CHIP, R, C), parts.dtype),
        compiler_params=_cp(("arbitrary", "arbitrary"), 2 * 3 * tr * C * 2 + 3 * tr * C * 4))(core, parts, got)


def chip_exchange(sums, full, split, name):
    ns, nf, nsp = len(sums), len(full), len(split)
    n_sem = 3 * ns + (N_DEV - 1) * (nf + nsp)

    def body(*refs):
        n_in = ns + nf + nsp
        srcs, dsts = refs[:n_in], refs[n_in:2 * n_in]
        send_sems, recv_sems, local_sems = refs[2 * n_in:]
        x, y, c, chips = _place()
        me, my_chip = 4 * x + 2 * y + c, 2 * x + y
        started, arrivals, own = [], [], []
        for a in range(ns):
            own.append(pltpu.make_async_copy(srcs[a].at[my_chip], dsts[a].at[my_chip], local_sems.at[a]))
            for j, (px, py) in enumerate(chips):
                k = 2 * px + py
                sem = 3 * a + j
                started.append(pltpu.make_async_remote_copy(
                    src_ref=srcs[a].at[k], dst_ref=dsts[a].at[my_chip],
                    send_sem=send_sems.at[sem], recv_sem=recv_sems.at[sem],
                    device_id=(px, py, c), device_id_type=MESH))
                arrivals.append(pltpu.make_async_remote_copy(
                    src_ref=srcs[a].at[my_chip], dst_ref=dsts[a].at[k],
                    send_sem=send_sems.at[sem], recv_sem=recv_sems.at[sem],
                    device_id=(px, py, c), device_id_type=MESH))
        for b in range(nf + nsp):
            a = ns + b
            is_split = b >= nf
            own.append(pltpu.make_async_copy(srcs[a].at[me] if is_split else srcs[a], dsts[a].at[me],
                                             local_sems.at[a]))
            for rel in range(1, N_DEV):
                px, py, pc = x ^ ((rel >> 2) & 1), y ^ ((rel >> 1) & 1), c ^ (rel & 1)
                pidx = 4 * px + 2 * py + pc
                sem = 3 * ns + b * (N_DEV - 1) + rel - 1
                started.append(pltpu.make_async_remote_copy(
                    src_ref=srcs[a].at[pidx] if is_split else srcs[a], dst_ref=dsts[a].at[me],
                    send_sem=send_sems.at[sem], recv_sem=recv_sems.at[sem],
                    device_id=(px, py, pc), device_id_type=MESH))
                arrivals.append(pltpu.make_async_remote_copy(
                    src_ref=srcs[a].at[me] if is_split else srcs[a], dst_ref=dsts[a].at[pidx],
                    send_sem=send_sems.at[sem], recv_sem=recv_sems.at[sem],
                    device_id=(px, py, pc), device_id_type=MESH))
        for o in own:
            o.start()
        for s in started:
            s.start()
        for r in arrivals:
            r.wait_recv()
        for s in started:
            s.wait_send()
        for o in own:
            o.wait()

    out_shape = ([SDS(a.shape, a.dtype) for a in sums] + [SDS((N_DEV,) + a.shape, a.dtype) for a in full]
                 + [SDS(a.shape, a.dtype) for a in split])
    return _comm_call(body, list(sums) + list(full) + list(split), out_shape, n_sem, ns + nf + nsp, name)


class LayerParams(NamedTuple):
    nw: jax.Array
    w_in: jax.Array
    cw: jax.Array
    cb: jax.Array
    par: jax.Array
    par_all: jax.Array
    snw: jax.Array
    w_out: jax.Array


def head_params(dt_bias, a_log, d_skip, dm):
    rows = jnp.stack([dt_bias, a_log, d_skip])
    par_all = jnp.pad(rows, ((0, SUBLANES - 3), (0, LANES - dm.NH)))
    par = jnp.pad(rows.reshape(3, dm.G, dm.R).transpose(1, 0, 2), ((0, 0), (0, SUBLANES - 3), (0, LANES - dm.R)))
    return par, par_all


def layer_fwd(x, p, dm, tag, next_shards=None):
    h = rms_fwd(x, p.nw, f"rms_fwd{tag}")
    gathered = None
    if next_shards is None:
        proj = mm(h, p.w_in, "nn", tm=512, tn=1920, tk=dm.D, name=f"in_proj{tag}", b_outer=True)
    else:
        proj, gathered = mm_gather(h, p.w_in, list(next_shards), tm=512, tn=1920, name=f"in_proj_gather{tag}")
    xa = conv_fwd(proj, p.cw, p.cb, dm, f"conv_fwd{tag}")
    prep = ssd_prep(proj, p.par_all, dm, f"ssd_prep{tag}")
    y, hprev = ssd_fwd(xa, prep, p.par, dm, f"ssd_fwd{tag}")
    o, tot, nblk = attn_fwd(proj, dm, f"attn_fwd{tag}")
    mix = gate_fwd(y, o, proj, p.snw, dm, f"gate_fwd{tag}")
    xn = mm(mix, p.w_out, "nn", tm=512, tn=1024, tk=dm.SW + dm.AW, name=f"out_proj{tag}", res=x)
    return xn, (x, h, proj, xa, prep, y, hprev, o, tot, nblk, mix), gathered


def layer_bwd(dxn, saved, p, dm, tag):
    x, h, proj, xa, prep, y, hprev, o, tot, nblk, mix = saved
    dmix = mm(dxn, p.w_out, "nt", tm=512, tn=1024, tk=dm.D, name=f"d_mix{tag}")
    dw_out = mm(mix, dxn, "tn", tm=1024, tn=dm.D, tk=512, name=f"dw_out{tag}")
    dyo, dzg, dsnw = gate_bwd(dmix, y, o, proj, p.snw, dm, f"gate_bwd{tag}")
    dq, dk, dv = attn_bwd(proj, dyo, tot, nblk, dm, f"attn_bwd{tag}")
    dxs, db, dc, dac, span, ddtx, gsk = ssd_bwd(xa, prep, p.par, hprev, dyo, dm, f"ssd_bwd{tag}")
    ddt_blk, ghead = ssd_post(proj, p.par_all, dac, span, ddtx, gsk, dm, f"ssd_post{tag}")
    dcv, gconv = conv_bwd_pre(proj, dxs, db, dc, p.cw, p.cb, dm, f"conv_bwd_pre{tag}")
    dxbc = conv_bwd_in(dcv, p.cw, dm, f"conv_bwd_in{tag}")
    dproj = jnp.concatenate([dzg[:, :dm.SW], dxbc, dq, dk, dv, dzg[:, dm.SW:], ddt_blk], axis=1)
    dh = mm(dproj, p.w_in, "nt", tm=512, tn=dm.D, tk=1920, name=f"d_h{tag}")
    dw_in = mm(h, dproj, "tn", tm=dm.D, tn=960, tk=1024, name=f"dw_in{tag}")
    dx, dnw = rms_bwd(dh, x, p.nw, dxn, f"rms_bwd{tag}")
    small = dict(norm_w=dnw[0], conv_w=gconv[:SSD_CONV], conv_b=gconv[SSD_CONV],
                 dt_bias=ghead[0, :dm.NH], a_log=ghead[1, :dm.NH], d_skip=ghead[2, :dm.NH],
                 ssd_norm_w=dsnw[0, :dm.SW])
    return dx, dw_in, dw_out, small


SMALL = ("norm_w", "conv_b", "dt_bias", "a_log", "d_skip", "ssd_norm_w")


def _to_mine(w, dm):
    a, b = dm.SW + dm.CD, dm.SW + dm.CD + dm.NH
    pad = jnp.zeros((w.shape[0], LANES - dm.NH), w.dtype)
    return jnp.concatenate([w[:, :a], w[:, b:], w[:, a:b], pad], axis=1)


def _from_mine(w, dm):
    a = dm.SW + dm.CD
    return jnp.concatenate([w[:, :a], w[:, dm.dt_off:dm.dt_off + dm.NH], w[:, a:dm.dt_off]], axis=1)


def _pack(pieces):
    flat = jnp.concatenate([p.reshape(-1) for p in pieces])
    rows = -(-flat.shape[0] // LANES)
    rows = -(-rows // SUBLANES) * SUBLANES
    return jnp.pad(flat, (0, rows * LANES - flat.shape[0])).reshape(rows, LANES)


def _unpack(buf, shapes):
    flat, out, at = buf.reshape(-1), [], 0
    for s in shapes:
        n = math.prod(s)
        out.append(flat[at:at + n].reshape(s))
        at += n
    return out


def kernel(x, norm_w, w_in, conv_w, conv_b, dt_bias, a_log, d_skip, ssd_norm_w, w_out, final_norm_w, loss_target, m_norm_w, m_w_in, m_conv_w, m_conv_b, m_dt_bias, m_a_log, m_d_skip, m_ssd_norm_w, m_w_out, m_final_norm_w, v_norm_w, v_w_in, v_conv_w, v_conv_b, v_dt_bias, v_a_log, v_d_skip, v_ssd_norm_w, v_w_out, v_final_norm_w):
    depth, D = norm_w.shape
    L = x.shape[1]
    NH = dt_bias.shape[1]
    SW = NH * SSD_HEAD_DIM
    CD = conv_b.shape[1]
    dm = Dims(L=L, D=D, SW=SW, G=(CD - SW) // (2 * SSD_STATE), AW=w_out.shape[1] * N_DEV - SW)
    ncol, csh, osh = w_in.shape[2], conv_w.shape[2], w_out.shape[1]
    me = 4 * lax.axis_index("x") + 2 * lax.axis_index("y") + lax.axis_index("c")

    shards = [(w_in[l].astype(BF16), w_out[l].astype(BF16), conv_w[l]) for l in range(depth)]

    def layer_params(l, gathered):
        g_in, g_out, g_cw = gathered
        full_in = g_in.transpose(1, 0, 2).reshape(D, N_DEV * ncol)
        full_cw = g_cw.transpose(1, 0, 2).reshape(SSD_CONV, CD)
        par, par_all = head_params(dt_bias[l], a_log[l], d_skip[l], dm)
        return LayerParams(
            nw=norm_w[l][None], w_in=_to_mine(full_in, dm), cw=full_cw, cb=conv_b[l][None], par=par, par_all=par_all,
            snw=ssd_norm_w[l][None], w_out=g_out.reshape(N_DEV * osh, D))

    h = x[0]
    params, saved = [], []
    gathered = gather_weights(list(shards[0]), "gather_weights")
    for l in range(depth):
        params.append(layer_params(l, gathered))
        h, s, gathered = layer_fwd(h, params[l], dm, "", shards[l + 1] if l + 1 < depth else None)
        saved.append(s)
    dh, dfw, ls = loss_head(h, final_norm_w[None], loss_target[0], "loss_head")
    loss = lax.psum(ls[0, 0], ("x", "y", "c"))
    gw_in, gw_out, smalls = [None] * depth, [None] * depth, [None] * depth
    for l in reversed(range(depth)):
        dh, gw_in[l], gw_out[l], smalls[l] = layer_bwd(dh, saved[l], params[l], dm, "")
    grad_x = dh[None]

    p_in = jnp.stack([_from_mine(g, dm).astype(BF16).reshape(D, N_DEV, ncol).transpose(1, 0, 2)
                      for g in gw_in], axis=1).reshape(N_DEV, depth * D, ncol)
    p_out = jnp.stack([g.astype(BF16).reshape(N_DEV, osh, D) for g in gw_out], axis=1).reshape(N_DEV, depth * osh, D)
    rep = [jnp.stack([s[k] for s in smalls]) for k in SMALL] + [dfw[0]]
    rep_shapes = [r.shape for r in rep]
    p_rep = _pack(rep)
    p_cw = jnp.stack([s["conv_w"] for s in smalls]).reshape(depth * SSD_CONV, N_DEV, csh).transpose(1, 0, 2)
    s_in, s_out = pair_exchange([p_in, p_out], "pair_exchange")
    c_in, c_out = pair_add(p_in, s_in, "pair_add_w_in"), pair_add(p_out, s_out, "pair_add_w_out")
    r_in, r_out, r_rep, r_cw = chip_exchange([c_in, c_out], [p_rep], [p_cw], "chip_exchange")

    out_in = adamw(r_in, w_in.reshape(depth * D, ncol), m_w_in.reshape(depth * D, ncol),
                   v_w_in.reshape(depth * D, ncol), "adamw_w_in")
    out_out = adamw(r_out, w_out.reshape(depth * osh, D), m_w_out.reshape(depth * osh, D),
                    v_w_out.reshape(depth * osh, D), "adamw_w_out")
    out_cw = adamw(r_cw, conv_w.reshape(depth * SSD_CONV, csh), m_conv_w.reshape(depth * SSD_CONV, csh),
                   v_conv_w.reshape(depth * SSD_CONV, csh), "adamw_conv_w")
    rep_w = [norm_w, conv_b, dt_bias, a_log, d_skip, ssd_norm_w, final_norm_w]
    rep_m = [m_norm_w, m_conv_b, m_dt_bias, m_a_log, m_d_skip, m_ssd_norm_w, m_final_norm_w]
    rep_v = [v_norm_w, v_conv_b, v_dt_bias, v_a_log, v_d_skip, v_ssd_norm_w, v_final_norm_w]
    out_rep = adamw(r_rep, _pack(rep_w), _pack(rep_m), _pack(rep_v), "adamw_replicated")

    outs = {}
    for kind, i in (("grad", 0), ("delta", 1), ("new_m", 2), ("new_v", 3)):
        r = dict(zip(SMALL + ("final_norm_w",), _unpack(out_rep[i], rep_shapes)))
        r["w_in"] = out_in[i].reshape(w_in.shape)
        r["w_out"] = out_out[i].reshape(w_out.shape)
        r["conv_w"] = out_cw[i].reshape(conv_w.shape)
        outs[kind] = r
    order = ("norm_w", "w_in", "conv_w", "conv_b", "dt_bias", "a_log", "d_skip", "ssd_norm_w", "w_out", "final_norm_w")
    return (loss, grad_x, *[outs[k][n] for k in ("grad", "delta", "new_m", "new_v") for n in order])
```

```python
import functools
import math
from typing import NamedTuple

import jax
import jax.numpy as jnp
from jax import lax
from jax.experimental import pallas as pl
from jax.experimental.pallas import tpu as pltpu

F32, BF16 = jnp.float32, jnp.bfloat16
SDS = jax.ShapeDtypeStruct
EPS = 1e-6
LANES = 128
SUBLANES = 8
VMEM_BYTES = 64 * 2 ** 20
N_DEV = 8
SSD_HEAD_DIM = 64
SSD_STATE = 128
SSD_CONV = 4
SBA_HEAD_DIM = 128
ADAM_LR, ADAM_B1, ADAM_B2, ADAM_EPS, ADAM_WD, ADAM_STEP = 0.001, 0.9, 0.999, 1e-08, 0.01, 10
SKIP_SUM = 110.0
HIGHEST = lax.Precision.HIGHEST
MESH = pl.DeviceIdType.MESH


class Dims(NamedTuple):
    L: int
    D: int
    SW: int
    G: int
    AW: int

    @property
    def NH(self): return self.SW // SSD_HEAD_DIM
    @property
    def R(self): return self.NH // self.G
    @property
    def GW(self): return self.SW // self.G
    @property
    def CD(self): return self.SW + 2 * self.G * SSD_STATE
    @property
    def AH(self): return self.AW // SBA_HEAD_DIM
    @property
    def q_off(self): return self.SW + self.CD
    @property
    def k_off(self): return self.q_off + self.AW
    @property
    def v_off(self): return self.q_off + 2 * self.AW
    @property
    def g_off(self): return self.q_off + 3 * self.AW
    @property
    def dt_off(self): return self.q_off + 4 * self.AW
    @property
    def NP(self): return self.dt_off + LANES


def _pick(n, target, mult):
    t = (min(target, n) // mult) * mult
    while t >= mult:
        if n % t == 0:
            return t
        t -= mult
    return n


def _cp(sem, vmem_est):
    limit = int(min(max(vmem_est * 5 // 4 + (4 << 20), 32 << 20), VMEM_BYTES - (8 << 20)))
    return pltpu.CompilerParams(dimension_semantics=sem, vmem_limit_bytes=limit)


def _sigmoid(x):
    return 1.0 / (1.0 + jnp.exp(-x))


def _softplus(x):
    return jnp.maximum(x, 0.0) + jnp.log(1.0 + jnp.exp(-jnp.abs(x)))


def _nbytes(shape, dtype):
    return math.prod(shape) * jnp.dtype(dtype).itemsize


_DOT_DIMS = {"nn": ((1,), (0,)), "nt": ((1,), (1,)), "tn": ((0,), (0,))}


class Exchange(NamedTuple):
    arrays: list
    out_shape: list
    n_sems: int
    n_local: int
    halves: object


def mm(a, b, mode, *, tm, tn, tk, name, res=None, b_outer=False, exchange=None):
    if mode == "nn":
        (M, K), N = a.shape, b.shape[1]
    elif mode == "nt":
        (M, K), N = a.shape, b.shape[0]
    else:
        (K, M), N = a.shape, b.shape[1]
    tm, tn, tk = _pick(M, tm, LANES), _pick(N, tn, LANES), _pick(K, tk, LANES)
    nk = K // tk

    def ij(p0, p1):
        return (p1, p0) if b_outer else (p0, p1)

    if mode == "tn":
        a_spec = pl.BlockSpec((tk, tm), lambda p0, p1, k: (k, ij(p0, p1)[0]))
    else:
        a_spec = pl.BlockSpec((tm, tk), lambda p0, p1, k: (ij(p0, p1)[0], k))
    if mode == "nt":
        b_spec = pl.BlockSpec((tn, tk), lambda p0, p1, k: (ij(p0, p1)[1], k))
    else:
        b_spec = pl.BlockSpec((tk, tn), lambda p0, p1, k: (k, ij(p0, p1)[1]))
    o_spec = pl.BlockSpec((tm, tn), lambda p0, p1, k: ij(p0, p1))
    dims = (_DOT_DIMS[mode], ((), ()))

    grid = (N // tn, M // tm, nk) if b_outer else (M // tm, N // tn, nk)
    n_res = 0 if res is None else 1
    n_ex = 0 if exchange is None else len(exchange.arrays)

    def body(*refs):
        a_ref, b_ref, o_ref = refs[0], refs[1], refs[2 + n_res + n_ex]
        ids = [pl.program_id(d) for d in range(3)]
        if exchange is not None:
            srcs = refs[2 + n_res:2 + n_res + n_ex]
            dsts = refs[3 + n_res + n_ex:3 + n_res + 2 * n_ex]
            start, finish = exchange.halves(srcs, dsts, *refs[3 + n_res + 2 * n_ex:])

            @pl.when(jnp.logical_and(jnp.logical_and(ids[0] == 0, ids[1] == 0), ids[2] == 0))
            def _():
                start()

        part = lax.dot_general(a_ref[...].astype(BF16), b_ref[...].astype(BF16), dims,
                               preferred_element_type=F32)
        if res is not None:
            first = part + refs[2][...]
        else:
            first = part
        if nk == 1:
            o_ref[...] = first
        else:
            @pl.when(ids[2] == 0)
            def _():
                o_ref[...] = first

            @pl.when(ids[2] > 0)
            def _():
                o_ref[...] += part

        if exchange is not None:
            @pl.when(jnp.logical_and(jnp.logical_and(ids[0] == grid[0] - 1, ids[1] == grid[1] - 1),
                                     ids[2] == nk - 1))
            def _():
                finish()

    ins, specs = [a, b], [a_spec, b_spec]
    if res is not None:
        ins.append(res)
        specs.append(o_spec)
    est = 2 * (tm * tk * a.dtype.itemsize + tk * tn * b.dtype.itemsize + tm * tn * 4 * (2 if res is not None else 1))
    est += tm * tk * 2 + tk * tn * 2 + tm * tn * 4
    if exchange is None:
        return pl.pallas_call(
            body, name=name, grid=grid, in_specs=specs, out_specs=o_spec,
            out_shape=SDS((M, N), F32),
            compiler_params=_cp(("arbitrary", "arbitrary", "arbitrary"), est))(*ins)
    hbm = pl.BlockSpec(memory_space=pltpu.HBM)
    cp = _cp(("arbitrary", "arbitrary", "arbitrary"), est)
    out = pl.pallas_call(
        body, name=name, grid=grid, in_specs=specs + [hbm] * n_ex, out_specs=[o_spec] + [hbm] * n_ex,
        out_shape=[SDS((M, N), F32)] + list(exchange.out_shape),
        scratch_shapes=[pltpu.SemaphoreType.DMA((exchange.n_sems,)), pltpu.SemaphoreType.DMA((exchange.n_sems,)),
                        pltpu.SemaphoreType.DMA((exchange.n_local,))],
        compiler_params=pltpu.CompilerParams(
            dimension_semantics=cp.dimension_semantics, vmem_limit_bytes=cp.vmem_limit_bytes,
            has_side_effects=True))(*ins, *exchange.arrays)
    return out[0], out[1:]


def rms_fwd(x, nw, name):
    L, D = x.shape
    tm = _pick(L, 512, SUBLANES)

    def body(x_ref, w_ref, h_ref):
        xx = x_ref[...]
        r = lax.rsqrt(jnp.mean(xx * xx, axis=-1, keepdims=True) + EPS)
        h_ref[...] = (xx * r * w_ref[...]).astype(BF16)

    return pl.pallas_call(
        body, name=name, grid=(L // tm,),
        in_specs=[pl.BlockSpec((tm, D), lambda i: (i, 0)), pl.BlockSpec((1, D), lambda i: (0, 0))],
        out_specs=pl.BlockSpec((tm, D), lambda i: (i, 0)), out_shape=SDS((L, D), BF16),
        compiler_params=_cp(("arbitrary",), 2 * tm * D * 6))(x, nw)


def rms_bwd(dh, x, nw, dres, name):
    L, D = x.shape
    tm = _pick(L, 256, SUBLANES)

    def body(dh_ref, x_ref, w_ref, dr_ref, dx_ref, dw_ref):
        @pl.when(pl.program_id(0) == 0)
        def _():
            dw_ref[...] = jnp.zeros_like(dw_ref)

        xx, d = x_ref[...], dh_ref[...]
        r = lax.rsqrt(jnp.mean(xx * xx, axis=-1, keepdims=True) + EPS)
        xh = xx * r
        dw_ref[0:1, :] += jnp.sum(d * xh, axis=0, keepdims=True)
        dxh = d * w_ref[...]
        dx_ref[...] = dr_ref[...] + r * (dxh - xh * jnp.mean(dxh * xh, axis=-1, keepdims=True))

    row = pl.BlockSpec((tm, D), lambda i: (i, 0))
    return pl.pallas_call(
        body, name=name, grid=(L // tm,),
        in_specs=[row, row, pl.BlockSpec((1, D), lambda i: (0, 0)), row],
        out_specs=[row, pl.BlockSpec((SUBLANES, D), lambda i: (0, 0))],
        out_shape=[SDS((L, D), F32), SDS((SUBLANES, D), F32)],
        compiler_params=_cp(("arbitrary",), 2 * tm * D * 16))(dh, x, nw, dres)


def loss_head(h, fw, tgt, name):
    L, D = h.shape
    tm = _pick(L, 256, SUBLANES)

    def body(h_ref, w_ref, t_ref, dh_ref, dw_ref, ls_ref):
        @pl.when(pl.program_id(0) == 0)
        def _():
            dw_ref[...] = jnp.zeros_like(dw_ref)
            ls_ref[...] = jnp.zeros_like(ls_ref)

        xx = h_ref[...]
        r = lax.rsqrt(jnp.mean(xx * xx, axis=-1, keepdims=True) + EPS)
        xh = xx * r
        err = xh * w_ref[...] - t_ref[...]
        per_tok = jnp.mean(err * err, axis=-1, keepdims=True)
        ls_ref[...] += jnp.broadcast_to(0.5 * jnp.sum(per_tok, axis=0, keepdims=True), ls_ref.shape)
        dy = err * (1.0 / D)
        dw_ref[0:1, :] += jnp.sum(dy * xh, axis=0, keepdims=True)
        dxh = dy * w_ref[...]
        dh_ref[...] = r * (dxh - xh * jnp.mean(dxh * xh, axis=-1, keepdims=True))

    row = pl.BlockSpec((tm, D), lambda i: (i, 0))
    return pl.pallas_call(
        body, name=name, grid=(L // tm,),
        in_specs=[row, pl.BlockSpec((1, D), lambda i: (0, 0)), row],
        out_specs=[row, pl.BlockSpec((SUBLANES, D), lambda i: (0, 0)),
                   pl.BlockSpec((SUBLANES, LANES), lambda i: (0, 0))],
        out_shape=[SDS((L, D), F32), SDS((SUBLANES, D), F32), SDS((SUBLANES, LANES), F32)],
        compiler_params=_cp(("arbitrary",), 2 * tm * D * 12))(h, fw, tgt)


def _shifted(u, edge, s, back):
    n = u.shape[0]
    row = lax.broadcasted_iota(jnp.int32, (SUBLANES, u.shape[1]), 0)
    if back:
        r = pltpu.roll(u, s, 0)
        head = jnp.where(row < s, pltpu.roll(edge, s, 0), r[0:SUBLANES])
        return jnp.concatenate([head, r[SUBLANES:]], axis=0)
    r = pltpu.roll(u, n - s, 0)
    tail = jnp.where(row >= SUBLANES - s, pltpu.roll(edge, SUBLANES - s, 0), r[n - SUBLANES:])
    return jnp.concatenate([r[:n - SUBLANES], tail], axis=0)


def _conv_pre(u, prev, w, b):
    acc = b + w[SSD_CONV - 1:SSD_CONV] * u
    taps = [u]
    for s in range(1, SSD_CONV):
        us = _shifted(u, prev, s, True)
        taps.append(us)
        acc = acc + w[SSD_CONV - 1 - s:SSD_CONV - s] * us
    return acc, taps


def _conv_specs(dm, tm, tc, col0):
    rb = tm // SUBLANES
    u_spec = pl.BlockSpec((tm, tc), lambda j, i: (i, col0 + j))
    prev_spec = pl.BlockSpec((SUBLANES, tc), lambda j, i: (jnp.maximum(i * rb - 1, 0), col0 + j))
    w_spec = pl.BlockSpec((SSD_CONV, tc), lambda j, i: (0, j))
    b_spec = pl.BlockSpec((1, tc), lambda j, i: (0, j))
    return u_spec, prev_spec, w_spec, b_spec


def conv_fwd(proj, cw, cb, dm, name):
    L, CD = dm.L, dm.CD
    tm, tc = _pick(L, 512, SUBLANES), _pick(math.gcd(CD, dm.SW), 512, LANES)
    u_spec, prev_spec, w_spec, b_spec = _conv_specs(dm, tm, tc, dm.SW // tc)

    def body(u_ref, p_ref, w_ref, b_ref, o_ref):
        prev = jnp.where(pl.program_id(1) == 0, 0.0, p_ref[...])
        c, _ = _conv_pre(u_ref[...], prev, w_ref[...], b_ref[...])
        o_ref[...] = c * _sigmoid(c)

    return pl.pallas_call(
        body, name=name, grid=(CD // tc, L // tm),
        in_specs=[u_spec, prev_spec, w_spec, b_spec],
        out_specs=pl.BlockSpec((tm, tc), lambda j, i: (i, j)), out_shape=SDS((L, CD), F32),
        compiler_params=_cp(("arbitrary", "arbitrary"), 12 * tm * tc * 4))(proj, proj, cw, cb)


def conv_bwd_pre(proj, dxs, db, dcm, cw, cb, dm, name):
    L, CD = dm.L, dm.CD
    gn = dm.G * SSD_STATE
    tm, tc = _pick(L, 512, SUBLANES), _pick(math.gcd(gn, dm.SW), 512, LANES)
    u_spec, prev_spec, w_spec, b_spec = _conv_specs(dm, tm, tc, dm.SW // tc)
    nx, nb = dm.SW // tc, gn // tc

    def body(u_ref, p_ref, dx_ref, db_ref, dcm_ref, w_ref, b_ref, dc_ref, g_ref):
        j = pl.program_id(0)

        @pl.when(pl.program_id(1) == 0)
        def _():
            g_ref[...] = jnp.zeros_like(g_ref)

        prev = jnp.where(pl.program_id(1) == 0, 0.0, p_ref[...])
        c, taps = _conv_pre(u_ref[...], prev, w_ref[...], b_ref[...])
        sg = _sigmoid(c)
        d = jnp.where(j < nx, dx_ref[...], jnp.where(j < nx + nb, db_ref[...], dcm_ref[...]))
        dc = d * (sg * (1.0 + c * (1.0 - sg)))
        dc_ref[...] = dc
        for s in range(SSD_CONV):
            g_ref[SSD_CONV - 1 - s:SSD_CONV - s, :] += jnp.sum(dc * taps[s], axis=0, keepdims=True)
        g_ref[SSD_CONV:SSD_CONV + 1, :] += jnp.sum(dc, axis=0, keepdims=True)

    blk = pl.BlockSpec((tm, tc), lambda j, i: (i, j))
    part = lambda lo, n: pl.BlockSpec((tm, tc), lambda j, i: (i, jnp.clip(j - lo, 0, n - 1)))
    return pl.pallas_call(
        body, name=name, grid=(CD // tc, L // tm),
        in_specs=[u_spec, prev_spec, part(0, nx), part(nx, nb), part(nx + nb, nb), w_spec, b_spec],
        out_specs=[blk, pl.BlockSpec((SUBLANES, tc), lambda j, i: (0, j))],
        out_shape=[SDS((L, CD), F32), SDS((SUBLANES, CD), F32)],
        compiler_params=_cp(("arbitrary", "arbitrary"), 20 * tm * tc * 4))(proj, proj, dxs, db, dcm, cw, cb)


def conv_bwd_in(dc, cw, dm, name):
    L, CD = dm.L, dm.CD
    tm, tc = _pick(L, 512, SUBLANES), _pick(CD, 512, LANES)
    rb, nrow = tm // SUBLANES, L // SUBLANES
    ni = L // tm

    def body(d_ref, n_ref, w_ref, o_ref):
        nxt = jnp.where(pl.program_id(1) == ni - 1, 0.0, n_ref[...])
        dc_, w = d_ref[...], w_ref[...]
        acc = w[SSD_CONV - 1:SSD_CONV] * dc_
        for s in range(1, SSD_CONV):
            acc = acc + w[SSD_CONV - 1 - s:SSD_CONV - s] * _shifted(dc_, nxt, s, False)
        o_ref[...] = acc.astype(BF16)

    blk = pl.BlockSpec((tm, tc), lambda j, i: (i, j))
    return pl.pallas_call(
        body, name=name, grid=(CD // tc, ni),
        in_specs=[blk, pl.BlockSpec((SUBLANES, tc), lambda j, i: (jnp.minimum((i + 1) * rb, nrow - 1), j)),
                  pl.BlockSpec((SSD_CONV, tc), lambda j, i: (0, j))],
        out_specs=blk, out_shape=SDS((L, CD), BF16),
        compiler_params=_cp(("arbitrary", "arbitrary"), 10 * tm * tc * 4))(dc, dc, cw)


def _ssd_chunk(dm):
    return _pick(dm.L, 512, LANES)


def _dt_parts(dtr, par):
    return _softplus(dtr + par[0:1]), -jnp.exp(par[1:2])


def _tri(Q):
    rows = lax.broadcasted_iota(jnp.int32, (Q, Q), 0)
    cols = lax.broadcasted_iota(jnp.int32, (Q, Q), 1)
    return rows, cols


def _lanes_to_group(v, g, R, axis):
    n = v.shape[axis]
    return v if g == 0 else pltpu.roll(v, n - g * R, axis)


def ssd_prep(proj, par_all, dm, name):
    Q, G, R = _ssd_chunk(dm), dm.G, dm.R
    nc = dm.L // Q

    def body(dtr_ref, par_ref, dt_ref, ac_ref, at_ref):
        dt, a = _dt_parts(dtr_ref[...], par_ref[...])
        da = dt * a
        rows, cols = _tri(Q)
        acum = jnp.dot((rows >= cols).astype(F32), da, precision=HIGHEST, preferred_element_type=F32)
        acum_t = lax.dot_general(da, (rows <= cols).astype(F32), (_DOT_DIMS["tn"], ((), ())),
                                 precision=HIGHEST, preferred_element_type=F32)
        for g in range(G):
            dt_ref[g] = _lanes_to_group(dt, g, R, 1)
            ac_ref[g] = _lanes_to_group(acum, g, R, 1)
            at_ref[g, 0] = _lanes_to_group(acum_t, g, R, 0)[0:SUBLANES]

    lane_blk = pl.BlockSpec((G, Q, LANES), lambda c: (0, c, 0))
    return pl.pallas_call(
        body, name=name, grid=(nc,),
        in_specs=[pl.BlockSpec((Q, LANES), lambda c: (c, dm.dt_off // LANES)),
                  pl.BlockSpec((SUBLANES, LANES), lambda c: (0, 0))],
        out_specs=[lane_blk, lane_blk, pl.BlockSpec((G, 1, SUBLANES, Q), lambda c: (0, c, 0, 0))],
        out_shape=[SDS((G, dm.L, LANES), F32), SDS((G, dm.L, LANES), F32), SDS((G, nc, SUBLANES, Q), F32)],
        compiler_params=_cp(("arbitrary",), 8 << 20))(proj, par_all)


def _ssd_common(xa_refs, Q):
    _, b_ref, c_ref = xa_refs
    rows, cols = _tri(Q)
    bm, cm = b_ref[...].astype(BF16), c_ref[...].astype(BF16)
    gm = lax.dot_general(cm, bm, (_DOT_DIMS["nt"], ((), ())), preferred_element_type=F32)
    return rows >= cols, bm, cm, gm


def _ssd_head(r, Q, x, dt, tri, acum, acum_t, gm):
    P = SSD_HEAD_DIM
    col = acum[:, r:r + 1]
    row = acum_t[r:r + 1, :]
    lam = jnp.where(tri, jnp.exp(jnp.minimum(col - row, 0.0)), 0.0)
    m = gm * lam
    xh = x[:, r * P:(r + 1) * P]
    xs = xh * dt[:, r:r + 1]
    a_last = acum_t[r:r + 1, Q - 1:Q]
    return col, lam, m, xh, xs, a_last


def _ssd_specs(dm, Q, rev):
    nc = dm.L // Q
    cc = (lambda c: nc - 1 - c) if rev else (lambda c: c)
    nb = dm.SW // SSD_STATE
    x_spec = pl.BlockSpec((Q, dm.GW), lambda g, c: (cc(c), g))
    b_spec = pl.BlockSpec((Q, SSD_STATE), lambda g, c: (cc(c), nb + g))
    c_spec = pl.BlockSpec((Q, SSD_STATE), lambda g, c: (cc(c), nb + dm.G + g))
    dt_spec = pl.BlockSpec((1, Q, LANES), lambda g, c: (g, cc(c), 0))
    at_spec = pl.BlockSpec((1, 1, SUBLANES, Q), lambda g, c: (g, cc(c), 0, 0))
    par_spec = pl.BlockSpec((1, SUBLANES, LANES), lambda g, c: (g, 0, 0))
    h_spec = pl.BlockSpec((1, 1, dm.GW, SSD_STATE), lambda g, c: (cc(c), g, 0, 0))
    return x_spec, b_spec, c_spec, dt_spec, at_spec, par_spec, h_spec


def ssd_fwd(xa, prep, par, dm, name):
    Q = _ssd_chunk(dm)
    nc, P, R = dm.L // Q, SSD_HEAD_DIM, dm.R
    x_spec, b_spec, c_spec, dt_spec, at_spec, par_spec, h_spec = _ssd_specs(dm, Q, False)

    def body(x_ref, b_ref, c_ref, dt_ref, ac_ref, at_ref, par_ref, y_ref, hp_ref, h_scr):
        @pl.when(pl.program_id(1) == 0)
        def _():
            h_scr[...] = jnp.zeros_like(h_scr)

        tri, bm, cm, gm = _ssd_common((x_ref, b_ref, c_ref), Q)
        par, dt, acum, acum_t = par_ref[0], dt_ref[0], ac_ref[0], at_ref[0, 0]
        x = x_ref[...]
        hp = h_scr[...]
        hp_ref[0, 0] = hp
        for r in range(R):
            col, lam, m, xh, xs, a_last = _ssd_head(r, Q, x, dt, tri, acum, acum_t, gm)
            hpr = hp[r * P:(r + 1) * P, :]
            ydiag = jnp.dot(m.astype(BF16), xs.astype(BF16), preferred_element_type=F32)
            yoff = jnp.exp(col) * lax.dot_general(cm, hpr.astype(BF16), (_DOT_DIMS["nt"], ((), ())),
                                                  preferred_element_type=F32)
            dte = jnp.exp(a_last - col)
            st = lax.dot_general((xs * dte).astype(BF16), bm, (_DOT_DIMS["tn"], ((), ())),
                                 preferred_element_type=F32)
            h_scr[r * P:(r + 1) * P, :] = jnp.exp(a_last) * hpr + st
            y_ref[:, r * P:(r + 1) * P] = ydiag + yoff + par[2:3, r:r + 1] * xh

    return pl.pallas_call(
        body, name=name, grid=(dm.G, nc),
        in_specs=[x_spec, b_spec, c_spec, dt_spec, dt_spec, at_spec, par_spec],
        out_specs=[x_spec, h_spec],
        out_shape=[SDS((dm.L, dm.SW), F32), SDS((nc, dm.G, dm.GW, SSD_STATE), F32)],
        scratch_shapes=[pltpu.VMEM((dm.GW, SSD_STATE), F32)],
        compiler_params=_cp(("arbitrary", "arbitrary"), 24 << 20))(xa, xa, xa, *prep, par)


def ssd_bwd(xa, prep, par, hprev, dyo, dm, name):
    Q = _ssd_chunk(dm)
    nc, P, R, N = dm.L // Q, SSD_HEAD_DIM, dm.R, SSD_STATE
    x_spec, b_spec, c_spec, dt_spec, at_spec, par_spec, h_spec = _ssd_specs(dm, Q, True)
    bc_spec = pl.BlockSpec((Q, N), lambda g, c: (nc - 1 - c, g))
    NT, TN = (_DOT_DIMS["nt"], ((), ())), (_DOT_DIMS["tn"], ((), ()))

    def body(x_ref, b_ref, c_ref, dt_ref, ac_ref, at_ref, par_ref, hp_ref, dy_ref,
             dx_ref, db_ref, dc_ref, dac_ref, span_ref, ddt_ref, gs_ref, dh_scr):
        @pl.when(pl.program_id(1) == 0)
        def _():
            dh_scr[...] = jnp.zeros_like(dh_scr)
            gs_ref[...] = jnp.zeros_like(gs_ref)

        tri, bm, cm, gm = _ssd_common((x_ref, b_ref, c_ref), Q)
        par, dt, acum, acum_t = par_ref[0], dt_ref[0], ac_ref[0], at_ref[0, 0]
        x, dy, hp, dhn = x_ref[...], dy_ref[...], hp_ref[0, 0], dh_scr[...]
        lane = lax.broadcasted_iota(jnp.int32, (Q, LANES), 1)
        rowi = lax.broadcasted_iota(jnp.int32, (Q, LANES), 0)
        d_acum = jnp.zeros((Q, LANES), F32)
        d_dt = jnp.zeros((Q, LANES), F32)
        d_skip = jnp.zeros((1, LANES), F32)
        dgsum = jnp.zeros((Q, Q), F32)
        dye_all, xse_all, span_rows = [], [], []
        rq = lax.broadcasted_iota(jnp.int32, (Q, Q), 0)
        cq = lax.broadcasted_iota(jnp.int32, (Q, Q), 1)
        ue = (rq < cq).astype(BF16)
        for r in range(R):
            col, lam, m, xh, xs, a_last = _ssd_head(r, Q, x, dt, tri, acum, acum_t, gm)
            sl = slice(r * P, (r + 1) * P)
            dyh, hpr, dhr = dy[:, sl], hp[sl, :], dhn[sl, :]
            mb, xsb, dyb = m.astype(BF16), xs.astype(BF16), dyh.astype(BF16)
            e_a = jnp.exp(col)
            dte = jnp.exp(a_last - col)
            yoff = e_a * lax.dot_general(cm, hpr.astype(BF16), NT, preferred_element_type=F32)
            bdh = dte * lax.dot_general(bm, dhr.astype(BF16), NT, preferred_element_type=F32)
            dxs = lax.dot_general(mb, dyb, TN, preferred_element_type=F32) + bdh
            dm_ = lax.dot_general(dyb, xsb, NT, preferred_element_type=F32)
            dgsum = dgsum + dm_ * lam
            dye, xse = (dyh * e_a).astype(BF16), (xs * dte).astype(BF16)
            dye_all.append(dye)
            xse_all.append(xse)
            dh_scr[sl, :] = (jnp.exp(a_last) * dhr
                             + lax.dot_general(dye, cm, TN, preferred_element_type=F32))
            before = jnp.dot((dm_ * m).astype(BF16), ue, preferred_element_type=F32)
            span_rows.append(jnp.sum(jnp.where(tri, before, 0.0), axis=0, keepdims=True))
            da_col = jnp.sum(dyh * yoff - xs * bdh, axis=1, keepdims=True)
            da_last = (jnp.sum(xs * bdh, axis=(0, 1), keepdims=True)
                       + jnp.exp(a_last) * jnp.sum(dhr * hpr, axis=(0, 1), keepdims=True))
            d_acum = d_acum + jnp.where(lane == r, da_col + jnp.where(rowi == Q - 1, da_last, 0.0), 0.0)
            d_dt = d_dt + jnp.where(lane == r, jnp.sum(dxs * xh, axis=1, keepdims=True), 0.0)
            d_skip = d_skip + jnp.where(lane[0:1] == r, jnp.sum(dyh * xh, axis=(0, 1), keepdims=True), 0.0)
            dx_ref[:, sl] = dxs * dt[:, r:r + 1] + par[2:3, r:r + 1] * dyh
        dgb = dgsum.astype(BF16)
        dye_c = jnp.concatenate(dye_all, axis=1)
        xse_c = jnp.concatenate(xse_all, axis=1)
        dc_ref[...] = (jnp.dot(dgb, bm, preferred_element_type=F32)
                       + jnp.dot(dye_c, hp.astype(BF16), preferred_element_type=F32))
        db_ref[...] = (lax.dot_general(dgb, cm, TN, preferred_element_type=F32)
                       + jnp.dot(xse_c, dhn.astype(BF16), preferred_element_type=F32))
        dac_ref[0] = d_acum
        ddt_ref[0] = d_dt
        span_ref[0, 0] = jnp.concatenate(span_rows + [jnp.zeros((SUBLANES - R, Q), F32)] * (R < SUBLANES), axis=0)
        gs_ref[0, 2:3, :] += d_skip

    return pl.pallas_call(
        body, name=name, grid=(dm.G, nc),
        in_specs=[x_spec, b_spec, c_spec, dt_spec, dt_spec, at_spec, par_spec, h_spec, x_spec],
        out_specs=[x_spec, bc_spec, bc_spec, dt_spec, at_spec, dt_spec, par_spec],
        out_shape=[SDS((dm.L, dm.SW), F32), SDS((dm.L, dm.G * N), F32), SDS((dm.L, dm.G * N), F32),
                   SDS((dm.G, dm.L, LANES), F32), SDS((dm.G, nc, SUBLANES, Q), F32),
                   SDS((dm.G, dm.L, LANES), F32), SDS((dm.G, SUBLANES, LANES), F32)],
        scratch_shapes=[pltpu.VMEM((dm.GW, N), F32)],
        compiler_params=_cp(("arbitrary", "arbitrary"), 28 << 20))(xa, xa, xa, *prep, par, hprev, dyo)


def ssd_post(proj, par_all, dac, span, ddt, gs, dm, name):
    Q, G, R = _ssd_chunk(dm), dm.G, dm.R
    nc = dm.L // Q
    NT = (_DOT_DIMS["nt"], ((), ()))

    def body(dtr_ref, par_ref, dac_ref, span_ref, ddt_ref, gs_ref, out_ref, acc_ref):
        @pl.when(pl.program_id(0) == 0)
        def _():
            acc_ref[...] = jnp.zeros_like(acc_ref)

        par = par_ref[...]
        dt, a = _dt_parts(dtr_ref[...], par)
        lane = lax.broadcasted_iota(jnp.int32, (Q, LANES), 1)
        row8 = lax.broadcasted_iota(jnp.int32, (SUBLANES, Q), 0)

        def heads(v, g):
            v = jnp.where(lane[:v.shape[0]] < R, v, 0.0)
            return v if g == 0 else pltpu.roll(v, g * R, 1)

        d_acum = sum(heads(dac_ref[g], g) for g in range(G))
        d_dtx = sum(heads(ddt_ref[g], g) for g in range(G))
        d_skip = sum(heads(gs_ref[g][2:3], g) for g in range(G))
        span_t = jnp.zeros((LANES, Q), F32)
        for g in range(G):
            rows_g = jnp.concatenate([jnp.where(row8 < R, span_ref[g, 0], 0.0),
                                      jnp.zeros((LANES - SUBLANES, Q), F32)], axis=0)
            span_t = span_t + (rows_g if g == 0 else pltpu.roll(rows_g, g * R, 0))
        rq, cq = _tri(Q)
        d_da = (jnp.dot((rq <= cq).astype(F32), d_acum, precision=HIGHEST, preferred_element_type=F32)
                + lax.dot_general((rq == cq).astype(F32), span_t, NT, precision=HIGHEST,
                                  preferred_element_type=F32))
        d_raw = (d_dtx + d_da * a) * _sigmoid(dtr_ref[...] + par[0:1])
        out_ref[...] = d_raw.astype(BF16)
        acc_ref[0:1, :] += jnp.sum(d_raw, axis=0, keepdims=True)
        acc_ref[1:2, :] += jnp.sum(d_da * dt, axis=0, keepdims=True) * a
        acc_ref[2:3, :] = d_skip

    lane_blk = pl.BlockSpec((G, Q, LANES), lambda c: (0, c, 0))
    small = pl.BlockSpec((SUBLANES, LANES), lambda c: (0, 0))
    return pl.pallas_call(
        body, name=name, grid=(nc,),
        in_specs=[pl.BlockSpec((Q, LANES), lambda c: (c, dm.dt_off // LANES)), small, lane_blk,
                  pl.BlockSpec((G, 1, SUBLANES, Q), lambda c: (0, c, 0, 0)), lane_blk,
                  pl.BlockSpec((G, SUBLANES, LANES), lambda c: (0, 0, 0))],
        out_specs=[pl.BlockSpec((Q, LANES), lambda c: (c, 0)), small],
        out_shape=[SDS((dm.L, LANES), BF16), SDS((SUBLANES, LANES), F32)],
        compiler_params=_cp(("arbitrary",), 8 << 20))(proj, par_all, dac, span, ddt, gs)


def _attn_tile(dm):
    return _pick(dm.L, 256, LANES)


def attn_fwd(proj, dm, name):
    L, T, DH, AH = dm.L, _attn_tile(dm), SBA_HEAD_DIM, dm.AH
    nq = L // T
    scale = 1.0 / math.sqrt(DH)
    NT = (_DOT_DIMS["nt"], ((), ()))

    def body(q_ref, k_ref, v_ref, o_ref, tot_ref, nb_ref, ks, vs, o_scr, acc_scr):
        h, i = pl.program_id(0), pl.program_id(1)

        @pl.when(i == 0)
        def _():
            ks[...] = k_ref[...].astype(BF16)
            vs[...] = v_ref[...].astype(BF16)

        qb = q_ref[...].astype(BF16)
        rows = lax.broadcasted_iota(jnp.int32, (T, T), 0)
        cols = lax.broadcasted_iota(jnp.int32, (T, T), 1)
        causal = cols < rows
        u_rev = (rows >= cols).astype(BF16)

        def scores(j, masked):
            sl = pl.ds(pl.multiple_of(j * T, T), T)
            z = lax.dot_general(qb, ks[sl, :], NT, preferred_element_type=F32) * scale
            sp = _softplus(z)
            if masked:
                sp = jnp.where(causal, sp, 0.0)
            cs = jnp.dot(sp.astype(BF16), u_rev, preferred_element_type=F32)
            return sl, z, cs

        def weighted(blk, acc, masked):
            sl, z, cs = blk
            w = jnp.exp(z - cs - acc)
            if masked:
                w = jnp.where(causal, w, 0.0)
            return jnp.dot(w.astype(BF16), vs[sl, :], preferred_element_type=F32), acc + cs[:, 0:1]

        zero = jnp.zeros((T, 1), F32)

        @pl.when(i == 0)
        def _():
            o_scr[...], acc_scr[...] = weighted(scores(i, True), zero, True)

        @pl.when(i > 0)
        def _():
            diag, prev = scores(i, True), scores(i - 1, False)
            pv0, acc1 = weighted(diag, zero, True)
            pv1, acc2 = weighted(prev, acc1, False)
            o_scr[...] = pv0 + pv1
            acc_scr[...] = acc2

        def cond(c):
            return jnp.logical_and(c[0] >= 0, c[1] < SKIP_SUM)

        def loop(c):
            pv, acc = weighted(scores(c[0], False), acc_scr[...], False)
            o_scr[...] += pv
            acc_scr[...] = acc
            return c[0] - 1, jnp.min(acc)

        j_end, _ = lax.while_loop(cond, loop, (jnp.where(i > 0, i - 2, -1), jnp.min(acc_scr[...])))
        o_ref[...] = o_scr[...]
        tot_ref[0] = jnp.broadcast_to(acc_scr[...], (T, LANES))
        nb_ref[h, i] = i - j_end

    kv = lambda off: pl.BlockSpec((L, DH), lambda h, i: (0, off // DH + h))
    est = 2 * 2 * L * DH * 4 + 2 * L * DH * 2 + 12 * T * T * 4
    return pl.pallas_call(
        body, name=name, grid=(AH, nq),
        in_specs=[pl.BlockSpec((T, DH), lambda h, i: (i, dm.q_off // DH + h)), kv(dm.k_off), kv(dm.v_off)],
        out_specs=[pl.BlockSpec((T, DH), lambda h, i: (i, h)),
                   pl.BlockSpec((1, T, LANES), lambda h, i: (h, i, 0)),
                   pl.BlockSpec(memory_space=pltpu.SMEM)],
        out_shape=[SDS((L, dm.AW), F32), SDS((AH, L, LANES), F32), SDS((AH, nq), jnp.int32)],
        scratch_shapes=[pltpu.VMEM((L, DH), BF16), pltpu.VMEM((L, DH), BF16),
                        pltpu.VMEM((T, DH), F32), pltpu.VMEM((T, 1), F32)],
        compiler_params=_cp(("arbitrary", "arbitrary"), est))(proj, proj, proj)


def attn_bwd(proj, dyo, tot, nblk, dm, name):
    L, T, DH, AH = dm.L, _attn_tile(dm), SBA_HEAD_DIM, dm.AH
    nq = L // T
    scale = 1.0 / math.sqrt(DH)
    NT, TN = (_DOT_DIMS["nt"], ((), ())), (_DOT_DIMS["tn"], ((), ()))

    def body(nb_ref, q_ref, k_ref, v_ref, do_ref, tot_ref, dq_ref, dk_out, dv_out,
             ks, vs, dq_scr, p_scr, e_scr, dk_ref, dv_ref):
        h, i = pl.program_id(0), pl.program_id(1)

        @pl.when(i == 0)
        def _():
            ks[...] = k_ref[...].astype(BF16)
            vs[...] = v_ref[...].astype(BF16)
            dk_ref[...] = jnp.zeros_like(dk_ref)
            dv_ref[...] = jnp.zeros_like(dv_ref)

        qb = q_ref[...].astype(BF16)
        dob = do_ref[...].astype(BF16)
        tot_c = tot_ref[0][:, 0:1]
        rows = lax.broadcasted_iota(jnp.int32, (T, T), 0)
        cols = lax.broadcasted_iota(jnp.int32, (T, T), 1)
        causal = cols < rows
        u_fwd = (rows <= cols).astype(BF16)
        dq_scr[...] = jnp.zeros_like(dq_scr)
        p_scr[...] = jnp.zeros_like(p_scr)
        e_scr[...] = jnp.zeros_like(e_scr)

        def scores(j, masked):
            sl = pl.ds(pl.multiple_of(j * T, T), T)
            z = lax.dot_general(qb, ks[sl, :], NT, preferred_element_type=F32) * scale
            sp = _softplus(z)
            if masked:
                sp = jnp.where(causal, sp, 0.0)
            spb = sp.astype(BF16)
            pin = jnp.dot(spb, u_fwd, preferred_element_type=F32)
            dw = lax.dot_general(dob, vs[sl, :], NT, preferred_element_type=F32)
            return sl, z, sp, pin - spb.astype(F32), pin[:, T - 1:T], dw

        def grads(blk, before, e_before, masked):
            sl, z, sp, earlier, block_sum, dw = blk
            cs = (tot_c - before) - earlier
            w = jnp.exp(z - cs)
            if masked:
                w = jnp.where(causal, w, 0.0)
            e = dw * w
            fin = jnp.dot(e.astype(BF16), u_fwd, preferred_element_type=F32)
            dz = (e - jnp.exp(z - sp) * (e_before + fin)) * scale
            if masked:
                dz = jnp.where(causal, dz, 0.0)
            dzb = dz.astype(BF16)
            dq_scr[...] += jnp.dot(dzb, ks[sl, :], preferred_element_type=F32)
            dk_ref[sl, :] += lax.dot_general(dzb, qb, TN, preferred_element_type=F32)
            dv_ref[sl, :] += lax.dot_general(w.astype(BF16), dob, TN, preferred_element_type=F32)
            return before + block_sum, e_before + fin[:, T - 1:T]

        def loop(j, carry):
            p_scr[...], e_scr[...] = grads(scores(j, False), p_scr[...], e_scr[...], False)
            return carry

        lax.fori_loop(i - nb_ref[h, i] + 1, i - 1, loop, 0)

        @pl.when(i == 0)
        def _():
            grads(scores(i, True), p_scr[...], e_scr[...], True)

        @pl.when(i > 0)
        def _():
            prev, diag = scores(i - 1, False), scores(i, True)
            p1, e1 = grads(prev, p_scr[...], e_scr[...], False)
            grads(diag, p1, e1, True)

        dq_ref[...] = dq_scr[...].astype(BF16)

        @pl.when(i == nq - 1)
        def _():
            dk_out[...] = dk_ref[...].astype(BF16)
            dv_out[...] = dv_ref[...].astype(BF16)

    kv = lambda off: pl.BlockSpec((L, DH), lambda h, i, nb: (0, off // DH + h))
    qblk = lambda off: pl.BlockSpec((T, DH), lambda h, i, nb: (i, off // DH + h))
    acc = pl.BlockSpec((L, DH), lambda h, i, nb: (0, h))
    est = 2 * 2 * L * DH * 4 * 2 + 2 * L * DH * 2 + 16 * T * T * 4
    grid_spec = pltpu.PrefetchScalarGridSpec(
        num_scalar_prefetch=1, grid=(AH, nq),
        in_specs=[qblk(dm.q_off), kv(dm.k_off), kv(dm.v_off), qblk(dm.SW),
                  pl.BlockSpec((1, T, LANES), lambda h, i, nb: (h, i, 0))],
        out_specs=[qblk(0), acc, acc],
        scratch_shapes=[pltpu.VMEM((L, DH), BF16), pltpu.VMEM((L, DH), BF16),
                        pltpu.VMEM((T, DH), F32), pltpu.VMEM((T, 1), F32), pltpu.VMEM((T, 1), F32),
                        pltpu.VMEM((L, DH), F32), pltpu.VMEM((L, DH), F32)])
    return pl.pallas_call(
        body, name=name, grid_spec=grid_spec,
        out_shape=[SDS((L, dm.AW), BF16)] * 3,
        compiler_params=_cp(("arbitrary", "arbitrary"), est))(nblk, proj, proj, proj, dyo, tot)


def _gate_specs(dm, tm, order):
    GW, G = dm.GW, dm.G
    ix = (lambda a, b: (a, b)) if order == "ij" else (lambda a, b: (b, a))

    def spec(colfn):
        return pl.BlockSpec((tm, GW), lambda p0, p1: (ix(p0, p1)[0], colfn(ix(p0, p1)[1])))

    y_spec = spec(lambda j: jnp.minimum(j, G - 1))
    o_spec = spec(lambda j: jnp.maximum(j - G, 0))
    zg_spec = spec(lambda j: jnp.where(j < G, j, dm.g_off // GW + j - G))
    w_spec = pl.BlockSpec((1, GW), lambda p0, p1: (0, jnp.minimum(ix(p0, p1)[1], G - 1)))
    full = spec(lambda j: j)
    return y_spec, o_spec, zg_spec, w_spec, full


def gate_fwd(y, o, proj, snw, dm, name):
    L, GW, G = dm.L, dm.GW, dm.G
    tm = _pick(L, 512, SUBLANES)
    ncol = (dm.SW + dm.AW) // GW
    y_spec, o_spec, zg_spec, w_spec, full = _gate_specs(dm, tm, "ij")

    def body(y_ref, o_ref, zg_ref, w_ref, m_ref):
        j = pl.program_id(1)
        zg = zg_ref[...]
        gate = zg * _sigmoid(zg)

        @pl.when(j < G)
        def _():
            yz = y_ref[...] * gate
            r = lax.rsqrt(jnp.mean(yz * yz, axis=-1, keepdims=True) + EPS)
            m_ref[...] = (yz * r * w_ref[...]).astype(BF16)

        @pl.when(j >= G)
        def _():
            m_ref[...] = (o_ref[...] * gate).astype(BF16)

    return pl.pallas_call(
        body, name=name, grid=(L // tm, ncol),
        in_specs=[y_spec, o_spec, zg_spec, w_spec], out_specs=full,
        out_shape=SDS((L, dm.SW + dm.AW), BF16),
        compiler_params=_cp(("arbitrary", "arbitrary"), 2 * tm * GW * 16))(y, o, proj, snw)


def gate_bwd(dmix, y, o, proj, snw, dm, name):
    L, GW, G = dm.L, dm.GW, dm.G
    tm = _pick(L, 512, SUBLANES)
    W = dm.SW + dm.AW
    y_spec, o_spec, zg_spec, w_spec, full = _gate_specs(dm, tm, "ji")

    def body(d_ref, y_ref, o_ref, zg_ref, w_ref, dyo_ref, dzg_ref, dw_ref):
        j = pl.program_id(0)

        @pl.when(pl.program_id(1) == 0)
        def _():
            dw_ref[...] = jnp.zeros_like(dw_ref)

        zg, d = zg_ref[...], d_ref[...]
        sg = _sigmoid(zg)
        gate = zg * sg
        dgate = sg * (1.0 + zg * (1.0 - sg))

        @pl.when(j < G)
        def _():
            yv = y_ref[...]
            yz = yv * gate
            r = lax.rsqrt(jnp.mean(yz * yz, axis=-1, keepdims=True) + EPS)
            nrm = yz * r
            dw_ref[0:1, :] += jnp.sum(d * nrm, axis=0, keepdims=True)
            dn = d * w_ref[...]
            dyz = r * (dn - nrm * jnp.mean(dn * nrm, axis=-1, keepdims=True))
            dyo_ref[...] = dyz * gate
            dzg_ref[...] = (dyz * yv * dgate).astype(BF16)

        @pl.when(j >= G)
        def _():
            dyo_ref[...] = d * gate
            dzg_ref[...] = (d * o_ref[...] * dgate).astype(BF16)

    return pl.pallas_call(
        body, name=name, grid=(W // GW, L // tm),
        in_specs=[full, y_spec, o_spec, zg_spec, w_spec],
        out_specs=[full, full, pl.BlockSpec((SUBLANES, GW), lambda j, i: (0, j))],
        out_shape=[SDS((L, W), F32), SDS((L, W), BF16), SDS((SUBLANES, W), F32)],
        compiler_params=_cp(("arbitrary", "arbitrary"), 2 * tm * GW * 24))(dmix, y, o, proj, snw)


def adamw(parts, w, m, v, name):
    R, C = w.shape
    n_slot = parts.shape[0]
    tr = _pick(R, max(SUBLANES, (1 << 18) // C // SUBLANES * SUBLANES), SUBLANES)
    c1, c2 = 1.0 - ADAM_B1 ** ADAM_STEP, 1.0 - ADAM_B2 ** ADAM_STEP

    def body(p_ref, w_ref, m_ref, v_ref, g_ref, d_ref, m2_ref, v2_ref):
        g = p_ref[0].astype(F32)
        for s in range(1, n_slot):
            g = g + p_ref[s].astype(F32)
        m2 = ADAM_B1 * m_ref[...] + (1.0 - ADAM_B1) * g
        v2 = ADAM_B2 * v_ref[...] + (1.0 - ADAM_B2) * (g * g)
        g_ref[...] = g
        m2_ref[...] = m2
        v2_ref[...] = v2
        d_ref[...] = -ADAM_LR * ((m2 / c1) / (jnp.sqrt(v2 / c2) + ADAM_EPS) + ADAM_WD * w_ref[...])

    blk = pl.BlockSpec((tr, C), lambda i: (i, 0))
    return pl.pallas_call(
        body, name=name, grid=(R // tr,),
        in_specs=[pl.BlockSpec((n_slot, tr, C), lambda i: (0, i, 0)), blk, blk, blk],
        out_specs=[blk] * 4, out_shape=[SDS((R, C), F32)] * 4,
        compiler_params=_cp(("arbitrary",), 2 * tr * C * (n_slot * 4 + 28)))(parts, w, m, v)


N_CHIP = 4


def _place():
    x, y, c = lax.axis_index("x"), lax.axis_index("y"), lax.axis_index("c")
    return x, y, c, [(1 - x, y), (x, 1 - y), (1 - x, 1 - y)]


def _comm_call(body, arrays, out_shape, n_sems, n_local, name):
    hbm = pl.BlockSpec(memory_space=pltpu.HBM)
    return pl.pallas_call(
        body, name=name, in_specs=[hbm] * len(arrays), out_specs=[hbm] * len(out_shape), out_shape=out_shape,
        scratch_shapes=[pltpu.SemaphoreType.DMA((n_sems,)), pltpu.SemaphoreType.DMA((n_sems,)),
                        pltpu.SemaphoreType.DMA((n_local,))],
        compiler_params=pltpu.CompilerParams(has_side_effects=True))(*arrays)


def gather_weights(arrays, name):
    n, per = len(arrays), N_DEV - 1

    def body(*refs):
        srcs, dsts = refs[:n], refs[n:2 * n]
        send_sems, recv_sems, local_sems = refs[2 * n:]
        start, finish = _gather_halves(srcs, dsts, send_sems, recv_sems, local_sems)
        start()
        finish()

    out_shape = [SDS((N_DEV,) + a.shape, a.dtype) for a in arrays]
    return _comm_call(body, arrays, out_shape, n * per, n, name)


def _gather_halves(srcs, dsts, send_sems, recv_sems, local_sems):
    n, per = len(srcs), N_DEV - 1

    def parts():
        x, y, c, chips = _place()
        me, sib = 4 * x + 2 * y + c, (x, y, 1 - c)

        def cp(a, k, block, to, src=None):
            return pltpu.make_async_remote_copy(
                src_ref=dsts[a].at[block] if src is None else src, dst_ref=dsts[a].at[block],
                send_sem=send_sems.at[a * per + k], recv_sem=recv_sems.at[a * per + k],
                device_id=to, device_id_type=MESH)

        own = [pltpu.make_async_copy(srcs[a], dsts[a].at[me], local_sems.at[a]) for a in range(n)]
        first = []
        for a in range(n):
            first.append(cp(a, 0, me, sib, src=srcs[a]))
            first += [cp(a, 1 + j, me, (px, py, c), src=srcs[a]) for j, (px, py) in enumerate(chips)]
        return x, y, c, chips, sib, cp, own, first

    def start():
        *_, own, first = parts()
        for o in own:
            o.start()
        for f in first:
            f.start()

    def finish():
        x, y, c, chips, sib, cp, own, first = parts()
        passed = []
        for j, (px, py) in enumerate(chips):
            block = 4 * px + 2 * py + c
            for a in range(n):
                cp(a, 1 + j, block, sib).wait_recv()
                fwd = cp(a, 4 + j, block, sib)
                fwd.start()
                passed.append(fwd)
        for a in range(n):
            cp(a, 0, 4 * x + 2 * y + 1 - c, sib).wait_recv()
            for j, (px, py) in enumerate(chips):
                cp(a, 4 + j, 4 * px + 2 * py + 1 - c, sib).wait_recv()
        for f in first + passed:
            f.wait_send()
        for o in own:
            o.wait()

    return start, finish


def gather_beside(shards):
    shards = list(shards)
    return Exchange(arrays=shards, out_shape=[SDS((N_DEV,) + s.shape, s.dtype) for s in shards],
                    n_sems=len(shards) * (N_DEV - 1), n_local=len(shards), halves=_gather_halves)


def pair_exchange(arrays, name):
    n = len(arrays)

    def body(*refs):
        srcs, dsts = refs[:n], refs[n:2 * n]
        send_sems, recv_sems, _ = refs[2 * n:]
        x, y, c, _chips = _place()
        sib = (x, y, 1 - c)

        def cp(a, k):
            return pltpu.make_async_remote_copy(
                src_ref=srcs[a].at[2 * k + 1 - c], dst_ref=dsts[a].at[k],
                send_sem=send_sems.at[a * N_CHIP + k], recv_sem=recv_sems.at[a * N_CHIP + k],
                device_id=sib, device_id_type=MESH)

        cps = [cp(a, k) for k in range(N_CHIP) for a in range(n)]
        for p in cps:
            p.start()
        for p in cps:
            p.wait_recv()
        for p in cps:
            p.wait_send()

    out_shape = [SDS((N_CHIP,) + a.shape[1:], a.dtype) for a in arrays]
    return _comm_call(body, arrays, out_shape, n * N_CHIP, 1, name)


def pair_add(parts, got, name):
    _, R, C = parts.shape
    tr = _pick(R, max(16, (1 << 19) // C // 16 * 16), 16)
    core = lax.axis_index("c").astype(jnp.int32).reshape(1)

    def body(c_ref, p_ref, g_ref, o_ref):
        o_ref[...] = (p_ref[...].astype(F32) + g_ref[...].astype(F32)).astype(o_ref.dtype)

    grid_spec = pltpu.PrefetchScalarGridSpec(
        num_scalar_prefetch=1, grid=(N_CHIP, R // tr),
        in_specs=[pl.BlockSpec((1, tr, C), lambda k, i, c_ref: (2 * k + c_ref[0], i, 0)),
                  pl.BlockSpec((1, tr, C), lambda k, i, c_ref: (k, i, 0))],
        out_specs=pl.BlockSpec((1, tr, C), lambda k, i, c_ref: (k, i, 0)))
    return pl.pallas_call(
        body, name=name, grid_spec=grid_spec, out_shape=SDS((N_CHIP, R, C), parts.dtype),
        compiler_params=_cp(("arbitrary", "arbitrary"), 2 * 3 * tr * C * 2 + 3 * tr * C * 4))(core, parts, got)


def chip_exchange_spec(sums, full=(), split=()):
    ns, nf, nsp = len(sums), len(full), len(split)
    n_sem = 3 * ns + (N_DEV - 1) * (nf + nsp)

    def copies(srcs, dsts, send_sems, recv_sems, local_sems):
        x, y, c, chips = _place()
        me, my_chip = 4 * x + 2 * y + c, 2 * x + y
        started, arrivals, own = [], [], []
        for a in range(ns):
            own.append(pltpu.make_async_copy(srcs[a].at[my_chip], dsts[a].at[my_chip], local_sems.at[a]))
            for j, (px, py) in enumerate(chips):
                k = 2 * px + py
                sem = 3 * a + j
                started.append(pltpu.make_async_remote_copy(
                    src_ref=srcs[a].at[k], dst_ref=dsts[a].at[my_chip],
                    send_sem=send_sems.at[sem], recv_sem=recv_sems.at[sem],
                    device_id=(px, py, c), device_id_type=MESH))
                arrivals.append(dict(
                    src_ref=srcs[a].at[my_chip], dst_ref=dsts[a].at[k],
                    send_sem=send_sems.at[sem], recv_sem=recv_sems.at[sem],
                    device_id=(px, py, c), device_id_type=MESH))
        for b in range(nf + nsp):
            a = ns + b
            is_split = b >= nf
            own.append(pltpu.make_async_copy(srcs[a].at[me] if is_split else srcs[a], dsts[a].at[me],
                                             local_sems.at[a]))
            for rel in range(1, N_DEV):
                px, py, pc = x ^ ((rel >> 2) & 1), y ^ ((rel >> 1) & 1), c ^ (rel & 1)
                pidx = 4 * px + 2 * py + pc
                sem = 3 * ns + b * (N_DEV - 1) + rel - 1
                started.append(pltpu.make_async_remote_copy(
                    src_ref=srcs[a].at[pidx] if is_split else srcs[a], dst_ref=dsts[a].at[me],
                    send_sem=send_sems.at[sem], recv_sem=recv_sems.at[sem],
                    device_id=(px, py, pc), device_id_type=MESH))
                arrivals.append(dict(
                    src_ref=srcs[a].at[me] if is_split else srcs[a], dst_ref=dsts[a].at[pidx],
                    send_sem=send_sems.at[sem], recv_sem=recv_sems.at[sem],
                    device_id=(px, py, pc), device_id_type=MESH))
        return own, started, arrivals

    def halves(*refs):
        def start():
            own, started, _ = copies(*refs)
            for o in own:
                o.start()
            for s in started:
                s.start()

        def finish():
            own, started, arrivals = copies(*refs)
            for r in arrivals:
                pltpu.make_async_remote_copy(**r).wait_recv()
            for s in started:
                s.wait_send()
            for o in own:
                o.wait()

        return start, finish

    out_shape = ([SDS(a.shape, a.dtype) for a in sums] + [SDS((N_DEV,) + a.shape, a.dtype) for a in full]
                 + [SDS(a.shape, a.dtype) for a in split])
    return Exchange(arrays=list(sums) + list(full) + list(split), out_shape=out_shape, n_sems=n_sem,
                    n_local=ns + nf + nsp, halves=halves)


def chip_exchange(sums, full, split, name):
    ex = chip_exchange_spec(sums, full, split)
    n = len(ex.arrays)

    def body(*refs):
        start, finish = ex.halves(refs[:n], refs[n:2 * n], *refs[2 * n:])
        start()
        finish()

    return _comm_call(body, ex.arrays, ex.out_shape, ex.n_sems, ex.n_local, name)


class LayerParams(NamedTuple):
    nw: jax.Array
    w_in: jax.Array
    cw: jax.Array
    cb: jax.Array
    par: jax.Array
    par_all: jax.Array
    snw: jax.Array
    w_out: jax.Array


def head_params(dt_bias, a_log, d_skip, dm):
    rows = jnp.stack([dt_bias, a_log, d_skip])
    par_all = jnp.pad(rows, ((0, SUBLANES - 3), (0, LANES - dm.NH)))
    par = jnp.pad(rows.reshape(3, dm.G, dm.R).transpose(1, 0, 2), ((0, 0), (0, SUBLANES - 3), (0, LANES - dm.R)))
    return par, par_all


def layer_fwd(x, p, dm, tag, next_shards=None):
    h = rms_fwd(x, p.nw, f"rms_fwd{tag}")
    gathered = None
    if next_shards is None:
        proj = mm(h, p.w_in, "nn", tm=512, tn=1920, tk=dm.D, name=f"in_proj{tag}", b_outer=True)
    else:
        proj, gathered = mm(h, p.w_in, "nn", tm=512, tn=1920, tk=dm.D, name=f"in_proj_gather{tag}", b_outer=True,
                            exchange=gather_beside(next_shards))
    xa = conv_fwd(proj, p.cw, p.cb, dm, f"conv_fwd{tag}")
    prep = ssd_prep(proj, p.par_all, dm, f"ssd_prep{tag}")
    y, hprev = ssd_fwd(xa, prep, p.par, dm, f"ssd_fwd{tag}")
    o, tot, nblk = attn_fwd(proj, dm, f"attn_fwd{tag}")
    mix = gate_fwd(y, o, proj, p.snw, dm, f"gate_fwd{tag}")
    xn = mm(mix, p.w_out, "nn", tm=512, tn=1024, tk=dm.SW + dm.AW, name=f"out_proj{tag}", res=x)
    return xn, (x, h, proj, xa, prep, y, hprev, o, tot, nblk, mix), gathered


def layer_bwd(dxn, saved, p, dm, tag, exchange=None):
    x, h, proj, xa, prep, y, hprev, o, tot, nblk, mix = saved
    dmix = mm(dxn, p.w_out, "nt", tm=512, tn=1024, tk=dm.D, name=f"d_mix{tag}")
    dw_out = mm(mix, dxn, "tn", tm=1024, tn=dm.D, tk=512, name=f"dw_out{tag}")
    dyo, dzg, dsnw = gate_bwd(dmix, y, o, proj, p.snw, dm, f"gate_bwd{tag}")
    dq, dk, dv = attn_bwd(proj, dyo, tot, nblk, dm, f"attn_bwd{tag}")
    dxs, db, dc, dac, span, ddtx, gsk = ssd_bwd(xa, prep, p.par, hprev, dyo, dm, f"ssd_bwd{tag}")
    ddt_blk, ghead = ssd_post(proj, p.par_all, dac, span, ddtx, gsk, dm, f"ssd_post{tag}")
    dcv, gconv = conv_bwd_pre(proj, dxs, db, dc, p.cw, p.cb, dm, f"conv_bwd_pre{tag}")
    dxbc = conv_bwd_in(dcv, p.cw, dm, f"conv_bwd_in{tag}")
    dproj = jnp.concatenate([dzg[:, :dm.SW], dxbc, dq, dk, dv, dzg[:, dm.SW:], ddt_blk], axis=1)
    brought = None
    if exchange is None:
        dh = mm(dproj, p.w_in, "nt", tm=512, tn=dm.D, tk=1920, name=f"d_h{tag}")
    else:
        dh, brought = mm(dproj, p.w_in, "nt", tm=512, tn=dm.D, tk=1920, name=f"d_h_exchange{tag}", exchange=exchange)
    dw_in = mm(h, dproj, "tn", tm=dm.D, tn=960, tk=2048, name=f"dw_in{tag}")
    dx, dnw = rms_bwd(dh, x, p.nw, dxn, f"rms_bwd{tag}")
    small = dict(norm_w=dnw[0], conv_w=gconv[:SSD_CONV], conv_b=gconv[SSD_CONV],
                 dt_bias=ghead[0, :dm.NH], a_log=ghead[1, :dm.NH], d_skip=ghead[2, :dm.NH],
                 ssd_norm_w=dsnw[0, :dm.SW])
    return dx, dw_in, dw_out, small, brought


SMALL = ("norm_w", "conv_b", "dt_bias", "a_log", "d_skip", "ssd_norm_w")


def _to_mine(w, dm):
    a, b = dm.SW + dm.CD, dm.SW + dm.CD + dm.NH
    pad = jnp.zeros((w.shape[0], LANES - dm.NH), w.dtype)
    return jnp.concatenate([w[:, :a], w[:, b:], w[:, a:b], pad], axis=1)


def _from_mine(w, dm):
    a = dm.SW + dm.CD
    return jnp.concatenate([w[:, :a], w[:, dm.dt_off:dm.dt_off + dm.NH], w[:, a:dm.dt_off]], axis=1)


def _pack(pieces):
    flat = jnp.concatenate([p.reshape(-1) for p in pieces])
    rows = -(-flat.shape[0] // LANES)
    rows = -(-rows // SUBLANES) * SUBLANES
    return jnp.pad(flat, (0, rows * LANES - flat.shape[0])).reshape(rows, LANES)


def _unpack(buf, shapes):
    flat, out, at = buf.reshape(-1), [], 0
    for s in shapes:
        n = math.prod(s)
        out.append(flat[at:at + n].reshape(s))
        at += n
    return out


def kernel(x, norm_w, w_in, conv_w, conv_b, dt_bias, a_log, d_skip, ssd_norm_w, w_out, final_norm_w, loss_target, m_norm_w, m_w_in, m_conv_w, m_conv_b, m_dt_bias, m_a_log, m_d_skip, m_ssd_norm_w, m_w_out, m_final_norm_w, v_norm_w, v_w_in, v_conv_w, v_conv_b, v_dt_bias, v_a_log, v_d_skip, v_ssd_norm_w, v_w_out, v_final_norm_w):
    depth, D = norm_w.shape
    L = x.shape[1]
    NH = dt_bias.shape[1]
    SW = NH * SSD_HEAD_DIM
    CD = conv_b.shape[1]
    dm = Dims(L=L, D=D, SW=SW, G=(CD - SW) // (2 * SSD_STATE), AW=w_out.shape[1] * N_DEV - SW)
    ncol, csh, osh = w_in.shape[2], conv_w.shape[2], w_out.shape[1]
    me = 4 * lax.axis_index("x") + 2 * lax.axis_index("y") + lax.axis_index("c")

    shards = [(w_in[l].astype(BF16), w_out[l].astype(BF16), conv_w[l]) for l in range(depth)]

    def layer_params(l, gathered):
        g_in, g_out, g_cw = gathered
        full_in = g_in.transpose(1, 0, 2).reshape(D, N_DEV * ncol)
        full_cw = g_cw.transpose(1, 0, 2).reshape(SSD_CONV, CD)
        par, par_all = head_params(dt_bias[l], a_log[l], d_skip[l], dm)
        return LayerParams(
            nw=norm_w[l][None], w_in=_to_mine(full_in, dm), cw=full_cw, cb=conv_b[l][None], par=par, par_all=par_all,
            snw=ssd_norm_w[l][None], w_out=g_out.reshape(N_DEV * osh, D))

    h = x[0]
    params, saved = [], []
    gathered = gather_weights(list(shards[0]), "gather_weights")
    for l in range(depth):
        params.append(layer_params(l, gathered))
        h, s, gathered = layer_fwd(h, params[l], dm, "", shards[l + 1] if l + 1 < depth else None)
        saved.append(s)
    dh, dfw, ls = loss_head(h, final_norm_w[None], loss_target[0], "loss_head")
    loss = lax.psum(ls[0, 0], ("x", "y", "c"))
    smalls, r_in, r_out = [None] * depth, [None] * depth, [None] * depth
    pending = None
    for l in reversed(range(depth)):
        ex = None if pending is None else chip_exchange_spec(pending)
        dh, gw_in, gw_out, smalls[l], brought = layer_bwd(dh, saved[l], params[l], dm, "", ex)
        if brought is not None:
            r_in[l + 1], r_out[l + 1] = brought
        p_in = _from_mine(gw_in, dm).astype(BF16).reshape(D, N_DEV, ncol).transpose(1, 0, 2)
        p_out = gw_out.astype(BF16).reshape(N_DEV, osh, D)
        s_in, s_out = pair_exchange([p_in, p_out], "pair_exchange")
        pending = [pair_add(p_in, s_in, "pair_add_w_in"), pair_add(p_out, s_out, "pair_add_w_out")]
    grad_x = dh[None]
    rep = [jnp.stack([s[k] for s in smalls]) for k in SMALL] + [dfw[0]]
    rep_shapes = [r.shape for r in rep]
    p_rep = _pack(rep)
    p_cw = jnp.stack([s["conv_w"] for s in smalls]).reshape(depth * SSD_CONV, N_DEV, csh).transpose(1, 0, 2)
    r_in[0], r_out[0], r_rep, r_cw = chip_exchange(pending, [p_rep], [p_cw], "chip_exchange")
    r_in, r_out = jnp.concatenate(r_in, axis=1), jnp.concatenate(r_out, axis=1)

    out_in = adamw(r_in, w_in.reshape(depth * D, ncol), m_w_in.reshape(depth * D, ncol),
                   v_w_in.reshape(depth * D, ncol), "adamw_w_in")
    out_out = adamw(r_out, w_out.reshape(depth * osh, D), m_w_out.reshape(depth * osh, D),
                    v_w_out.reshape(depth * osh, D), "adamw_w_out")
    out_cw = adamw(r_cw, conv_w.reshape(depth * SSD_CONV, csh), m_conv_w.reshape(depth * SSD_CONV, csh),
                   v_conv_w.reshape(depth * SSD_CONV, csh), "adamw_conv_w")
    rep_w = [norm_w, conv_b, dt_bias, a_log, d_skip, ssd_norm_w, final_norm_w]
    rep_m = [m_norm_w, m_conv_b, m_dt_bias, m_a_log, m_d_skip, m_ssd_norm_w, m_final_norm_w]
    rep_v = [v_norm_w, v_conv_b, v_dt_bias, v_a_log, v_d_skip, v_ssd_norm_w, v_final_norm_w]
    out_rep = adamw(r_rep, _pack(rep_w), _pack(rep_m), _pack(rep_v), "adamw_replicated")

    outs = {}
    for kind, i in (("grad", 0), ("delta", 1), ("new_m", 2), ("new_v", 3)):
        r = dict(zip(SMALL + ("final_norm_w",), _unpack(out_rep[i], rep_shapes)))
        r["w_in"] = out_in[i].reshape(w_in.shape)
        r["w_out"] = out_out[i].reshape(w_out.shape)
        r["conv_w"] = out_cw[i].reshape(conv_w.shape)
        outs[kind] = r
    order = ("norm_w", "w_in", "conv_w", "conv_b", "dt_bias", "a_log", "d_skip", "ssd_norm_w", "w_out", "final_norm_w")
    return (loss, grad_x, *[outs[k][n] for k in ("grad", "delta", "new_m", "new_v") for n in order])
```

```python
import functools
import math
from typing import NamedTuple

import jax
import jax.numpy as jnp
from jax import lax
from jax.experimental import pallas as pl
from jax.experimental.pallas import tpu as pltpu

F32, BF16 = jnp.float32, jnp.bfloat16
SDS = jax.ShapeDtypeStruct
EPS = 1e-6
LANES = 128
SUBLANES = 8
VMEM_BYTES = 64 * 2 ** 20
N_DEV = 8
SSD_HEAD_DIM = 64
SSD_STATE = 128
SSD_CONV = 4
SBA_HEAD_DIM = 128
ADAM_LR, ADAM_B1, ADAM_B2, ADAM_EPS, ADAM_WD, ADAM_STEP = 0.001, 0.9, 0.999, 1e-08, 0.01, 10
SKIP_SUM = 110.0
HIGHEST = lax.Precision.HIGHEST
MESH = pl.DeviceIdType.MESH


class Dims(NamedTuple):
    L: int
    D: int
    SW: int
    G: int
    AW: int

    @property
    def NH(self): return self.SW // SSD_HEAD_DIM
    @property
    def R(self): return self.NH // self.G
    @property
    def GW(self): return self.SW // self.G
    @property
    def CD(self): return self.SW + 2 * self.G * SSD_STATE
    @property
    def AH(self): return self.AW // SBA_HEAD_DIM
    @property
    def q_off(self): return self.SW + self.CD
    @property
    def k_off(self): return self.q_off + self.AW
    @property
    def v_off(self): return self.q_off + 2 * self.AW
    @property
    def g_off(self): return self.q_off + 3 * self.AW
    @property
    def dt_off(self): return self.q_off + 4 * self.AW
    @property
    def NP(self): return self.dt_off + LANES


def _pick(n, target, mult):
    t = (min(target, n) // mult) * mult
    while t >= mult:
        if n % t == 0:
            return t
        t -= mult
    return n


def _cp(sem, vmem_est):
    limit = int(min(max(vmem_est * 5 // 4 + (4 << 20), 32 << 20), VMEM_BYTES - (8 << 20)))
    return pltpu.CompilerParams(dimension_semantics=sem, vmem_limit_bytes=limit)


def _sigmoid(x):
    return 1.0 / (1.0 + jnp.exp(-x))


def _softplus(x):
    return jnp.maximum(x, 0.0) + jnp.log(1.0 + jnp.exp(-jnp.abs(x)))


def _nbytes(shape, dtype):
    return math.prod(shape) * jnp.dtype(dtype).itemsize


_DOT_DIMS = {"nn": ((1,), (0,)), "nt": ((1,), (1,)), "tn": ((0,), (0,))}


class Exchange(NamedTuple):
    arrays: list
    out_shape: list
    n_sems: int
    n_local: int
    halves: object


def mm(a, b, mode, *, tm, tn, tk, name, res=None, b_outer=False, exchange=None, out_dtype=F32):
    if mode == "nn":
        (M, K), N = a.shape, b.shape[1]
    elif mode == "nt":
        (M, K), N = a.shape, b.shape[0]
    else:
        (K, M), N = a.shape, b.shape[1]
    tm, tn, tk = _pick(M, tm, LANES), _pick(N, tn, LANES), _pick(K, tk, LANES)
    nk = K // tk

    def ij(p0, p1):
        return (p1, p0) if b_outer else (p0, p1)

    if mode == "tn":
        a_spec = pl.BlockSpec((tk, tm), lambda p0, p1, k: (k, ij(p0, p1)[0]))
    else:
        a_spec = pl.BlockSpec((tm, tk), lambda p0, p1, k: (ij(p0, p1)[0], k))
    if mode == "nt":
        b_spec = pl.BlockSpec((tn, tk), lambda p0, p1, k: (ij(p0, p1)[1], k))
    else:
        b_spec = pl.BlockSpec((tk, tn), lambda p0, p1, k: (k, ij(p0, p1)[1]))
    o_spec = pl.BlockSpec((tm, tn), lambda p0, p1, k: ij(p0, p1))
    dims = (_DOT_DIMS[mode], ((), ()))

    grid = (N // tn, M // tm, nk) if b_outer else (M // tm, N // tn, nk)
    n_res = 0 if res is None else 1
    n_ex = 0 if exchange is None else len(exchange.arrays)

    def body(*refs):
        a_ref, b_ref, o_ref = refs[0], refs[1], refs[2 + n_res + n_ex]
        ids = [pl.program_id(d) for d in range(3)]
        if exchange is not None:
            srcs = refs[2 + n_res:2 + n_res + n_ex]
            dsts = refs[3 + n_res + n_ex:3 + n_res + 2 * n_ex]
            start, finish = exchange.halves(srcs, dsts, *refs[3 + n_res + 2 * n_ex:])

            @pl.when(jnp.logical_and(jnp.logical_and(ids[0] == 0, ids[1] == 0), ids[2] == 0))
            def _():
                start()

        part = lax.dot_general(a_ref[...].astype(BF16), b_ref[...].astype(BF16), dims,
                               preferred_element_type=F32)
        if res is not None:
            first = part + refs[2][...]
        else:
            first = part
        if nk == 1:
            o_ref[...] = first.astype(out_dtype)
        else:
            acc_ref = o_ref if out_dtype == F32 else refs[-1]

            @pl.when(ids[2] == 0)
            def _():
                acc_ref[...] = first

            @pl.when(ids[2] > 0)
            def _():
                acc_ref[...] += part

            if out_dtype != F32:
                @pl.when(ids[2] == nk - 1)
                def _():
                    o_ref[...] = acc_ref[...].astype(out_dtype)

        if exchange is not None:
            @pl.when(jnp.logical_and(jnp.logical_and(ids[0] == grid[0] - 1, ids[1] == grid[1] - 1),
                                     ids[2] == nk - 1))
            def _():
                finish()

    ins, specs = [a, b], [a_spec, b_spec]
    if res is not None:
        ins.append(res)
        specs.append(o_spec)
    est = 2 * (tm * tk * a.dtype.itemsize + tk * tn * b.dtype.itemsize + tm * tn * 4 * (2 if res is not None else 1))
    est += tm * tk * 2 + tk * tn * 2 + tm * tn * 4
    if exchange is None:
        scratch = [pltpu.VMEM((tm, tn), F32)] if (out_dtype != F32 and nk > 1) else []
        return pl.pallas_call(
            body, name=name, grid=grid, in_specs=specs, out_specs=o_spec,
            out_shape=SDS((M, N), out_dtype), scratch_shapes=scratch,
            compiler_params=_cp(("arbitrary", "arbitrary", "arbitrary"), est))(*ins)
    assert out_dtype == F32
    hbm = pl.BlockSpec(memory_space=pltpu.HBM)
    cp = _cp(("arbitrary", "arbitrary", "arbitrary"), est)
    out = pl.pallas_call(
        body, name=name, grid=grid, in_specs=specs + [hbm] * n_ex, out_specs=[o_spec] + [hbm] * n_ex,
        out_shape=[SDS((M, N), F32)] + list(exchange.out_shape),
        scratch_shapes=[pltpu.SemaphoreType.DMA((exchange.n_sems,)), pltpu.SemaphoreType.DMA((exchange.n_sems,)),
                        pltpu.SemaphoreType.DMA((exchange.n_local,))],
        compiler_params=pltpu.CompilerParams(
            dimension_semantics=cp.dimension_semantics, vmem_limit_bytes=cp.vmem_limit_bytes,
            has_side_effects=True))(*ins, *exchange.arrays)
    return out[0], out[1:]


def rms_fwd(x, nw, name):
    L, D = x.shape
    tm = _pick(L, 512, SUBLANES)

    def body(x_ref, w_ref, h_ref):
        xx = x_ref[...]
        r = lax.rsqrt(jnp.mean(xx * xx, axis=-1, keepdims=True) + EPS)
        h_ref[...] = (xx * r * w_ref[...]).astype(BF16)

    return pl.pallas_call(
        body, name=name, grid=(L // tm,),
        in_specs=[pl.BlockSpec((tm, D), lambda i: (i, 0)), pl.BlockSpec((1, D), lambda i: (0, 0))],
        out_specs=pl.BlockSpec((tm, D), lambda i: (i, 0)), out_shape=SDS((L, D), BF16),
        compiler_params=_cp(("arbitrary",), 2 * tm * D * 6))(x, nw)


def rms_bwd(dh, x, nw, dres, name):
    L, D = x.shape
    tm = _pick(L, 256, SUBLANES)

    def body(dh_ref, x_ref, w_ref, dr_ref, dx_ref, dw_ref):
        @pl.when(pl.program_id(0) == 0)
        def _():
            dw_ref[...] = jnp.zeros_like(dw_ref)

        xx, d = x_ref[...], dh_ref[...]
        r = lax.rsqrt(jnp.mean(xx * xx, axis=-1, keepdims=True) + EPS)
        xh = xx * r
        dw_ref[0:1, :] += jnp.sum(d * xh, axis=0, keepdims=True)
        dxh = d * w_ref[...]
        dx_ref[...] = dr_ref[...] + r * (dxh - xh * jnp.mean(dxh * xh, axis=-1, keepdims=True))

    row = pl.BlockSpec((tm, D), lambda i: (i, 0))
    return pl.pallas_call(
        body, name=name, grid=(L // tm,),
        in_specs=[row, row, pl.BlockSpec((1, D), lambda i: (0, 0)), row],
        out_specs=[row, pl.BlockSpec((SUBLANES, D), lambda i: (0, 0))],
        out_shape=[SDS((L, D), F32), SDS((SUBLANES, D), F32)],
        compiler_params=_cp(("arbitrary",), 2 * tm * D * 16))(dh, x, nw, dres)


def loss_head(h, fw, tgt, name):
    L, D = h.shape
    tm = _pick(L, 256, SUBLANES)

    def body(h_ref, w_ref, t_ref, dh_ref, dw_ref, ls_ref):
        @pl.when(pl.program_id(0) == 0)
        def _():
            dw_ref[...] = jnp.zeros_like(dw_ref)
            ls_ref[...] = jnp.zeros_like(ls_ref)

        xx = h_ref[...]
        r = lax.rsqrt(jnp.mean(xx * xx, axis=-1, keepdims=True) + EPS)
        xh = xx * r
        err = xh * w_ref[...] - t_ref[...]
        per_tok = jnp.mean(err * err, axis=-1, keepdims=True)
        ls_ref[...] += jnp.broadcast_to(0.5 * jnp.sum(per_tok, axis=0, keepdims=True), ls_ref.shape)
        dy = err * (1.0 / D)
        dw_ref[0:1, :] += jnp.sum(dy * xh, axis=0, keepdims=True)
        dxh = dy * w_ref[...]
        dh_ref[...] = r * (dxh - xh * jnp.mean(dxh * xh, axis=-1, keepdims=True))

    row = pl.BlockSpec((tm, D), lambda i: (i, 0))
    return pl.pallas_call(
        body, name=name, grid=(L // tm,),
        in_specs=[row, pl.BlockSpec((1, D), lambda i: (0, 0)), row],
        out_specs=[row, pl.BlockSpec((SUBLANES, D), lambda i: (0, 0)),
                   pl.BlockSpec((SUBLANES, LANES), lambda i: (0, 0))],
        out_shape=[SDS((L, D), F32), SDS((SUBLANES, D), F32), SDS((SUBLANES, LANES), F32)],
        compiler_params=_cp(("arbitrary",), 2 * tm * D * 12))(h, fw, tgt)


def _shifted(u, edge, s, back):
    n = u.shape[0]
    row = lax.broadcasted_iota(jnp.int32, (SUBLANES, u.shape[1]), 0)
    if back:
        r = pltpu.roll(u, s, 0)
        head = jnp.where(row < s, pltpu.roll(edge, s, 0), r[0:SUBLANES])
        return jnp.concatenate([head, r[SUBLANES:]], axis=0)
    r = pltpu.roll(u, n - s, 0)
    tail = jnp.where(row >= SUBLANES - s, pltpu.roll(edge, SUBLANES - s, 0), r[n - SUBLANES:])
    return jnp.concatenate([r[:n - SUBLANES], tail], axis=0)


def _conv_pre(u, prev, w, b):
    acc = b + w[SSD_CONV - 1:SSD_CONV] * u
    taps = [u]
    for s in range(1, SSD_CONV):
        us = _shifted(u, prev, s, True)
        taps.append(us)
        acc = acc + w[SSD_CONV - 1 - s:SSD_CONV - s] * us
    return acc, taps


def _conv_specs(dm, tm, tc, col0):
    rb = tm // SUBLANES
    u_spec = pl.BlockSpec((tm, tc), lambda j, i: (i, col0 + j))
    prev_spec = pl.BlockSpec((SUBLANES, tc), lambda j, i: (jnp.maximum(i * rb - 1, 0), col0 + j))
    w_spec = pl.BlockSpec((SSD_CONV, tc), lambda j, i: (0, j))
    b_spec = pl.BlockSpec((1, tc), lambda j, i: (0, j))
    return u_spec, prev_spec, w_spec, b_spec


def conv_fwd(proj, cw, cb, dm, name):
    L, CD = dm.L, dm.CD
    tm, tc = _pick(L, 1024, SUBLANES), _pick(math.gcd(CD, dm.SW), 512, LANES)
    u_spec, prev_spec, w_spec, b_spec = _conv_specs(dm, tm, tc, dm.SW // tc)

    def body(u_ref, p_ref, w_ref, b_ref, o_ref):
        prev = jnp.where(pl.program_id(1) == 0, 0.0, p_ref[...])
        c, _ = _conv_pre(u_ref[...], prev, w_ref[...], b_ref[...])
        o_ref[...] = c * _sigmoid(c)

    return pl.pallas_call(
        body, name=name, grid=(CD // tc, L // tm),
        in_specs=[u_spec, prev_spec, w_spec, b_spec],
        out_specs=pl.BlockSpec((tm, tc), lambda j, i: (i, j)), out_shape=SDS((L, CD), F32),
        compiler_params=_cp(("arbitrary", "arbitrary"), 12 * tm * tc * 4))(proj, proj, cw, cb)


def conv_bwd_pre(proj, dxs, db, dcm, cw, cb, dm, name):
    L, CD = dm.L, dm.CD
    gn = dm.G * SSD_STATE
    tm, tc = _pick(L, 1024, SUBLANES), _pick(math.gcd(gn, dm.SW), 512, LANES)
    u_spec, prev_spec, w_spec, b_spec = _conv_specs(dm, tm, tc, dm.SW // tc)
    nx, nb = dm.SW // tc, gn // tc

    def body(u_ref, p_ref, dx_ref, db_ref, dcm_ref, w_ref, b_ref, dc_ref, g_ref):
        j = pl.program_id(0)

        @pl.when(pl.program_id(1) == 0)
        def _():
            g_ref[...] = jnp.zeros_like(g_ref)

        prev = jnp.where(pl.program_id(1) == 0, 0.0, p_ref[...])
        c, taps = _conv_pre(u_ref[...], prev, w_ref[...], b_ref[...])
        sg = _sigmoid(c)
        d = jnp.where(j < nx, dx_ref[...], jnp.where(j < nx + nb, db_ref[...], dcm_ref[...]))
        dc = d * (sg * (1.0 + c * (1.0 - sg)))
        dc_ref[...] = dc
        for s in range(SSD_CONV):
            g_ref[SSD_CONV - 1 - s:SSD_CONV - s, :] += jnp.sum(dc * taps[s], axis=0, keepdims=True)
        g_ref[SSD_CONV:SSD_CONV + 1, :] += jnp.sum(dc, axis=0, keepdims=True)

    blk = pl.BlockSpec((tm, tc), lambda j, i: (i, j))
    part = lambda lo, n: pl.BlockSpec((tm, tc), lambda j, i: (i, jnp.clip(j - lo, 0, n - 1)))
    return pl.pallas_call(
        body, name=name, grid=(CD // tc, L // tm),
        in_specs=[u_spec, prev_spec, part(0, nx), part(nx, nb), part(nx + nb, nb), w_spec, b_spec],
        out_specs=[blk, pl.BlockSpec((SUBLANES, tc), lambda j, i: (0, j))],
        out_shape=[SDS((L, CD), F32), SDS((SUBLANES, CD), F32)],
        compiler_params=_cp(("arbitrary", "arbitrary"), 20 * tm * tc * 4))(proj, proj, dxs, db, dcm, cw, cb)


def conv_bwd_in(dc, cw, dm, name):
    L, CD = dm.L, dm.CD
    tm, tc = _pick(L, 1024, SUBLANES), _pick(CD, 512, LANES)
    rb, nrow = tm // SUBLANES, L // SUBLANES
    ni = L // tm

    def body(d_ref, n_ref, w_ref, o_ref):
        nxt = jnp.where(pl.program_id(1) == ni - 1, 0.0, n_ref[...])
        dc_, w = d_ref[...], w_ref[...]
        acc = w[SSD_CONV - 1:SSD_CONV] * dc_
        for s in range(1, SSD_CONV):
            acc = acc + w[SSD_CONV - 1 - s:SSD_CONV - s] * _shifted(dc_, nxt, s, False)
        o_ref[...] = acc.astype(BF16)

    blk = pl.BlockSpec((tm, tc), lambda j, i: (i, j))
    return pl.pallas_call(
        body, name=name, grid=(CD // tc, ni),
        in_specs=[blk, pl.BlockSpec((SUBLANES, tc), lambda j, i: (jnp.minimum((i + 1) * rb, nrow - 1), j)),
                  pl.BlockSpec((SSD_CONV, tc), lambda j, i: (0, j))],
        out_specs=blk, out_shape=SDS((L, CD), BF16),
        compiler_params=_cp(("arbitrary", "arbitrary"), 10 * tm * tc * 4))(dc, dc, cw)


def _ssd_chunk(dm):
    return _pick(dm.L, 512, LANES)


def _dt_parts(dtr, par):
    return _softplus(dtr + par[0:1]), -jnp.exp(par[1:2])


def _tri(Q):
    rows = lax.broadcasted_iota(jnp.int32, (Q, Q), 0)
    cols = lax.broadcasted_iota(jnp.int32, (Q, Q), 1)
    return rows, cols


def _lanes_to_group(v, g, R, axis):
    n = v.shape[axis]
    return v if g == 0 else pltpu.roll(v, n - g * R, axis)


def ssd_prep(proj, par_all, dm, name):
    Q, G, R = _ssd_chunk(dm), dm.G, dm.R
    nc = dm.L // Q

    def body(dtr_ref, par_ref, dt_ref, ac_ref, at_ref):
        dt, a = _dt_parts(dtr_ref[...], par_ref[...])
        da = dt * a
        rows, cols = _tri(Q)
        acum = jnp.dot((rows >= cols).astype(F32), da, precision=HIGHEST, preferred_element_type=F32)
        acum_t = lax.dot_general(da, (rows <= cols).astype(F32), (_DOT_DIMS["tn"], ((), ())),
                                 precision=HIGHEST, preferred_element_type=F32)
        for g in range(G):
            dt_ref[g] = _lanes_to_group(dt, g, R, 1)
            ac_ref[g] = _lanes_to_group(acum, g, R, 1)
            at_ref[g, 0] = _lanes_to_group(acum_t, g, R, 0)[0:SUBLANES]

    lane_blk = pl.BlockSpec((G, Q, LANES), lambda c: (0, c, 0))
    return pl.pallas_call(
        body, name=name, grid=(nc,),
        in_specs=[pl.BlockSpec((Q, LANES), lambda c: (c, dm.dt_off // LANES)),
                  pl.BlockSpec((SUBLANES, LANES), lambda c: (0, 0))],
        out_specs=[lane_blk, lane_blk, pl.BlockSpec((G, 1, SUBLANES, Q), lambda c: (0, c, 0, 0))],
        out_shape=[SDS((G, dm.L, LANES), F32), SDS((G, dm.L, LANES), F32), SDS((G, nc, SUBLANES, Q), F32)],
        compiler_params=_cp(("arbitrary",), 8 << 20))(proj, par_all)


def _ssd_common(xa_refs, Q):
    _, b_ref, c_ref = xa_refs
    rows, cols = _tri(Q)
    bm, cm = b_ref[...].astype(BF16), c_ref[...].astype(BF16)
    gm = lax.dot_general(cm, bm, (_DOT_DIMS["nt"], ((), ())), preferred_element_type=F32)
    return rows >= cols, bm, cm, gm


def _ssd_head(r, Q, x, dt, tri, acum, acum_t, gm):
    P = SSD_HEAD_DIM
    col = acum[:, r:r + 1]
    row = acum_t[r:r + 1, :]
    lam = jnp.where(tri, jnp.exp(jnp.minimum(col - row, 0.0)), 0.0)
    m = gm * lam
    xh = x[:, r * P:(r + 1) * P]
    xs = xh * dt[:, r:r + 1]
    a_last = acum_t[r:r + 1, Q - 1:Q]
    return col, lam, m, xh, xs, a_last


def _ssd_specs(dm, Q, rev):
    nc = dm.L // Q
    cc = (lambda c: nc - 1 - c) if rev else (lambda c: c)
    nb = dm.SW // SSD_STATE
    x_spec = pl.BlockSpec((Q, dm.GW), lambda g, c: (cc(c), g))
    b_spec = pl.BlockSpec((Q, SSD_STATE), lambda g, c: (cc(c), nb + g))
    c_spec = pl.BlockSpec((Q, SSD_STATE), lambda g, c: (cc(c), nb + dm.G + g))
    dt_spec = pl.BlockSpec((1, Q, LANES), lambda g, c: (g, cc(c), 0))
    at_spec = pl.BlockSpec((1, 1, SUBLANES, Q), lambda g, c: (g, cc(c), 0, 0))
    par_spec = pl.BlockSpec((1, SUBLANES, LANES), lambda g, c: (g, 0, 0))
    h_spec = pl.BlockSpec((1, 1, dm.GW, SSD_STATE), lambda g, c: (cc(c), g, 0, 0))
    return x_spec, b_spec, c_spec, dt_spec, at_spec, par_spec, h_spec


def ssd_fwd(xa, prep, par, dm, name):
    Q = _ssd_chunk(dm)
    nc, P, R = dm.L // Q, SSD_HEAD_DIM, dm.R
    x_spec, b_spec, c_spec, dt_spec, at_spec, par_spec, h_spec = _ssd_specs(dm, Q, False)

    def body(x_ref, b_ref, c_ref, dt_ref, ac_ref, at_ref, par_ref, y_ref, hp_ref, h_scr):
        @pl.when(pl.program_id(1) == 0)
        def _():
            h_scr[...] = jnp.zeros_like(h_scr)

        tri, bm, cm, gm = _ssd_common((x_ref, b_ref, c_ref), Q)
        par, dt, acum, acum_t = par_ref[0], dt_ref[0], ac_ref[0], at_ref[0, 0]
        x = x_ref[...]
        hp = h_scr[...]
        hp_ref[0, 0] = hp
        for r in range(R):
            col, lam, m, xh, xs, a_last = _ssd_head(r, Q, x, dt, tri, acum, acum_t, gm)
            hpr = hp[r * P:(r + 1) * P, :]
            ydiag = jnp.dot(m.astype(BF16), xs.astype(BF16), preferred_element_type=F32)
            yoff = jnp.exp(col) * lax.dot_general(cm, hpr.astype(BF16), (_DOT_DIMS["nt"], ((), ())),
                                                  preferred_element_type=F32)
            dte = jnp.exp(a_last - col)
            st = lax.dot_general((xs * dte).astype(BF16), bm, (_DOT_DIMS["tn"], ((), ())),
                                 preferred_element_type=F32)
            h_scr[r * P:(r + 1) * P, :] = jnp.exp(a_last) * hpr + st
            y_ref[:, r * P:(r + 1) * P] = ydiag + yoff + par[2:3, r:r + 1] * xh

    return pl.pallas_call(
        body, name=name, grid=(dm.G, nc),
        in_specs=[x_spec, b_spec, c_spec, dt_spec, dt_spec, at_spec, par_spec],
        out_specs=[x_spec, h_spec],
        out_shape=[SDS((dm.L, dm.SW), F32), SDS((nc, dm.G, dm.GW, SSD_STATE), F32)],
        scratch_shapes=[pltpu.VMEM((dm.GW, SSD_STATE), F32)],
        compiler_params=_cp(("arbitrary", "arbitrary"), 24 << 20))(xa, xa, xa, *prep, par)


def ssd_bwd(xa, prep, par, hprev, dyo, dm, name):
    Q = _ssd_chunk(dm)
    nc, P, R, N = dm.L // Q, SSD_HEAD_DIM, dm.R, SSD_STATE
    x_spec, b_spec, c_spec, dt_spec, at_spec, par_spec, h_spec = _ssd_specs(dm, Q, True)
    bc_spec = pl.BlockSpec((Q, N), lambda g, c: (nc - 1 - c, g))
    NT, TN = (_DOT_DIMS["nt"], ((), ())), (_DOT_DIMS["tn"], ((), ()))

    def body(x_ref, b_ref, c_ref, dt_ref, ac_ref, at_ref, par_ref, hp_ref, dy_ref,
             dx_ref, db_ref, dc_ref, dac_ref, span_ref, ddt_ref, gs_ref, dh_scr):
        @pl.when(pl.program_id(1) == 0)
        def _():
            dh_scr[...] = jnp.zeros_like(dh_scr)
            gs_ref[...] = jnp.zeros_like(gs_ref)

        tri, bm, cm, gm = _ssd_common((x_ref, b_ref, c_ref), Q)
        par, dt, acum, acum_t = par_ref[0], dt_ref[0], ac_ref[0], at_ref[0, 0]
        x, dy, hp, dhn = x_ref[...], dy_ref[...].astype(F32), hp_ref[0, 0], dh_scr[...]
        lane = lax.broadcasted_iota(jnp.int32, (Q, LANES), 1)
        rowi = lax.broadcasted_iota(jnp.int32, (Q, LANES), 0)
        d_acum = jnp.zeros((Q, LANES), F32)
        d_dt = jnp.zeros((Q, LANES), F32)
        d_skip = jnp.zeros((1, LANES), F32)
        dgsum = jnp.zeros((Q, Q), F32)
        dye_all, xse_all, span_rows = [], [], []
        rq = lax.broadcasted_iota(jnp.int32, (Q, Q), 0)
        cq = lax.broadcasted_iota(jnp.int32, (Q, Q), 1)
        ue = (rq < cq).astype(BF16)
        for r in range(R):
            col, lam, m, xh, xs, a_last = _ssd_head(r, Q, x, dt, tri, acum, acum_t, gm)
            sl = slice(r * P, (r + 1) * P)
            dyh, hpr, dhr = dy[:, sl], hp[sl, :], dhn[sl, :]
            mb, xsb, dyb = m.astype(BF16), xs.astype(BF16), dyh.astype(BF16)
            e_a = jnp.exp(col)
            dte = jnp.exp(a_last - col)
            yoff = e_a * lax.dot_general(cm, hpr.astype(BF16), NT, preferred_element_type=F32)
            bdh = dte * lax.dot_general(bm, dhr.astype(BF16), NT, preferred_element_type=F32)
            dxs = lax.dot_general(mb, dyb, TN, preferred_element_type=F32) + bdh
            dm_ = lax.dot_general(dyb, xsb, NT, preferred_element_type=F32)
            dgsum = dgsum + dm_ * lam
            dye, xse = (dyh * e_a).astype(BF16), (xs * dte).astype(BF16)
            dye_all.append(dye)
            xse_all.append(xse)
            dh_scr[sl, :] = (jnp.exp(a_last) * dhr
                             + lax.dot_general(dye, cm, TN, preferred_element_type=F32))
            before = jnp.dot((dm_ * m).astype(BF16), ue, preferred_element_type=F32)
            span_rows.append(jnp.sum(jnp.where(tri, before, 0.0), axis=0, keepdims=True))
            da_col = jnp.sum(dyh * yoff - xs * bdh, axis=1, keepdims=True)
            da_last = (jnp.sum(xs * bdh, axis=(0, 1), keepdims=True)
                       + jnp.exp(a_last) * jnp.sum(dhr * hpr, axis=(0, 1), keepdims=True))
            d_acum = d_acum + jnp.where(lane == r, da_col + jnp.where(rowi == Q - 1, da_last, 0.0), 0.0)
            d_dt = d_dt + jnp.where(lane == r, jnp.sum(dxs * xh, axis=1, keepdims=True), 0.0)
            d_skip = d_skip + jnp.where(lane[0:1] == r, jnp.sum(dyh * xh, axis=(0, 1), keepdims=True), 0.0)
            dx_ref[:, sl] = dxs * dt[:, r:r + 1] + par[2:3, r:r + 1] * dyh
        dgb = dgsum.astype(BF16)
        dye_c = jnp.concatenate(dye_all, axis=1)
        xse_c = jnp.concatenate(xse_all, axis=1)
        dc_ref[...] = (jnp.dot(dgb, bm, preferred_element_type=F32)
                       + jnp.dot(dye_c, hp.astype(BF16), preferred_element_type=F32))
        db_ref[...] = (lax.dot_general(dgb, cm, TN, preferred_element_type=F32)
                       + jnp.dot(xse_c, dhn.astype(BF16), preferred_element_type=F32))
        dac_ref[0] = d_acum
        ddt_ref[0] = d_dt
        span_ref[0, 0] = jnp.concatenate(span_rows + [jnp.zeros((SUBLANES - R, Q), F32)] * (R < SUBLANES), axis=0)
        gs_ref[0, 2:3, :] += d_skip

    return pl.pallas_call(
        body, name=name, grid=(dm.G, nc),
        in_specs=[x_spec, b_spec, c_spec, dt_spec, dt_spec, at_spec, par_spec, h_spec, x_spec],
        out_specs=[x_spec, bc_spec, bc_spec, dt_spec, at_spec, dt_spec, par_spec],
        out_shape=[SDS((dm.L, dm.SW), F32), SDS((dm.L, dm.G * N), F32), SDS((dm.L, dm.G * N), F32),
                   SDS((dm.G, dm.L, LANES), F32), SDS((dm.G, nc, SUBLANES, Q), F32),
                   SDS((dm.G, dm.L, LANES), F32), SDS((dm.G, SUBLANES, LANES), F32)],
        scratch_shapes=[pltpu.VMEM((dm.GW, N), F32)],
        compiler_params=_cp(("arbitrary", "arbitrary"), 28 << 20))(xa, xa, xa, *prep, par, hprev, dyo)


def ssd_post(proj, par_all, dac, span, ddt, gs, dm, name):
    Q, G, R = _ssd_chunk(dm), dm.G, dm.R
    nc = dm.L // Q
    NT = (_DOT_DIMS["nt"], ((), ()))

    def body(dtr_ref, par_ref, dac_ref, span_ref, ddt_ref, gs_ref, out_ref, acc_ref):
        @pl.when(pl.program_id(0) == 0)
        def _():
            acc_ref[...] = jnp.zeros_like(acc_ref)

        par = par_ref[...]
        dt, a = _dt_parts(dtr_ref[...], par)
        lane = lax.broadcasted_iota(jnp.int32, (Q, LANES), 1)
        row8 = lax.broadcasted_iota(jnp.int32, (SUBLANES, Q), 0)

        def heads(v, g):
            v = jnp.where(lane[:v.shape[0]] < R, v, 0.0)
            return v if g == 0 else pltpu.roll(v, g * R, 1)

        d_acum = sum(heads(dac_ref[g], g) for g in range(G))
        d_dtx = sum(heads(ddt_ref[g], g) for g in range(G))
        d_skip = sum(heads(gs_ref[g][2:3], g) for g in range(G))
        span_t = jnp.zeros((LANES, Q), F32)
        for g in range(G):
            rows_g = jnp.concatenate([jnp.where(row8 < R, span_ref[g, 0], 0.0),
                                      jnp.zeros((LANES - SUBLANES, Q), F32)], axis=0)
            span_t = span_t + (rows_g if g == 0 else pltpu.roll(rows_g, g * R, 0))
        rq, cq = _tri(Q)
        d_da = (jnp.dot((rq <= cq).astype(F32), d_acum, precision=HIGHEST, preferred_element_type=F32)
                + lax.dot_general((rq == cq).astype(F32), span_t, NT, precision=HIGHEST,
                                  preferred_element_type=F32))
        d_raw = (d_dtx + d_da * a) * _sigmoid(dtr_ref[...] + par[0:1])
        out_ref[...] = d_raw.astype(BF16)
        acc_ref[0:1, :] += jnp.sum(d_raw, axis=0, keepdims=True)
        acc_ref[1:2, :] += jnp.sum(d_da * dt, axis=0, keepdims=True) * a
        acc_ref[2:3, :] = d_skip

    lane_blk = pl.BlockSpec((G, Q, LANES), lambda c: (0, c, 0))
    small = pl.BlockSpec((SUBLANES, LANES), lambda c: (0, 0))
    return pl.pallas_call(
        body, name=name, grid=(nc,),
        in_specs=[pl.BlockSpec((Q, LANES), lambda c: (c, dm.dt_off // LANES)), small, lane_blk,
                  pl.BlockSpec((G, 1, SUBLANES, Q), lambda c: (0, c, 0, 0)), lane_blk,
                  pl.BlockSpec((G, SUBLANES, LANES), lambda c: (0, 0, 0))],
        out_specs=[pl.BlockSpec((Q, LANES), lambda c: (c, 0)), small],
        out_shape=[SDS((dm.L, LANES), BF16), SDS((SUBLANES, LANES), F32)],
        compiler_params=_cp(("arbitrary",), 8 << 20))(proj, par_all, dac, span, ddt, gs)


def _attn_tile(dm):
    return _pick(dm.L, 256, LANES)


def attn_fwd(proj, dm, name):
    L, T, DH, AH = dm.L, _attn_tile(dm), SBA_HEAD_DIM, dm.AH
    nq = L // T
    scale = 1.0 / math.sqrt(DH)
    NT = (_DOT_DIMS["nt"], ((), ()))

    def body(q_ref, k_ref, v_ref, o_ref, tot_ref, nb_ref, ks, vs, o_scr, acc_scr):
        h, i = pl.program_id(0), pl.program_id(1)

        @pl.when(i == 0)
        def _():
            ks[...] = k_ref[...].astype(BF16)
            vs[...] = v_ref[...].astype(BF16)

        qb = q_ref[...].astype(BF16)
        rows = lax.broadcasted_iota(jnp.int32, (T, T), 0)
        cols = lax.broadcasted_iota(jnp.int32, (T, T), 1)
        causal = cols < rows
        u_rev = (rows >= cols).astype(BF16)

        def scores(j, masked):
            sl = pl.ds(pl.multiple_of(j * T, T), T)
            z = lax.dot_general(qb, ks[sl, :], NT, preferred_element_type=F32) * scale
            sp = _softplus(z)
            if masked:
                sp = jnp.where(causal, sp, 0.0)
            cs = jnp.dot(sp.astype(BF16), u_rev, preferred_element_type=F32)
            return sl, z, cs

        def weighted(blk, acc, masked):
            sl, z, cs = blk
            w = jnp.exp(z - cs - acc)
            if masked:
                w = jnp.where(causal, w, 0.0)
            return jnp.dot(w.astype(BF16), vs[sl, :], preferred_element_type=F32), acc + cs[:, 0:1]

        zero = jnp.zeros((T, 1), F32)

        @pl.when(i == 0)
        def _():
            o_scr[...], acc_scr[...] = weighted(scores(i, True), zero, True)

        @pl.when(i > 0)
        def _():
            diag, prev = scores(i, True), scores(i - 1, False)
            pv0, acc1 = weighted(diag, zero, True)
            pv1, acc2 = weighted(prev, acc1, False)
            o_scr[...] = pv0 + pv1
            acc_scr[...] = acc2

        def cond(c):
            return jnp.logical_and(c[0] >= 0, c[1] < SKIP_SUM)

        def loop(c):
            pv, acc = weighted(scores(c[0], False), acc_scr[...], False)
            o_scr[...] += pv
            acc_scr[...] = acc
            return c[0] - 1, jnp.min(acc)

        j_end, _ = lax.while_loop(cond, loop, (jnp.where(i > 0, i - 2, -1), jnp.min(acc_scr[...])))
        o_ref[...] = o_scr[...]
        tot_ref[0] = jnp.broadcast_to(acc_scr[...], (T, LANES))
        nb_ref[h, i] = i - j_end

    kv = lambda off: pl.BlockSpec((L, DH), lambda h, i: (0, off // DH + h))
    est = 2 * 2 * L * DH * 4 + 2 * L * DH * 2 + 12 * T * T * 4
    return pl.pallas_call(
        body, name=name, grid=(AH, nq),
        in_specs=[pl.BlockSpec((T, DH), lambda h, i: (i, dm.q_off // DH + h)), kv(dm.k_off), kv(dm.v_off)],
        out_specs=[pl.BlockSpec((T, DH), lambda h, i: (i, h)),
                   pl.BlockSpec((1, T, LANES), lambda h, i: (h, i, 0)),
                   pl.BlockSpec(memory_space=pltpu.SMEM)],
        out_shape=[SDS((L, dm.AW), F32), SDS((AH, L, LANES), F32), SDS((AH, nq), jnp.int32)],
        scratch_shapes=[pltpu.VMEM((L, DH), BF16), pltpu.VMEM((L, DH), BF16),
                        pltpu.VMEM((T, DH), F32), pltpu.VMEM((T, 1), F32)],
        compiler_params=_cp(("arbitrary", "arbitrary"), est))(proj, proj, proj)


def attn_bwd(proj, dyo, tot, nblk, dm, name):
    L, T, DH, AH = dm.L, _attn_tile(dm), SBA_HEAD_DIM, dm.AH
    nq = L // T
    scale = 1.0 / math.sqrt(DH)
    NT, TN = (_DOT_DIMS["nt"], ((), ())), (_DOT_DIMS["tn"], ((), ()))

    def body(nb_ref, q_ref, k_ref, v_ref, do_ref, tot_ref, dq_ref, dk_out, dv_out,
             ks, vs, dq_scr, p_scr, e_scr, dk_ref, dv_ref):
        h, i = pl.program_id(0), pl.program_id(1)

        @pl.when(i == 0)
        def _():
            ks[...] = k_ref[...].astype(BF16)
            vs[...] = v_ref[...].astype(BF16)
            dk_ref[...] = jnp.zeros_like(dk_ref)
            dv_ref[...] = jnp.zeros_like(dv_ref)

        qb = q_ref[...].astype(BF16)
        dob = do_ref[...].astype(BF16)
        tot_c = tot_ref[0][:, 0:1]
        rows = lax.broadcasted_iota(jnp.int32, (T, T), 0)
        cols = lax.broadcasted_iota(jnp.int32, (T, T), 1)
        causal = cols < rows
        u_fwd = (rows <= cols).astype(BF16)
        dq_scr[...] = jnp.zeros_like(dq_scr)
        p_scr[...] = jnp.zeros_like(p_scr)
        e_scr[...] = jnp.zeros_like(e_scr)

        def scores(j, masked):
            sl = pl.ds(pl.multiple_of(j * T, T), T)
            z = lax.dot_general(qb, ks[sl, :], NT, preferred_element_type=F32) * scale
            sp = _softplus(z)
            if masked:
                sp = jnp.where(causal, sp, 0.0)
            spb = sp.astype(BF16)
            pin = jnp.dot(spb, u_fwd, preferred_element_type=F32)
            dw = lax.dot_general(dob, vs[sl, :], NT, preferred_element_type=F32)
            return sl, z, sp, pin - spb.astype(F32), pin[:, T - 1:T], dw

        def grads(blk, before, e_before, masked):
            sl, z, sp, earlier, block_sum, dw = blk
            cs = (tot_c - before) - earlier
            w = jnp.exp(z - cs)
            if masked:
                w = jnp.where(causal, w, 0.0)
            e = dw * w
            fin = jnp.dot(e.astype(BF16), u_fwd, preferred_element_type=F32)
            dz = (e - jnp.exp(z - sp) * (e_before + fin)) * scale
            if masked:
                dz = jnp.where(causal, dz, 0.0)
            dzb = dz.astype(BF16)
            dq_scr[...] += jnp.dot(dzb, ks[sl, :], preferred_element_type=F32)
            dk_ref[sl, :] += lax.dot_general(dzb, qb, TN, preferred_element_type=F32)
            dv_ref[sl, :] += lax.dot_general(w.astype(BF16), dob, TN, preferred_element_type=F32)
            return before + block_sum, e_before + fin[:, T - 1:T]

        def loop(j, carry):
            p_scr[...], e_scr[...] = grads(scores(j, False), p_scr[...], e_scr[...], False)
            return carry

        lax.fori_loop(i - nb_ref[h, i] + 1, i - 1, loop, 0)

        @pl.when(i == 0)
        def _():
            grads(scores(i, True), p_scr[...], e_scr[...], True)

        @pl.when(i > 0)
        def _():
            prev, diag = scores(i - 1, False), scores(i, True)
            p1, e1 = grads(prev, p_scr[...], e_scr[...], False)
            grads(diag, p1, e1, True)

        dq_ref[...] = dq_scr[...].astype(BF16)

        @pl.when(i == nq - 1)
        def _():
            dk_out[...] = dk_ref[...].astype(BF16)
            dv_out[...] = dv_ref[...].astype(BF16)

    kv = lambda off: pl.BlockSpec((L, DH), lambda h, i, nb: (0, off // DH + h))
    qblk = lambda off: pl.BlockSpec((T, DH), lambda h, i, nb: (i, off // DH + h))
    acc = pl.BlockSpec((L, DH), lambda h, i, nb: (0, h))
    est = 2 * 2 * L * DH * 4 * 2 + 2 * L * DH * 2 + 16 * T * T * 4
    grid_spec = pltpu.PrefetchScalarGridSpec(
        num_scalar_prefetch=1, grid=(AH, nq),
        in_specs=[qblk(dm.q_off), kv(dm.k_off), kv(dm.v_off), qblk(dm.SW),
                  pl.BlockSpec((1, T, LANES), lambda h, i, nb: (h, i, 0))],
        out_specs=[qblk(0), acc, acc],
        scratch_shapes=[pltpu.VMEM((L, DH), BF16), pltpu.VMEM((L, DH), BF16),
                        pltpu.VMEM((T, DH), F32), pltpu.VMEM((T, 1), F32), pltpu.VMEM((T, 1), F32),
                        pltpu.VMEM((L, DH), F32), pltpu.VMEM((L, DH), F32)])
    return pl.pallas_call(
        body, name=name, grid_spec=grid_spec,
        out_shape=[SDS((L, dm.AW), BF16)] * 3,
        compiler_params=_cp(("arbitrary", "arbitrary"), est))(nblk, proj, proj, proj, dyo, tot)


def _gate_specs(dm, tm, order):
    GW, G = dm.GW, dm.G
    ix = (lambda a, b: (a, b)) if order == "ij" else (lambda a, b: (b, a))

    def spec(colfn):
        return pl.BlockSpec((tm, GW), lambda p0, p1: (ix(p0, p1)[0], colfn(ix(p0, p1)[1])))

    y_spec = spec(lambda j: jnp.minimum(j, G - 1))
    o_spec = spec(lambda j: jnp.maximum(j - G, 0))
    zg_spec = spec(lambda j: jnp.where(j < G, j, dm.g_off // GW + j - G))
    w_spec = pl.BlockSpec((1, GW), lambda p0, p1: (0, jnp.minimum(ix(p0, p1)[1], G - 1)))
    full = spec(lambda j: j)
    return y_spec, o_spec, zg_spec, w_spec, full


def gate_fwd(y, o, proj, snw, dm, name):
    L, GW, G = dm.L, dm.GW, dm.G
    tm = _pick(L, 1024, SUBLANES)
    ncol = (dm.SW + dm.AW) // GW
    y_spec, o_spec, zg_spec, w_spec, full = _gate_specs(dm, tm, "ij")

    def body(y_ref, o_ref, zg_ref, w_ref, m_ref):
        j = pl.program_id(1)
        zg = zg_ref[...]
        gate = zg * _sigmoid(zg)

        @pl.when(j < G)
        def _():
            yz = y_ref[...] * gate
            r = lax.rsqrt(jnp.mean(yz * yz, axis=-1, keepdims=True) + EPS)
            m_ref[...] = (yz * r * w_ref[...]).astype(BF16)

        @pl.when(j >= G)
        def _():
            m_ref[...] = (o_ref[...] * gate).astype(BF16)

    return pl.pallas_call(
        body, name=name, grid=(L // tm, ncol),
        in_specs=[y_spec, o_spec, zg_spec, w_spec], out_specs=full,
        out_shape=SDS((L, dm.SW + dm.AW), BF16),
        compiler_params=_cp(("arbitrary", "arbitrary"), 2 * tm * GW * 16))(y, o, proj, snw)


def gate_bwd(dmix, y, o, proj, snw, dm, name):
    L, GW, G = dm.L, dm.GW, dm.G
    tm = _pick(L, 1024, SUBLANES)
    W = dm.SW + dm.AW
    y_spec, o_spec, zg_spec, w_spec, full = _gate_specs(dm, tm, "ji")

    def body(d_ref, y_ref, o_ref, zg_ref, w_ref, dyo_ref, dzg_ref, dw_ref):
        j = pl.program_id(0)

        @pl.when(pl.program_id(1) == 0)
        def _():
            dw_ref[...] = jnp.zeros_like(dw_ref)

        zg, d = zg_ref[...], d_ref[...].astype(F32)
        sg = _sigmoid(zg)
        gate = zg * sg
        dgate = sg * (1.0 + zg * (1.0 - sg))

        @pl.when(j < G)
        def _():
            yv = y_ref[...]
            yz = yv * gate
            r = lax.rsqrt(jnp.mean(yz * yz, axis=-1, keepdims=True) + EPS)
            nrm = yz * r
            dw_ref[0:1, :] += jnp.sum(d * nrm, axis=0, keepdims=True)
            dn = d * w_ref[...]
            dyz = r * (dn - nrm * jnp.mean(dn * nrm, axis=-1, keepdims=True))
            dyo_ref[...] = (dyz * gate).astype(BF16)
            dzg_ref[...] = (dyz * yv * dgate).astype(BF16)

        @pl.when(j >= G)
        def _():
            dyo_ref[...] = (d * gate).astype(BF16)
            dzg_ref[...] = (d * o_ref[...] * dgate).astype(BF16)

    return pl.pallas_call(
        body, name=name, grid=(W // GW, L // tm),
        in_specs=[full, y_spec, o_spec, zg_spec, w_spec],
        out_specs=[full, full, pl.BlockSpec((SUBLANES, GW), lambda j, i: (0, j))],
        out_shape=[SDS((L, W), BF16), SDS((L, W), BF16), SDS((SUBLANES, W), F32)],
        compiler_params=_cp(("arbitrary", "arbitrary"), 2 * tm * GW * 24))(dmix, y, o, proj, snw)


def adamw(parts, w, m, v, name):
    R, C = w.shape
    n_slot = parts.shape[0]
    tr = _pick(R, max(SUBLANES, (1 << 18) // C // SUBLANES * SUBLANES), SUBLANES)
    c1, c2 = 1.0 - ADAM_B1 ** ADAM_STEP, 1.0 - ADAM_B2 ** ADAM_STEP

    def body(p_ref, w_ref, m_ref, v_ref, g_ref, d_ref, m2_ref, v2_ref):
        g = p_ref[0].astype(F32)
        for s in range(1, n_slot):
            g = g + p_ref[s].astype(F32)
        m2 = ADAM_B1 * m_ref[...] + (1.0 - ADAM_B1) * g
        v2 = ADAM_B2 * v_ref[...] + (1.0 - ADAM_B2) * (g * g)
        g_ref[...] = g
        m2_ref[...] = m2
        v2_ref[...] = v2
        d_ref[...] = -ADAM_LR * ((m2 / c1) / (jnp.sqrt(v2 / c2) + ADAM_EPS) + ADAM_WD * w_ref[...])

    blk = pl.BlockSpec((tr, C), lambda i: (i, 0))
    return pl.pallas_call(
        body, name=name, grid=(R // tr,),
        in_specs=[pl.BlockSpec((n_slot, tr, C), lambda i: (0, i, 0)), blk, blk, blk],
        out_specs=[blk] * 4, out_shape=[SDS((R, C), F32)] * 4,
        compiler_params=_cp(("arbitrary",), 2 * tr * C * (n_slot * 4 + 28)))(parts, w, m, v)


N_CHIP = 4


def _place():
    x, y, c = lax.axis_index("x"), lax.axis_index("y"), lax.axis_index("c")
    return x, y, c, [(1 - x, y), (x, 1 - y), (1 - x, 1 - y)]


def _comm_call(body, arrays, out_shape, n_sems, n_local, name):
    hbm = pl.BlockSpec(memory_space=pltpu.HBM)
    return pl.pallas_call(
        body, name=name, in_specs=[hbm] * len(arrays), out_specs=[hbm] * len(out_shape), out_shape=out_shape,
        scratch_shapes=[pltpu.SemaphoreType.DMA((n_sems,)), pltpu.SemaphoreType.DMA((n_sems,)),
                        pltpu.SemaphoreType.DMA((n_local,))],
        compiler_params=pltpu.CompilerParams(has_side_effects=True))(*arrays)


def gather_weights(arrays, name):
    n, per = len(arrays), N_DEV - 1

    def body(*refs):
        srcs, dsts = refs[:n], refs[n:2 * n]
        send_sems, recv_sems, local_sems = refs[2 * n:]
        start, finish = _gather_halves(srcs, dsts, send_sems, recv_sems, local_sems)
        start()
        finish()

    out_shape = [SDS((N_DEV,) + a.shape, a.dtype) for a in arrays]
    return _comm_call(body, arrays, out_shape, n * per, n, name)


def _gather_halves(srcs, dsts, send_sems, recv_sems, local_sems):
    n, per = len(srcs), N_DEV - 1

    def parts():
        x, y, c, chips = _place()
        me, sib = 4 * x + 2 * y + c, (x, y, 1 - c)

        def cp(a, k, block, to, src=None):
            return pltpu.make_async_remote_copy(
                src_ref=dsts[a].at[block] if src is None else src, dst_ref=dsts[a].at[block],
                send_sem=send_sems.at[a * per + k], recv_sem=recv_sems.at[a * per + k],
                device_id=to, device_id_type=MESH)

        own = [pltpu.make_async_copy(srcs[a], dsts[a].at[me], local_sems.at[a]) for a in range(n)]
        first = []
        for a in range(n):
            first.append(cp(a, 0, me, sib, src=srcs[a]))
            first += [cp(a, 1 + j, me, (px, py, c), src=srcs[a]) for j, (px, py) in enumerate(chips)]
        return x, y, c, chips, sib, cp, own, first

    def start():
        *_, own, first = parts()
        for o in own:
            o.start()
        for f in first:
            f.start()

    def finish():
        x, y, c, chips, sib, cp, own, first = parts()
        passed = []
        for j, (px, py) in enumerate(chips):
            block = 4 * px + 2 * py + c
            for a in range(n):
                cp(a, 1 + j, block, sib).wait_recv()
                fwd = cp(a, 4 + j, block, sib)
                fwd.start()
                passed.append(fwd)
        for a in range(n):
            cp(a, 0, 4 * x + 2 * y + 1 - c, sib).wait_recv()
            for j, (px, py) in enumerate(chips):
                cp(a, 4 + j, 4 * px + 2 * py + 1 - c, sib).wait_recv()
        for f in first + passed:
            f.wait_send()
        for o in own:
            o.wait()

    return start, finish


def gather_beside(shards):
    shards = list(shards)
    return Exchange(arrays=shards, out_shape=[SDS((N_DEV,) + s.shape, s.dtype) for s in shards],
                    n_sems=len(shards) * (N_DEV - 1), n_local=len(shards), halves=_gather_halves)


def pair_exchange(arrays, name):
    n = len(arrays)

    def body(*refs):
        srcs, dsts = refs[:n], refs[n:2 * n]
        send_sems, recv_sems, _ = refs[2 * n:]
        x, y, c, _chips = _place()
        sib = (x, y, 1 - c)

        def cp(a, k):
            return pltpu.make_async_remote_copy(
                src_ref=srcs[a].at[2 * k + 1 - c], dst_ref=dsts[a].at[k],
                send_sem=send_sems.at[a * N_CHIP + k], recv_sem=recv_sems.at[a * N_CHIP + k],
                device_id=sib, device_id_type=MESH)

        cps = [cp(a, k) for k in range(N_CHIP) for a in range(n)]
        for p in cps:
            p.start()
        for p in cps:
            p.wait_recv()
        for p in cps:
            p.wait_send()

    out_shape = [SDS((N_CHIP,) + a.shape[1:], a.dtype) for a in arrays]
    return _comm_call(body, arrays, out_shape, n * N_CHIP, 1, name)


def pair_add(parts, got, name):
    _, R, C = parts.shape
    tr = _pick(R, max(16, (1 << 19) // C // 16 * 16), 16)
    core = lax.axis_index("c").astype(jnp.int32).reshape(1)

    def body(c_ref, p_ref, g_ref, o_ref):
        o_ref[...] = (p_ref[...].astype(F32) + g_ref[...].astype(F32)).astype(o_ref.dtype)

    grid_spec = pltpu.PrefetchScalarGridSpec(
        num_scalar_prefetch=1, grid=(N_CHIP, R // tr),
        in_specs=[pl.BlockSpec((1, tr, C), lambda k, i, c_ref: (2 * k + c_ref[0], i, 0)),
                  pl.BlockSpec((1, tr, C), lambda k, i, c_ref: (k, i, 0))],
        out_specs=pl.BlockSpec((1, tr, C), lambda k, i, c_ref: (k, i, 0)))
    return pl.pallas_call(
        body, name=name, grid_spec=grid_spec, out_shape=SDS((N_CHIP, R, C), parts.dtype),
        compiler_params=_cp(("arbitrary", "arbitrary"), 2 * 3 * tr * C * 2 + 3 * tr * C * 4))(core, parts, got)


def chip_exchange_spec(sums, full=(), split=()):
    ns, nf, nsp = len(sums), len(full), len(split)
    n_sem = 3 * ns + (N_DEV - 1) * (nf + nsp)

    def copies(srcs, dsts, send_sems, recv_sems, local_sems):
        x, y, c, chips = _place()
        me, my_chip = 4 * x + 2 * y + c, 2 * x + y
        started, arrivals, own = [], [], []
        for a in range(ns):
            own.append(pltpu.make_async_copy(srcs[a].at[my_chip], dsts[a].at[my_chip], local_sems.at[a]))
            for j, (px, py) in enumerate(chips):
                k = 2 * px + py
                sem = 3 * a + j
                started.append(pltpu.make_async_remote_copy(
                    src_ref=srcs[a].at[k], dst_ref=dsts[a].at[my_chip],
                    send_sem=send_sems.at[sem], recv_sem=recv_sems.at[sem],
                    device_id=(px, py, c), device_id_type=MESH))
                arrivals.append(dict(
                    src_ref=srcs[a].at[my_chip], dst_ref=dsts[a].at[k],
                    send_sem=send_sems.at[sem], recv_sem=recv_sems.at[sem],
                    device_id=(px, py, c), device_id_type=MESH))
        for b in range(nf + nsp):
            a = ns + b
            is_split = b >= nf
            own.append(pltpu.make_async_copy(srcs[a].at[me] if is_split else srcs[a], dsts[a].at[me],
                                             local_sems.at[a]))
            for rel in range(1, N_DEV):
                px, py, pc = x ^ ((rel >> 2) & 1), y ^ ((rel >> 1) & 1), c ^ (rel & 1)
                pidx = 4 * px + 2 * py + pc
                sem = 3 * ns + b * (N_DEV - 1) + rel - 1
                started.append(pltpu.make_async_remote_copy(
                    src_ref=srcs[a].at[pidx] if is_split else srcs[a], dst_ref=dsts[a].at[me],
                    send_sem=send_sems.at[sem], recv_sem=recv_sems.at[sem],
                    device_id=(px, py, pc), device_id_type=MESH))
                arrivals.append(dict(
                    src_ref=srcs[a].at[me] if is_split else srcs[a], dst_ref=dsts[a].at[pidx],
                    send_sem=send_sems.at[sem], recv_sem=recv_sems.at[sem],
                    device_id=(px, py, pc), device_id_type=MESH))
        return own, started, arrivals

    def halves(*refs):
        def start():
            own, started, _ = copies(*refs)
            for o in own:
                o.start()
            for s in started:
                s.start()

        def finish():
            own, started, arrivals = copies(*refs)
            for r in arrivals:
                pltpu.make_async_remote_copy(**r).wait_recv()
            for s in started:
                s.wait_send()
            for o in own:
                o.wait()

        return start, finish

    out_shape = ([SDS(a.shape, a.dtype) for a in sums] + [SDS((N_DEV,) + a.shape, a.dtype) for a in full]
                 + [SDS(a.shape, a.dtype) for a in split])
    return Exchange(arrays=list(sums) + list(full) + list(split), out_shape=out_shape, n_sems=n_sem,
                    n_local=ns + nf + nsp, halves=halves)


def chip_exchange(sums, full, split, name):
    ex = chip_exchange_spec(sums, full, split)
    n = len(ex.arrays)

    def body(*refs):
        start, finish = ex.halves(refs[:n], refs[n:2 * n], *refs[2 * n:])
        start()
        finish()

    return _comm_call(body, ex.arrays, ex.out_shape, ex.n_sems, ex.n_local, name)


class LayerParams(NamedTuple):
    nw: jax.Array
    w_in: jax.Array
    cw: jax.Array
    cb: jax.Array
    par: jax.Array
    par_all: jax.Array
    snw: jax.Array
    w_out: jax.Array


def head_params(dt_bias, a_log, d_skip, dm):
    rows = jnp.stack([dt_bias, a_log, d_skip])
    par_all = jnp.pad(rows, ((0, SUBLANES - 3), (0, LANES - dm.NH)))
    par = jnp.pad(rows.reshape(3, dm.G, dm.R).transpose(1, 0, 2), ((0, 0), (0, SUBLANES - 3), (0, LANES - dm.R)))
    return par, par_all


def layer_fwd(x, p, dm, tag, next_shards=None):
    h = rms_fwd(x, p.nw, f"rms_fwd{tag}")
    gathered = None
    if next_shards is None:
        proj = mm(h, p.w_in, "nn", tm=512, tn=1920, tk=dm.D, name=f"in_proj{tag}", b_outer=True)
    else:
        proj, gathered = mm(h, p.w_in, "nn", tm=512, tn=1920, tk=dm.D, name=f"in_proj_gather{tag}", b_outer=True,
                            exchange=gather_beside(next_shards))
    xa = conv_fwd(proj, p.cw, p.cb, dm, f"conv_fwd{tag}")
    prep = ssd_prep(proj, p.par_all, dm, f"ssd_prep{tag}")
    y, hprev = ssd_fwd(xa, prep, p.par, dm, f"ssd_fwd{tag}")
    o, tot, nblk = attn_fwd(proj, dm, f"attn_fwd{tag}")
    mix = gate_fwd(y, o, proj, p.snw, dm, f"gate_fwd{tag}")
    xn = mm(mix, p.w_out, "nn", tm=512, tn=1024, tk=dm.SW + dm.AW, name=f"out_proj{tag}", res=x)
    return xn, (x, h, proj, xa, prep, y, hprev, o, tot, nblk, mix), gathered


def layer_bwd(dxn, saved, p, dm, tag, exchange=None):
    x, h, proj, xa, prep, y, hprev, o, tot, nblk, mix = saved
    dmix = mm(dxn, p.w_out, "nt", tm=512, tn=1024, tk=dm.D, name=f"d_mix{tag}", out_dtype=BF16)
    dw_out = mm(mix, dxn, "tn", tm=1024, tn=dm.D, tk=512, name=f"dw_out{tag}", out_dtype=BF16)
    dyo, dzg, dsnw = gate_bwd(dmix, y, o, proj, p.snw, dm, f"gate_bwd{tag}")
    dq, dk, dv = attn_bwd(proj, dyo, tot, nblk, dm, f"attn_bwd{tag}")
    dxs, db, dc, dac, span, ddtx, gsk = ssd_bwd(xa, prep, p.par, hprev, dyo, dm, f"ssd_bwd{tag}")
    ddt_blk, ghead = ssd_post(proj, p.par_all, dac, span, ddtx, gsk, dm, f"ssd_post{tag}")
    dcv, gconv = conv_bwd_pre(proj, dxs, db, dc, p.cw, p.cb, dm, f"conv_bwd_pre{tag}")
    dxbc = conv_bwd_in(dcv, p.cw, dm, f"conv_bwd_in{tag}")
    dproj = jnp.concatenate([dzg[:, :dm.SW], dxbc, dq, dk, dv, dzg[:, dm.SW:], ddt_blk], axis=1)
    brought = None
    if exchange is None:
        dh = mm(dproj, p.w_in, "nt", tm=512, tn=dm.D, tk=1920, name=f"d_h{tag}")
    else:
        dh, brought = mm(dproj, p.w_in, "nt", tm=512, tn=dm.D, tk=1920, name=f"d_h_exchange{tag}", exchange=exchange)
    dw_in = mm(h, dproj, "tn", tm=dm.D, tn=960, tk=2048, name=f"dw_in{tag}", out_dtype=BF16)
    dx, dnw = rms_bwd(dh, x, p.nw, dxn, f"rms_bwd{tag}")
    small = dict(norm_w=dnw[0], conv_w=gconv[:SSD_CONV], conv_b=gconv[SSD_CONV],
                 dt_bias=ghead[0, :dm.NH], a_log=ghead[1, :dm.NH], d_skip=ghead[2, :dm.NH],
                 ssd_norm_w=dsnw[0, :dm.SW])
    return dx, dw_in, dw_out, small, brought


SMALL = ("norm_w", "conv_b", "dt_bias", "a_log", "d_skip", "ssd_norm_w")


def _to_mine(w, dm):
    a, b = dm.SW + dm.CD, dm.SW + dm.CD + dm.NH
    pad = jnp.zeros((w.shape[0], LANES - dm.NH), w.dtype)
    return jnp.concatenate([w[:, :a], w[:, b:], w[:, a:b], pad], axis=1)


def _from_mine(w, dm):
    a = dm.SW + dm.CD
    return jnp.concatenate([w[:, :a], w[:, dm.dt_off:dm.dt_off + dm.NH], w[:, a:dm.dt_off]], axis=1)


def weights_to_mine(g_in, dm, name):
    _, D, ncol = g_in.shape
    tm = _pick(D, 256, 16)

    def body(g_ref, o_ref):
        full = jnp.concatenate([g_ref[j] for j in range(N_DEV)], axis=1)
        o_ref[...] = _to_mine(full, dm)

    return pl.pallas_call(
        body, name=name, grid=(D // tm,),
        in_specs=[pl.BlockSpec((N_DEV, tm, ncol), lambda i: (0, i, 0))],
        out_specs=pl.BlockSpec((tm, dm.NP), lambda i: (i, 0)), out_shape=SDS((D, dm.NP), g_in.dtype),
        compiler_params=_cp(("arbitrary",), 6 * tm * dm.NP * 2))(g_in)


def grads_from_mine(gw, dm, ncol, name):
    D = gw.shape[0]
    tm = _pick(D, 256, 16)

    def body(g_ref, o_ref):
        full = _from_mine(g_ref[...], dm)
        for j in range(N_DEV):
            o_ref[j] = full[:, j * ncol:(j + 1) * ncol]

    return pl.pallas_call(
        body, name=name, grid=(D // tm,),
        in_specs=[pl.BlockSpec((tm, dm.NP), lambda i: (i, 0))],
        out_specs=pl.BlockSpec((N_DEV, tm, ncol), lambda i: (0, i, 0)), out_shape=SDS((N_DEV, D, ncol), gw.dtype),
        compiler_params=_cp(("arbitrary",), 6 * tm * dm.NP * 2))(gw)


def _pack(pieces):
    flat = jnp.concatenate([p.reshape(-1) for p in pieces])
    rows = -(-flat.shape[0] // LANES)
    rows = -(-rows // SUBLANES) * SUBLANES
    return jnp.pad(flat, (0, rows * LANES - flat.shape[0])).reshape(rows, LANES)


def _unpack(buf, shapes):
    flat, out, at = buf.reshape(-1), [], 0
    for s in shapes:
        n = math.prod(s)
        out.append(flat[at:at + n].reshape(s))
        at += n
    return out


def kernel(x, norm_w, w_in, conv_w, conv_b, dt_bias, a_log, d_skip, ssd_norm_w, w_out, final_norm_w, loss_target, m_norm_w, m_w_in, m_conv_w, m_conv_b, m_dt_bias, m_a_log, m_d_skip, m_ssd_norm_w, m_w_out, m_final_norm_w, v_norm_w, v_w_in, v_conv_w, v_conv_b, v_dt_bias, v_a_log, v_d_skip, v_ssd_norm_w, v_w_out, v_final_norm_w):
    depth, D = norm_w.shape
    L = x.shape[1]
    NH = dt_bias.shape[1]
    SW = NH * SSD_HEAD_DIM
    CD = conv_b.shape[1]
    dm = Dims(L=L, D=D, SW=SW, G=(CD - SW) // (2 * SSD_STATE), AW=w_out.shape[1] * N_DEV - SW)
    ncol, csh, osh = w_in.shape[2], conv_w.shape[2], w_out.shape[1]
    me = 4 * lax.axis_index("x") + 2 * lax.axis_index("y") + lax.axis_index("c")

    shards = [(w_in[l].astype(BF16), w_out[l].astype(BF16), conv_w[l]) for l in range(depth)]

    def layer_params(l, gathered):
        g_in, g_out, g_cw = gathered
        full_cw = g_cw.transpose(1, 0, 2).reshape(SSD_CONV, CD)
        par, par_all = head_params(dt_bias[l], a_log[l], d_skip[l], dm)
        return LayerParams(
            nw=norm_w[l][None], w_in=weights_to_mine(g_in, dm, "weights_to_mine"), cw=full_cw, cb=conv_b[l][None],
            par=par, par_all=par_all,
            snw=ssd_norm_w[l][None], w_out=g_out.reshape(N_DEV * osh, D))

    h = x[0]
    params, saved = [], []
    gathered = gather_weights(list(shards[0]), "gather_weights")
    for l in range(depth):
        params.append(layer_params(l, gathered))
        h, s, gathered = layer_fwd(h, params[l], dm, "", shards[l + 1] if l + 1 < depth else None)
        saved.append(s)
    dh, dfw, ls = loss_head(h, final_norm_w[None], loss_target[0], "loss_head")
    loss = lax.psum(ls[0, 0], ("x", "y", "c"))
    smalls, r_in, r_out = [None] * depth, [None] * depth, [None] * depth
    pending = None
    for l in reversed(range(depth)):
        ex = None if pending is None else chip_exchange_spec(pending)
        dh, gw_in, gw_out, smalls[l], brought = layer_bwd(dh, saved[l], params[l], dm, "", ex)
        if brought is not None:
            r_in[l + 1], r_out[l + 1] = brought
        p_in = grads_from_mine(gw_in, dm, ncol, "grads_from_mine")
        p_out = gw_out.astype(BF16).reshape(N_DEV, osh, D)
        s_in, s_out = pair_exchange([p_in, p_out], "pair_exchange")
        pending = [pair_add(p_in, s_in, "pair_add_w_in"), pair_add(p_out, s_out, "pair_add_w_out")]
    grad_x = dh[None]
    rep = [jnp.stack([s[k] for s in smalls]) for k in SMALL] + [dfw[0]]
    rep_shapes = [r.shape for r in rep]
    p_rep = _pack(rep)
    p_cw = jnp.stack([s["conv_w"] for s in smalls]).reshape(depth * SSD_CONV, N_DEV, csh).transpose(1, 0, 2)
    r_in[0], r_out[0], r_rep, r_cw = chip_exchange(pending, [p_rep], [p_cw], "chip_exchange")
    r_in, r_out = jnp.concatenate(r_in, axis=1), jnp.concatenate(r_out, axis=1)

    out_in = adamw(r_in, w_in.reshape(depth * D, ncol), m_w_in.reshape(depth * D, ncol),
                   v_w_in.reshape(depth * D, ncol), "adamw_w_in")
    out_out = adamw(r_out, w_out.reshape(depth * osh, D), m_w_out.reshape(depth * osh, D),
                    v_w_out.reshape(depth * osh, D), "adamw_w_out")
    out_cw = adamw(r_cw, conv_w.reshape(depth * SSD_CONV, csh), m_conv_w.reshape(depth * SSD_CONV, csh),
                   v_conv_w.reshape(depth * SSD_CONV, csh), "adamw_conv_w")
    rep_w = [norm_w, conv_b, dt_bias, a_log, d_skip, ssd_norm_w, final_norm_w]
    rep_m = [m_norm_w, m_conv_b, m_dt_bias, m_a_log, m_d_skip, m_ssd_norm_w, m_final_norm_w]
    rep_v = [v_norm_w, v_conv_b, v_dt_bias, v_a_log, v_d_skip, v_ssd_norm_w, v_final_norm_w]
    out_rep = adamw(r_rep, _pack(rep_w), _pack(rep_m), _pack(rep_v), "adamw_replicated")

    outs = {}
    for kind, i in (("grad", 0), ("delta", 1), ("new_m", 2), ("new_v", 3)):
        r = dict(zip(SMALL + ("final_norm_w",), _unpack(out_rep[i], rep_shapes)))
        r["w_in"] = out_in[i].reshape(w_in.shape)
        r["w_out"] = out_out[i].reshape(w_out.shape)
        r["conv_w"] = out_cw[i].reshape(conv_w.shape)
        outs[kind] = r
    order = ("norm_w", "w_in", "conv_w", "conv_b", "dt_bias", "a_log", "d_skip", "ssd_norm_w", "w_out", "final_norm_w")
    return (loss, grad_x, *[outs[k][n] for k in ("grad", "delta", "new_m", "new_v") for n in order])
```

```python
import functools
import math
from typing import NamedTuple

import jax
import jax.numpy as jnp
from jax import lax
from jax.experimental import pallas as pl
from jax.experimental.pallas import tpu as pltpu

F32, BF16 = jnp.float32, jnp.bfloat16
SDS = jax.ShapeDtypeStruct
EPS = 1e-6
LANES = 128
SUBLANES = 8
VMEM_BYTES = 64 * 2 ** 20
N_DEV = 8
SSD_HEAD_DIM = 64
SSD_STATE = 128
SSD_CONV = 4
SBA_HEAD_DIM = 128
ADAM_LR, ADAM_B1, ADAM_B2, ADAM_EPS, ADAM_WD, ADAM_STEP = 0.001, 0.9, 0.999, 1e-08, 0.01, 10
SKIP_SUM = 110.0
HIGHEST = lax.Precision.HIGHEST
MESH = pl.DeviceIdType.MESH


class Dims(NamedTuple):
    L: int
    D: int
    SW: int
    G: int
    AW: int

    @property
    def NH(self): return self.SW // SSD_HEAD_DIM
    @property
    def R(self): return self.NH // self.G
    @property
    def GW(self): return self.SW // self.G
    @property
    def CD(self): return self.SW + 2 * self.G * SSD_STATE
    @property
    def AH(self): return self.AW // SBA_HEAD_DIM
    @property
    def q_off(self): return self.SW + self.CD
    @property
    def k_off(self): return self.q_off + self.AW
    @property
    def v_off(self): return self.q_off + 2 * self.AW
    @property
    def g_off(self): return self.q_off + 3 * self.AW
    @property
    def dt_off(self): return self.q_off + 4 * self.AW
    @property
    def NP(self): return self.dt_off + LANES


def _pick(n, target, mult):
    t = (min(target, n) // mult) * mult
    while t >= mult:
        if n % t == 0:
            return t
        t -= mult
    return n


def _cp(sem, vmem_est):
    limit = int(min(max(vmem_est * 5 // 4 + (4 << 20), 32 << 20), VMEM_BYTES - (8 << 20)))
    return pltpu.CompilerParams(dimension_semantics=sem, vmem_limit_bytes=limit)


def _sigmoid(x):
    return 1.0 / (1.0 + jnp.exp(-x))


def _softplus(x):
    return jnp.maximum(x, 0.0) + jnp.log(1.0 + jnp.exp(-jnp.abs(x)))


def _nbytes(shape, dtype):
    return math.prod(shape) * jnp.dtype(dtype).itemsize


_DOT_DIMS = {"nn": ((1,), (0,)), "nt": ((1,), (1,)), "tn": ((0,), (0,))}


class Exchange(NamedTuple):
    arrays: list
    out_shape: list
    n_sems: int
    n_local: int
    halves: object


def mm(a, b, mode, *, tm, tn, tk, name, res=None, b_outer=False, exchange=None, out_dtype=F32):
    if mode == "nn":
        (M, K), N = a.shape, b.shape[1]
    elif mode == "nt":
        (M, K), N = a.shape, b.shape[0]
    else:
        (K, M), N = a.shape, b.shape[1]
    tm, tn, tk = _pick(M, tm, LANES), _pick(N, tn, LANES), _pick(K, tk, LANES)
    nk = K // tk

    def ij(p0, p1):
        return (p1, p0) if b_outer else (p0, p1)

    if mode == "tn":
        a_spec = pl.BlockSpec((tk, tm), lambda p0, p1, k: (k, ij(p0, p1)[0]))
    else:
        a_spec = pl.BlockSpec((tm, tk), lambda p0, p1, k: (ij(p0, p1)[0], k))
    if mode == "nt":
        b_spec = pl.BlockSpec((tn, tk), lambda p0, p1, k: (ij(p0, p1)[1], k))
    else:
        b_spec = pl.BlockSpec((tk, tn), lambda p0, p1, k: (k, ij(p0, p1)[1]))
    o_spec = pl.BlockSpec((tm, tn), lambda p0, p1, k: ij(p0, p1))
    dims = (_DOT_DIMS[mode], ((), ()))

    grid = (N // tn, M // tm, nk) if b_outer else (M // tm, N // tn, nk)
    n_res = 0 if res is None else 1
    n_ex = 0 if exchange is None else len(exchange.arrays)

    def body(*refs):
        a_ref, b_ref, o_ref = refs[0], refs[1], refs[2 + n_res + n_ex]
        ids = [pl.program_id(d) for d in range(3)]
        if exchange is not None:
            srcs = refs[2 + n_res:2 + n_res + n_ex]
            dsts = refs[3 + n_res + n_ex:3 + n_res + 2 * n_ex]
            start, finish = exchange.halves(srcs, dsts, *refs[3 + n_res + 2 * n_ex:])

            @pl.when(jnp.logical_and(jnp.logical_and(ids[0] == 0, ids[1] == 0), ids[2] == 0))
            def _():
                start()

        part = lax.dot_general(a_ref[...].astype(BF16), b_ref[...].astype(BF16), dims,
                               preferred_element_type=F32)
        if res is not None:
            first = part + refs[2][...]
        else:
            first = part
        if nk == 1:
            o_ref[...] = first.astype(out_dtype)
        else:
            acc_ref = o_ref if out_dtype == F32 else refs[-1]

            @pl.when(ids[2] == 0)
            def _():
                acc_ref[...] = first

            @pl.when(ids[2] > 0)
            def _():
                acc_ref[...] += part

            if out_dtype != F32:
                @pl.when(ids[2] == nk - 1)
                def _():
                    o_ref[...] = acc_ref[...].astype(out_dtype)

        if exchange is not None:
            @pl.when(jnp.logical_and(jnp.logical_and(ids[0] == grid[0] - 1, ids[1] == grid[1] - 1),
                                     ids[2] == nk - 1))
            def _():
                finish()

    ins, specs = [a, b], [a_spec, b_spec]
    if res is not None:
        ins.append(res)
        specs.append(o_spec)
    est = 2 * (tm * tk * a.dtype.itemsize + tk * tn * b.dtype.itemsize + tm * tn * 4 * (2 if res is not None else 1))
    est += tm * tk * 2 + tk * tn * 2 + tm * tn * 4
    if exchange is None:
        scratch = [pltpu.VMEM((tm, tn), F32)] if (out_dtype != F32 and nk > 1) else []
        return pl.pallas_call(
            body, name=name, grid=grid, in_specs=specs, out_specs=o_spec,
            out_shape=SDS((M, N), out_dtype), scratch_shapes=scratch,
            compiler_params=_cp(("arbitrary", "arbitrary", "arbitrary"), est))(*ins)
    assert out_dtype == F32
    hbm = pl.BlockSpec(memory_space=pltpu.HBM)
    cp = _cp(("arbitrary", "arbitrary", "arbitrary"), est)
    out = pl.pallas_call(
        body, name=name, grid=grid, in_specs=specs + [hbm] * n_ex, out_specs=[o_spec] + [hbm] * n_ex,
        out_shape=[SDS((M, N), F32)] + list(exchange.out_shape),
        scratch_shapes=[pltpu.SemaphoreType.DMA((exchange.n_sems,)), pltpu.SemaphoreType.DMA((exchange.n_sems,)),
                        pltpu.SemaphoreType.DMA((exchange.n_local,))],
        compiler_params=pltpu.CompilerParams(
            dimension_semantics=cp.dimension_semantics, vmem_limit_bytes=cp.vmem_limit_bytes,
            has_side_effects=True))(*ins, *exchange.arrays)
    return out[0], out[1:]


def rms_fwd(x, nw, name):
    L, D = x.shape
    tm = _pick(L, 512, SUBLANES)

    def body(x_ref, w_ref, h_ref):
        xx = x_ref[...]
        r = lax.rsqrt(jnp.mean(xx * xx, axis=-1, keepdims=True) + EPS)
        h_ref[...] = (xx * r * w_ref[...]).astype(BF16)

    return pl.pallas_call(
        body, name=name, grid=(L // tm,),
        in_specs=[pl.BlockSpec((tm, D), lambda i: (i, 0)), pl.BlockSpec((1, D), lambda i: (0, 0))],
        out_specs=pl.BlockSpec((tm, D), lambda i: (i, 0)), out_shape=SDS((L, D), BF16),
        compiler_params=_cp(("arbitrary",), 2 * tm * D * 6))(x, nw)


def rms_bwd(dh, x, nw, dres, name):
    L, D = x.shape
    tm = _pick(L, 256, SUBLANES)

    def body(dh_ref, x_ref, w_ref, dr_ref, dx_ref, dw_ref):
        @pl.when(pl.program_id(0) == 0)
        def _():
            dw_ref[...] = jnp.zeros_like(dw_ref)

        xx, d = x_ref[...], dh_ref[...]
        r = lax.rsqrt(jnp.mean(xx * xx, axis=-1, keepdims=True) + EPS)
        xh = xx * r
        dw_ref[0:1, :] += jnp.sum(d * xh, axis=0, keepdims=True)
        dxh = d * w_ref[...]
        dx_ref[...] = dr_ref[...] + r * (dxh - xh * jnp.mean(dxh * xh, axis=-1, keepdims=True))

    row = pl.BlockSpec((tm, D), lambda i: (i, 0))
    return pl.pallas_call(
        body, name=name, grid=(L // tm,),
        in_specs=[row, row, pl.BlockSpec((1, D), lambda i: (0, 0)), row],
        out_specs=[row, pl.BlockSpec((SUBLANES, D), lambda i: (0, 0))],
        out_shape=[SDS((L, D), F32), SDS((SUBLANES, D), F32)],
        compiler_params=_cp(("arbitrary",), 2 * tm * D * 16))(dh, x, nw, dres)


def loss_head(h, fw, tgt, name):
    L, D = h.shape
    tm = _pick(L, 256, SUBLANES)

    def body(h_ref, w_ref, t_ref, dh_ref, dw_ref, ls_ref):
        @pl.when(pl.program_id(0) == 0)
        def _():
            dw_ref[...] = jnp.zeros_like(dw_ref)
            ls_ref[...] = jnp.zeros_like(ls_ref)

        xx = h_ref[...]
        r = lax.rsqrt(jnp.mean(xx * xx, axis=-1, keepdims=True) + EPS)
        xh = xx * r
        err = xh * w_ref[...] - t_ref[...]
        per_tok = jnp.mean(err * err, axis=-1, keepdims=True)
        ls_ref[...] += jnp.broadcast_to(0.5 * jnp.sum(per_tok, axis=0, keepdims=True), ls_ref.shape)
        dy = err * (1.0 / D)
        dw_ref[0:1, :] += jnp.sum(dy * xh, axis=0, keepdims=True)
        dxh = dy * w_ref[...]
        dh_ref[...] = r * (dxh - xh * jnp.mean(dxh * xh, axis=-1, keepdims=True))

    row = pl.BlockSpec((tm, D), lambda i: (i, 0))
    return pl.pallas_call(
        body, name=name, grid=(L // tm,),
        in_specs=[row, pl.BlockSpec((1, D), lambda i: (0, 0)), row],
        out_specs=[row, pl.BlockSpec((SUBLANES, D), lambda i: (0, 0)),
                   pl.BlockSpec((SUBLANES, LANES), lambda i: (0, 0))],
        out_shape=[SDS((L, D), F32), SDS((SUBLANES, D), F32), SDS((SUBLANES, LANES), F32)],
        compiler_params=_cp(("arbitrary",), 2 * tm * D * 12))(h, fw, tgt)


def _shifted(u, edge, s, back):
    n = u.shape[0]
    row = lax.broadcasted_iota(jnp.int32, (SUBLANES, u.shape[1]), 0)
    if back:
        r = pltpu.roll(u, s, 0)
        head = jnp.where(row < s, pltpu.roll(edge, s, 0), r[0:SUBLANES])
        return jnp.concatenate([head, r[SUBLANES:]], axis=0)
    r = pltpu.roll(u, n - s, 0)
    tail = jnp.where(row >= SUBLANES - s, pltpu.roll(edge, SUBLANES - s, 0), r[n - SUBLANES:])
    return jnp.concatenate([r[:n - SUBLANES], tail], axis=0)


def _conv_pre(u, prev, w, b):
    acc = b + w[SSD_CONV - 1:SSD_CONV] * u
    taps = [u]
    for s in range(1, SSD_CONV):
        us = _shifted(u, prev, s, True)
        taps.append(us)
        acc = acc + w[SSD_CONV - 1 - s:SSD_CONV - s] * us
    return acc, taps


def _conv_specs(dm, tm, tc, col0):
    rb = tm // SUBLANES
    u_spec = pl.BlockSpec((tm, tc), lambda j, i: (i, col0 + j))
    prev_spec = pl.BlockSpec((SUBLANES, tc), lambda j, i: (jnp.maximum(i * rb - 1, 0), col0 + j))
    w_spec = pl.BlockSpec((SSD_CONV, tc), lambda j, i: (0, j))
    b_spec = pl.BlockSpec((1, tc), lambda j, i: (0, j))
    return u_spec, prev_spec, w_spec, b_spec


def conv_fwd(proj, cw, cb, dm, name):
    L, CD = dm.L, dm.CD
    tm, tc = _pick(L, 1024, SUBLANES), _pick(math.gcd(CD, dm.SW), 512, LANES)
    u_spec, prev_spec, w_spec, b_spec = _conv_specs(dm, tm, tc, dm.SW // tc)

    def body(u_ref, p_ref, w_ref, b_ref, o_ref):
        prev = jnp.where(pl.program_id(1) == 0, 0.0, p_ref[...])
        c, _ = _conv_pre(u_ref[...], prev, w_ref[...], b_ref[...])
        o_ref[...] = c * _sigmoid(c)

    return pl.pallas_call(
        body, name=name, grid=(CD // tc, L // tm),
        in_specs=[u_spec, prev_spec, w_spec, b_spec],
        out_specs=pl.BlockSpec((tm, tc), lambda j, i: (i, j)), out_shape=SDS((L, CD), F32),
        compiler_params=_cp(("arbitrary", "arbitrary"), 12 * tm * tc * 4))(proj, proj, cw, cb)


def conv_bwd_pre(proj, dxs, db, dcm, cw, cb, dm, name):
    L, CD = dm.L, dm.CD
    gn = dm.G * SSD_STATE
    tm, tc = _pick(L, 1024, SUBLANES), _pick(math.gcd(gn, dm.SW), 512, LANES)
    u_spec, prev_spec, w_spec, b_spec = _conv_specs(dm, tm, tc, dm.SW // tc)
    nx, nb = dm.SW // tc, gn // tc

    def body(u_ref, p_ref, dx_ref, db_ref, dcm_ref, w_ref, b_ref, dc_ref, g_ref):
        j = pl.program_id(0)

        @pl.when(pl.program_id(1) == 0)
        def _():
            g_ref[...] = jnp.zeros_like(g_ref)

        prev = jnp.where(pl.program_id(1) == 0, 0.0, p_ref[...])
        c, taps = _conv_pre(u_ref[...], prev, w_ref[...], b_ref[...])
        sg = _sigmoid(c)
        d = jnp.where(j < nx, dx_ref[...], jnp.where(j < nx + nb, db_ref[...], dcm_ref[...]))
        dc = d * (sg * (1.0 + c * (1.0 - sg)))
        dc_ref[...] = dc
        for s in range(SSD_CONV):
            g_ref[SSD_CONV - 1 - s:SSD_CONV - s, :] += jnp.sum(dc * taps[s], axis=0, keepdims=True)
        g_ref[SSD_CONV:SSD_CONV + 1, :] += jnp.sum(dc, axis=0, keepdims=True)

    blk = pl.BlockSpec((tm, tc), lambda j, i: (i, j))
    part = lambda lo, n: pl.BlockSpec((tm, tc), lambda j, i: (i, jnp.clip(j - lo, 0, n - 1)))
    return pl.pallas_call(
        body, name=name, grid=(CD // tc, L // tm),
        in_specs=[u_spec, prev_spec, part(0, nx), part(nx, nb), part(nx + nb, nb), w_spec, b_spec],
        out_specs=[blk, pl.BlockSpec((SUBLANES, tc), lambda j, i: (0, j))],
        out_shape=[SDS((L, CD), F32), SDS((SUBLANES, CD), F32)],
        compiler_params=_cp(("arbitrary", "arbitrary"), 20 * tm * tc * 4))(proj, proj, dxs, db, dcm, cw, cb)


def conv_bwd_in(dc, cw, dm, name):
    L, CD = dm.L, dm.CD
    tm, tc = _pick(L, 1024, SUBLANES), _pick(CD, 512, LANES)
    rb, nrow = tm // SUBLANES, L // SUBLANES
    ni = L // tm

    def body(d_ref, n_ref, w_ref, o_ref):
        nxt = jnp.where(pl.program_id(1) == ni - 1, 0.0, n_ref[...])
        dc_, w = d_ref[...], w_ref[...]
        acc = w[SSD_CONV - 1:SSD_CONV] * dc_
        for s in range(1, SSD_CONV):
            acc = acc + w[SSD_CONV - 1 - s:SSD_CONV - s] * _shifted(dc_, nxt, s, False)
        o_ref[...] = acc.astype(BF16)

    blk = pl.BlockSpec((tm, tc), lambda j, i: (i, j))
    return pl.pallas_call(
        body, name=name, grid=(CD // tc, ni),
        in_specs=[blk, pl.BlockSpec((SUBLANES, tc), lambda j, i: (jnp.minimum((i + 1) * rb, nrow - 1), j)),
                  pl.BlockSpec((SSD_CONV, tc), lambda j, i: (0, j))],
        out_specs=blk, out_shape=SDS((L, CD), BF16),
        compiler_params=_cp(("arbitrary", "arbitrary"), 10 * tm * tc * 4))(dc, dc, cw)


def _ssd_chunk(dm):
    return _pick(dm.L, 512, LANES)


def _dt_parts(dtr, par):
    return _softplus(dtr + par[0:1]), -jnp.exp(par[1:2])


def _tri(Q):
    rows = lax.broadcasted_iota(jnp.int32, (Q, Q), 0)
    cols = lax.broadcasted_iota(jnp.int32, (Q, Q), 1)
    return rows, cols


def _lanes_to_group(v, g, R, axis):
    n = v.shape[axis]
    return v if g == 0 else pltpu.roll(v, n - g * R, axis)


def ssd_prep(proj, par_all, dm, name):
    Q, G, R = _ssd_chunk(dm), dm.G, dm.R
    nc = dm.L // Q

    def body(dtr_ref, par_ref, dt_ref, ac_ref, at_ref):
        dt, a = _dt_parts(dtr_ref[...], par_ref[...])
        da = dt * a
        rows, cols = _tri(Q)
        acum = jnp.dot((rows >= cols).astype(F32), da, precision=HIGHEST, preferred_element_type=F32)
        acum_t = lax.dot_general(da, (rows <= cols).astype(F32), (_DOT_DIMS["tn"], ((), ())),
                                 precision=HIGHEST, preferred_element_type=F32)
        for g in range(G):
            dt_ref[g] = _lanes_to_group(dt, g, R, 1)
            ac_ref[g] = _lanes_to_group(acum, g, R, 1)
            at_ref[g, 0] = _lanes_to_group(acum_t, g, R, 0)[0:SUBLANES]

    lane_blk = pl.BlockSpec((G, Q, LANES), lambda c: (0, c, 0))
    return pl.pallas_call(
        body, name=name, grid=(nc,),
        in_specs=[pl.BlockSpec((Q, LANES), lambda c: (c, dm.dt_off // LANES)),
                  pl.BlockSpec((SUBLANES, LANES), lambda c: (0, 0))],
        out_specs=[lane_blk, lane_blk, pl.BlockSpec((G, 1, SUBLANES, Q), lambda c: (0, c, 0, 0))],
        out_shape=[SDS((G, dm.L, LANES), F32), SDS((G, dm.L, LANES), F32), SDS((G, nc, SUBLANES, Q), F32)],
        compiler_params=_cp(("arbitrary",), 8 << 20))(proj, par_all)


def _ssd_common(xa_refs, Q):
    _, b_ref, c_ref = xa_refs
    rows, cols = _tri(Q)
    bm, cm = b_ref[...].astype(BF16), c_ref[...].astype(BF16)
    gm = lax.dot_general(cm, bm, (_DOT_DIMS["nt"], ((), ())), preferred_element_type=F32)
    return rows >= cols, bm, cm, gm


def _ssd_head(r, Q, x, dt, tri, acum, acum_t, gm):
    P = SSD_HEAD_DIM
    col = jnp.broadcast_to(acum[:, r:r + 1], (Q, Q))
    row = acum_t[r:r + 1, :]
    lam = jnp.where(tri, jnp.exp(jnp.minimum(col - row, 0.0)), 0.0)
    m = gm * lam
    xh = x[:, r * P:(r + 1) * P]
    xs = xh * dt[:, r:r + 1]
    a_last = acum_t[r:r + 1, Q - 1:Q]
    return col[:, :P], lam, m, xh, xs, a_last


def _ssd_specs(dm, Q, rev):
    nc = dm.L // Q
    cc = (lambda c: nc - 1 - c) if rev else (lambda c: c)
    nb = dm.SW // SSD_STATE
    x_spec = pl.BlockSpec((Q, dm.GW), lambda g, c: (cc(c), g))
    b_spec = pl.BlockSpec((Q, SSD_STATE), lambda g, c: (cc(c), nb + g))
    c_spec = pl.BlockSpec((Q, SSD_STATE), lambda g, c: (cc(c), nb + dm.G + g))
    dt_spec = pl.BlockSpec((1, Q, LANES), lambda g, c: (g, cc(c), 0))
    at_spec = pl.BlockSpec((1, 1, SUBLANES, Q), lambda g, c: (g, cc(c), 0, 0))
    par_spec = pl.BlockSpec((1, SUBLANES, LANES), lambda g, c: (g, 0, 0))
    h_spec = pl.BlockSpec((1, 1, dm.GW, SSD_STATE), lambda g, c: (cc(c), g, 0, 0))
    return x_spec, b_spec, c_spec, dt_spec, at_spec, par_spec, h_spec


def ssd_fwd(xa, prep, par, dm, name):
    Q = _ssd_chunk(dm)
    nc, P, R = dm.L // Q, SSD_HEAD_DIM, dm.R
    x_spec, b_spec, c_spec, dt_spec, at_spec, par_spec, h_spec = _ssd_specs(dm, Q, False)

    def body(x_ref, b_ref, c_ref, dt_ref, ac_ref, at_ref, par_ref, y_ref, hp_ref, h_scr):
        @pl.when(pl.program_id(1) == 0)
        def _():
            h_scr[...] = jnp.zeros_like(h_scr)

        tri, bm, cm, gm = _ssd_common((x_ref, b_ref, c_ref), Q)
        par, dt, acum, acum_t = par_ref[0], dt_ref[0], ac_ref[0], at_ref[0, 0]
        x = x_ref[...]
        hp = h_scr[...]
        hp_ref[0, 0] = hp
        for r in range(R):
            col, lam, m, xh, xs, a_last = _ssd_head(r, Q, x, dt, tri, acum, acum_t, gm)
            hpr = hp[r * P:(r + 1) * P, :]
            ydiag = jnp.dot(m.astype(BF16), xs.astype(BF16), preferred_element_type=F32)
            yoff = jnp.exp(col) * lax.dot_general(cm, hpr.astype(BF16), (_DOT_DIMS["nt"], ((), ())),
                                                  preferred_element_type=F32)
            dte = jnp.exp(a_last - col)
            st = lax.dot_general((xs * dte).astype(BF16), bm, (_DOT_DIMS["tn"], ((), ())),
                                 preferred_element_type=F32)
            h_scr[r * P:(r + 1) * P, :] = jnp.exp(a_last) * hpr + st
            y_ref[:, r * P:(r + 1) * P] = ydiag + yoff + par[2:3, r:r + 1] * xh

    return pl.pallas_call(
        body, name=name, grid=(dm.G, nc),
        in_specs=[x_spec, b_spec, c_spec, dt_spec, dt_spec, at_spec, par_spec],
        out_specs=[x_spec, h_spec],
        out_shape=[SDS((dm.L, dm.SW), F32), SDS((nc, dm.G, dm.GW, SSD_STATE), F32)],
        scratch_shapes=[pltpu.VMEM((dm.GW, SSD_STATE), F32)],
        compiler_params=_cp(("arbitrary", "arbitrary"), 24 << 20))(xa, xa, xa, *prep, par)


def ssd_bwd(xa, prep, par, hprev, dyo, dm, name):
    Q = _ssd_chunk(dm)
    nc, P, R, N = dm.L // Q, SSD_HEAD_DIM, dm.R, SSD_STATE
    x_spec, b_spec, c_spec, dt_spec, at_spec, par_spec, h_spec = _ssd_specs(dm, Q, True)
    bc_spec = pl.BlockSpec((Q, N), lambda g, c: (nc - 1 - c, g))
    NT, TN = (_DOT_DIMS["nt"], ((), ())), (_DOT_DIMS["tn"], ((), ()))

    def body(x_ref, b_ref, c_ref, dt_ref, ac_ref, at_ref, par_ref, hp_ref, dy_ref,
             dx_ref, db_ref, dc_ref, dac_ref, span_ref, ddt_ref, gs_ref, dh_scr):
        @pl.when(pl.program_id(1) == 0)
        def _():
            dh_scr[...] = jnp.zeros_like(dh_scr)
            gs_ref[...] = jnp.zeros_like(gs_ref)

        tri, bm, cm, gm = _ssd_common((x_ref, b_ref, c_ref), Q)
        par, dt, acum, acum_t = par_ref[0], dt_ref[0], ac_ref[0], at_ref[0, 0]
        x, dy, hp, dhn = x_ref[...], dy_ref[...].astype(F32), hp_ref[0, 0], dh_scr[...]
        lane = lax.broadcasted_iota(jnp.int32, (Q, LANES), 1)
        rowi = lax.broadcasted_iota(jnp.int32, (Q, LANES), 0)
        d_acum = jnp.zeros((Q, LANES), F32)
        d_dt = jnp.zeros((Q, LANES), F32)
        d_skip = jnp.zeros((1, LANES), F32)
        dgsum = jnp.zeros((Q, Q), F32)
        dye_all, xse_all, span_rows = [], [], []
        rq = lax.broadcasted_iota(jnp.int32, (Q, Q), 0)
        cq = lax.broadcasted_iota(jnp.int32, (Q, Q), 1)
        ue = (rq < cq).astype(BF16)
        for r in range(R):
            col, lam, m, xh, xs, a_last = _ssd_head(r, Q, x, dt, tri, acum, acum_t, gm)
            sl = slice(r * P, (r + 1) * P)
            dyh, hpr, dhr = dy[:, sl], hp[sl, :], dhn[sl, :]
            mb, xsb, dyb = m.astype(BF16), xs.astype(BF16), dyh.astype(BF16)
            e_a = jnp.exp(col)
            dte = jnp.exp(a_last - col)
            yoff = e_a * lax.dot_general(cm, hpr.astype(BF16), NT, preferred_element_type=F32)
            bdh = dte * lax.dot_general(bm, dhr.astype(BF16), NT, preferred_element_type=F32)
            dxs = lax.dot_general(mb, dyb, TN, preferred_element_type=F32) + bdh
            dm_ = lax.dot_general(dyb, xsb, NT, preferred_element_type=F32)
            dgsum = dgsum + dm_ * lam
            dye, xse = (dyh * e_a).astype(BF16), (xs * dte).astype(BF16)
            dye_all.append(dye)
            xse_all.append(xse)
            dh_scr[sl, :] = (jnp.exp(a_last) * dhr
                             + lax.dot_general(dye, cm, TN, preferred_element_type=F32))
            before = jnp.dot((dm_ * m).astype(BF16), ue, preferred_element_type=F32)
            span_rows.append(jnp.sum(jnp.where(tri, before, 0.0), axis=0, keepdims=True))
            da_col = jnp.sum(dyh * yoff - xs * bdh, axis=1, keepdims=True)
            da_last = (jnp.sum(xs * bdh, axis=(0, 1), keepdims=True)
                       + jnp.exp(a_last) * jnp.sum(dhr * hpr, axis=(0, 1), keepdims=True))
            d_acum = d_acum + jnp.where(lane == r, da_col + jnp.where(rowi == Q - 1, da_last, 0.0), 0.0)
            d_dt = d_dt + jnp.where(lane == r, jnp.sum(dxs * xh, axis=1, keepdims=True), 0.0)
            d_skip = d_skip + jnp.where(lane[0:1] == r, jnp.sum(dyh * xh, axis=(0, 1), keepdims=True), 0.0)
            dx_ref[:, sl] = dxs * dt[:, r:r + 1] + par[2:3, r:r + 1] * dyh
        dgb = dgsum.astype(BF16)
        dye_c = jnp.concatenate(dye_all, axis=1)
        xse_c = jnp.concatenate(xse_all, axis=1)
        dc_ref[...] = (jnp.dot(dgb, bm, preferred_element_type=F32)
                       + jnp.dot(dye_c, hp.astype(BF16), preferred_element_type=F32))
        db_ref[...] = (lax.dot_general(dgb, cm, TN, preferred_element_type=F32)
                       + jnp.dot(xse_c, dhn.astype(BF16), preferred_element_type=F32))
        dac_ref[0] = d_acum
        ddt_ref[0] = d_dt
        span_ref[0, 0] = jnp.concatenate(span_rows + [jnp.zeros((SUBLANES - R, Q), F32)] * (R < SUBLANES), axis=0)
        gs_ref[0, 2:3, :] += d_skip

    return pl.pallas_call(
        body, name=name, grid=(dm.G, nc),
        in_specs=[x_spec, b_spec, c_spec, dt_spec, dt_spec, at_spec, par_spec, h_spec, x_spec],
        out_specs=[x_spec, bc_spec, bc_spec, dt_spec, at_spec, dt_spec, par_spec],
        out_shape=[SDS((dm.L, dm.SW), F32), SDS((dm.L, dm.G * N), F32), SDS((dm.L, dm.G * N), F32),
                   SDS((dm.G, dm.L, LANES), F32), SDS((dm.G, nc, SUBLANES, Q), F32),
                   SDS((dm.G, dm.L, LANES), F32), SDS((dm.G, SUBLANES, LANES), F32)],
        scratch_shapes=[pltpu.VMEM((dm.GW, N), F32)],
        compiler_params=_cp(("arbitrary", "arbitrary"), 28 << 20))(xa, xa, xa, *prep, par, hprev, dyo)


def ssd_post(proj, par_all, dac, span, ddt, gs, dm, name):
    Q, G, R = _ssd_chunk(dm), dm.G, dm.R
    nc = dm.L // Q
    NT = (_DOT_DIMS["nt"], ((), ()))

    def body(dtr_ref, par_ref, dac_ref, span_ref, ddt_ref, gs_ref, out_ref, acc_ref):
        @pl.when(pl.program_id(0) == 0)
        def _():
            acc_ref[...] = jnp.zeros_like(acc_ref)

        par = par_ref[...]
        dt, a = _dt_parts(dtr_ref[...], par)
        lane = lax.broadcasted_iota(jnp.int32, (Q, LANES), 1)
        row8 = lax.broadcasted_iota(jnp.int32, (SUBLANES, Q), 0)

        def heads(v, g):
            v = jnp.where(lane[:v.shape[0]] < R, v, 0.0)
            return v if g == 0 else pltpu.roll(v, g * R, 1)

        d_acum = sum(heads(dac_ref[g], g) for g in range(G))
        d_dtx = sum(heads(ddt_ref[g], g) for g in range(G))
        d_skip = sum(heads(gs_ref[g][2:3], g) for g in range(G))
        span_t = jnp.zeros((LANES, Q), F32)
        for g in range(G):
            rows_g = jnp.concatenate([jnp.where(row8 < R, span_ref[g, 0], 0.0),
                                      jnp.zeros((LANES - SUBLANES, Q), F32)], axis=0)
            span_t = span_t + (rows_g if g == 0 else pltpu.roll(rows_g, g * R, 0))
        rq, cq = _tri(Q)
        d_da = (jnp.dot((rq <= cq).astype(F32), d_acum, precision=HIGHEST, preferred_element_type=F32)
                + lax.dot_general((rq == cq).astype(F32), span_t, NT, precision=HIGHEST,
                                  preferred_element_type=F32))
        d_raw = (d_dtx + d_da * a) * _sigmoid(dtr_ref[...] + par[0:1])
        out_ref[...] = d_raw.astype(BF16)
        acc_ref[0:1, :] += jnp.sum(d_raw, axis=0, keepdims=True)
        acc_ref[1:2, :] += jnp.sum(d_da * dt, axis=0, keepdims=True) * a
        acc_ref[2:3, :] = d_skip

    lane_blk = pl.BlockSpec((G, Q, LANES), lambda c: (0, c, 0))
    small = pl.BlockSpec((SUBLANES, LANES), lambda c: (0, 0))
    return pl.pallas_call(
        body, name=name, grid=(nc,),
        in_specs=[pl.BlockSpec((Q, LANES), lambda c: (c, dm.dt_off // LANES)), small, lane_blk,
                  pl.BlockSpec((G, 1, SUBLANES, Q), lambda c: (0, c, 0, 0)), lane_blk,
                  pl.BlockSpec((G, SUBLANES, LANES), lambda c: (0, 0, 0))],
        out_specs=[pl.BlockSpec((Q, LANES), lambda c: (c, 0)), small],
        out_shape=[SDS((dm.L, LANES), BF16), SDS((SUBLANES, LANES), F32)],
        compiler_params=_cp(("arbitrary",), 8 << 20))(proj, par_all, dac, span, ddt, gs)


def _attn_tile(dm):
    return _pick(dm.L, 256, LANES)


def attn_fwd(proj, dm, name):
    L, T, DH, AH = dm.L, _attn_tile(dm), SBA_HEAD_DIM, dm.AH
    nq = L // T
    scale = 1.0 / math.sqrt(DH)
    NT = (_DOT_DIMS["nt"], ((), ()))

    def body(q_ref, k_ref, v_ref, o_ref, tot_ref, nb_ref, ks, vs, o_scr, acc_scr):
        h, i = pl.program_id(0), pl.program_id(1)

        @pl.when(i == 0)
        def _():
            ks[...] = k_ref[...].astype(BF16)
            vs[...] = v_ref[...].astype(BF16)

        qb = q_ref[...].astype(BF16)
        rows = lax.broadcasted_iota(jnp.int32, (T, T), 0)
        cols = lax.broadcasted_iota(jnp.int32, (T, T), 1)
        causal = cols < rows
        u_rev = (rows >= cols).astype(BF16)

        def scores(j, masked):
            sl = pl.ds(pl.multiple_of(j * T, T), T)
            z = lax.dot_general(qb, ks[sl, :], NT, preferred_element_type=F32) * scale
            sp = _softplus(z)
            if masked:
                sp = jnp.where(causal, sp, 0.0)
            cs = jnp.dot(sp.astype(BF16), u_rev, preferred_element_type=F32)
            return sl, z, cs

        def weighted(blk, acc, masked):
            sl, z, cs = blk
            w = jnp.exp(z - cs - acc)
            if masked:
                w = jnp.where(causal, w, 0.0)
            return jnp.dot(w.astype(BF16), vs[sl, :], preferred_element_type=F32), acc + cs[:, 0:1]

        zero = jnp.zeros((T, 1), F32)

        @pl.when(i == 0)
        def _():
            o_scr[...], acc_scr[...] = weighted(scores(i, True), zero, True)

        @pl.when(i > 0)
        def _():
            diag, prev = scores(i, True), scores(i - 1, False)
            pv0, acc1 = weighted(diag, zero, True)
            pv1, acc2 = weighted(prev, acc1, False)
            o_scr[...] = pv0 + pv1
            acc_scr[...] = acc2

        def cond(c):
            return jnp.logical_and(c[0] >= 0, c[1] < SKIP_SUM)

        def loop(c):
            pv, acc = weighted(scores(c[0], False), acc_scr[...], False)
            o_scr[...] += pv
            acc_scr[...] = acc
            return c[0] - 1, jnp.min(acc)

        j_end, _ = lax.while_loop(cond, loop, (jnp.where(i > 0, i - 2, -1), jnp.min(acc_scr[...])))
        o_ref[...] = o_scr[...]
        tot_ref[0] = jnp.broadcast_to(acc_scr[...], (T, LANES))
        nb_ref[h, i] = i - j_end

    kv = lambda off: pl.BlockSpec((L, DH), lambda h, i: (0, off // DH + h))
    est = 2 * 2 * L * DH * 4 + 2 * L * DH * 2 + 12 * T * T * 4
    return pl.pallas_call(
        body, name=name, grid=(AH, nq),
        in_specs=[pl.BlockSpec((T, DH), lambda h, i: (i, dm.q_off // DH + h)), kv(dm.k_off), kv(dm.v_off)],
        out_specs=[pl.BlockSpec((T, DH), lambda h, i: (i, h)),
                   pl.BlockSpec((1, T, LANES), lambda h, i: (h, i, 0)),
                   pl.BlockSpec(memory_space=pltpu.SMEM)],
        out_shape=[SDS((L, dm.AW), F32), SDS((AH, L, LANES), F32), SDS((AH, nq), jnp.int32)],
        scratch_shapes=[pltpu.VMEM((L, DH), BF16), pltpu.VMEM((L, DH), BF16),
                        pltpu.VMEM((T, DH), F32), pltpu.VMEM((T, 1), F32)],
        compiler_params=_cp(("arbitrary", "arbitrary"), est))(proj, proj, proj)


def attn_bwd(proj, dyo, tot, nblk, dm, name):
    L, T, DH, AH = dm.L, _attn_tile(dm), SBA_HEAD_DIM, dm.AH
    nq = L // T
    scale = 1.0 / math.sqrt(DH)
    NT, TN = (_DOT_DIMS["nt"], ((), ())), (_DOT_DIMS["tn"], ((), ()))

    def body(nb_ref, q_ref, k_ref, v_ref, do_ref, tot_ref, dq_ref, dk_out, dv_out,
             ks, vs, dq_scr, p_scr, e_scr, dk_ref, dv_ref):
        h, i = pl.program_id(0), pl.program_id(1)

        @pl.when(i == 0)
        def _():
            ks[...] = k_ref[...].astype(BF16)
            vs[...] = v_ref[...].astype(BF16)
            dk_ref[...] = jnp.zeros_like(dk_ref)
            dv_ref[...] = jnp.zeros_like(dv_ref)

        qb = q_ref[...].astype(BF16)
        dob = do_ref[...].astype(BF16)
        tot_c = tot_ref[0][:, 0:1]
        rows = lax.broadcasted_iota(jnp.int32, (T, T), 0)
        cols = lax.broadcasted_iota(jnp.int32, (T, T), 1)
        causal = cols < rows
        u_fwd = (rows <= cols).astype(BF16)
        dq_scr[...] = jnp.zeros_like(dq_scr)
        p_scr[...] = jnp.zeros_like(p_scr)
        e_scr[...] = jnp.zeros_like(e_scr)

        def blocks(js, masks, before, e_before):
            sls = [pl.ds(pl.multiple_of(j * T, T), T) for j in js]
            zs = [lax.dot_general(qb, ks[sl, :], NT, preferred_element_type=F32) * scale for sl in sls]
            sps = [_softplus(z) for z in zs]
            sps = [jnp.where(causal, sp, 0.0) if m else sp for sp, m in zip(sps, masks)]
            spbs = [sp.astype(BF16) for sp in sps]
            pins = [jnp.dot(spb, u_fwd, preferred_element_type=F32) for spb in spbs]
            dws = [lax.dot_general(dob, vs[sl, :], NT, preferred_element_type=F32) for sl in sls]
            ws = []
            for z, spb, pin, m in zip(zs, spbs, pins, masks):
                cs = (tot_c - before) - (pin - spb.astype(F32))
                w = jnp.exp(z - cs)
                ws.append(jnp.where(causal, w, 0.0) if m else w)
                before = before + pin[:, T - 1:T]
            es = [dw * w for dw, w in zip(dws, ws)]
            fins = [jnp.dot(e.astype(BF16), u_fwd, preferred_element_type=F32) for e in es]
            dzbs = []
            for z, sp, e, fin, m in zip(zs, sps, es, fins, masks):
                dz = (e - jnp.exp(z - sp) * (e_before + fin)) * scale
                dzbs.append((jnp.where(causal, dz, 0.0) if m else dz).astype(BF16))
                e_before = e_before + fin[:, T - 1:T]
            dqs = [jnp.dot(dzb, ks[sl, :], preferred_element_type=F32) for dzb, sl in zip(dzbs, sls)]
            dks = [lax.dot_general(dzb, qb, TN, preferred_element_type=F32) for dzb in dzbs]
            dvs = [lax.dot_general(w.astype(BF16), dob, TN, preferred_element_type=F32) for w in ws]
            for sl, dk, dv in zip(sls, dks, dvs):
                dk_ref[sl, :] += dk
                dv_ref[sl, :] += dv
            return sum(dqs[1:], dqs[0]), before, e_before

        def loop(j, carry):
            dq, p_scr[...], e_scr[...] = blocks([j], [False], p_scr[...], e_scr[...])
            dq_scr[...] += dq
            return carry

        lax.fori_loop(i - nb_ref[h, i] + 1, i - 1, loop, 0)

        @pl.when(i == 0)
        def _():
            dq, _, _ = blocks([i], [True], p_scr[...], e_scr[...])
            dq_ref[...] = (dq_scr[...] + dq).astype(BF16)

        @pl.when(i > 0)
        def _():
            dq, _, _ = blocks([i - 1, i], [False, True], p_scr[...], e_scr[...])
            dq_ref[...] = (dq_scr[...] + dq).astype(BF16)

        @pl.when(i == nq - 1)
        def _():
            dk_out[...] = dk_ref[...].astype(BF16)
            dv_out[...] = dv_ref[...].astype(BF16)

    kv = lambda off: pl.BlockSpec((L, DH), lambda h, i, nb: (0, off // DH + h))
    qblk = lambda off: pl.BlockSpec((T, DH), lambda h, i, nb: (i, off // DH + h))
    acc = pl.BlockSpec((L, DH), lambda h, i, nb: (0, h))
    est = 2 * 2 * L * DH * 4 * 2 + 2 * L * DH * 2 + 16 * T * T * 4
    grid_spec = pltpu.PrefetchScalarGridSpec(
        num_scalar_prefetch=1, grid=(AH, nq),
        in_specs=[qblk(dm.q_off), kv(dm.k_off), kv(dm.v_off), qblk(dm.SW),
                  pl.BlockSpec((1, T, LANES), lambda h, i, nb: (h, i, 0))],
        out_specs=[qblk(0), acc, acc],
        scratch_shapes=[pltpu.VMEM((L, DH), BF16), pltpu.VMEM((L, DH), BF16),
                        pltpu.VMEM((T, DH), F32), pltpu.VMEM((T, 1), F32), pltpu.VMEM((T, 1), F32),
                        pltpu.VMEM((L, DH), F32), pltpu.VMEM((L, DH), F32)])
    return pl.pallas_call(
        body, name=name, grid_spec=grid_spec,
        out_shape=[SDS((L, dm.AW), BF16)] * 3,
        compiler_params=_cp(("arbitrary", "arbitrary"), est))(nblk, proj, proj, proj, dyo, tot)


def _gate_specs(dm, tm, order):
    GW, G = dm.GW, dm.G
    ix = (lambda a, b: (a, b)) if order == "ij" else (lambda a, b: (b, a))

    def spec(colfn):
        return pl.BlockSpec((tm, GW), lambda p0, p1: (ix(p0, p1)[0], colfn(ix(p0, p1)[1])))

    y_spec = spec(lambda j: jnp.minimum(j, G - 1))
    o_spec = spec(lambda j: jnp.maximum(j - G, 0))
    zg_spec = spec(lambda j: jnp.where(j < G, j, dm.g_off // GW + j - G))
    w_spec = pl.BlockSpec((1, GW), lambda p0, p1: (0, jnp.minimum(ix(p0, p1)[1], G - 1)))
    full = spec(lambda j: j)
    return y_spec, o_spec, zg_spec, w_spec, full


def gate_fwd(y, o, proj, snw, dm, name):
    L, GW, G = dm.L, dm.GW, dm.G
    tm = _pick(L, 1024, SUBLANES)
    ncol = (dm.SW + dm.AW) // GW
    y_spec, o_spec, zg_spec, w_spec, full = _gate_specs(dm, tm, "ij")

    def body(y_ref, o_ref, zg_ref, w_ref, m_ref):
        j = pl.program_id(1)
        zg = zg_ref[...]
        gate = zg * _sigmoid(zg)

        @pl.when(j < G)
        def _():
            yz = y_ref[...] * gate
            r = lax.rsqrt(jnp.mean(yz * yz, axis=-1, keepdims=True) + EPS)
            m_ref[...] = (yz * r * w_ref[...]).astype(BF16)

        @pl.when(j >= G)
        def _():
            m_ref[...] = (o_ref[...] * gate).astype(BF16)

    return pl.pallas_call(
        body, name=name, grid=(L // tm, ncol),
        in_specs=[y_spec, o_spec, zg_spec, w_spec], out_specs=full,
        out_shape=SDS((L, dm.SW + dm.AW), BF16),
        compiler_params=_cp(("arbitrary", "arbitrary"), 2 * tm * GW * 16))(y, o, proj, snw)


def gate_bwd(dmix, y, o, proj, snw, dm, name):
    L, GW, G = dm.L, dm.GW, dm.G
    tm = _pick(L, 1024, SUBLANES)
    W = dm.SW + dm.AW
    y_spec, o_spec, zg_spec, w_spec, full = _gate_specs(dm, tm, "ji")

    def body(d_ref, y_ref, o_ref, zg_ref, w_ref, dyo_ref, dzg_ref, dw_ref):
        j = pl.program_id(0)

        @pl.when(pl.program_id(1) == 0)
        def _():
            dw_ref[...] = jnp.zeros_like(dw_ref)

        zg, d = zg_ref[...], d_ref[...].astype(F32)
        sg = _sigmoid(zg)
        gate = zg * sg
        dgate = sg * (1.0 + zg * (1.0 - sg))

        @pl.when(j < G)
        def _():
            yv = y_ref[...]
            yz = yv * gate
            r = lax.rsqrt(jnp.mean(yz * yz, axis=-1, keepdims=True) + EPS)
            nrm = yz * r
            dw_ref[0:1, :] += jnp.sum(d * nrm, axis=0, keepdims=True)
            dn = d * w_ref[...]
            dyz = r * (dn - nrm * jnp.mean(dn * nrm, axis=-1, keepdims=True))
            dyo_ref[...] = (dyz * gate).astype(BF16)
            dzg_ref[...] = (dyz * yv * dgate).astype(BF16)

        @pl.when(j >= G)
        def _():
            dyo_ref[...] = (d * gate).astype(BF16)
            dzg_ref[...] = (d * o_ref[...] * dgate).astype(BF16)

    return pl.pallas_call(
        body, name=name, grid=(W // GW, L // tm),
        in_specs=[full, y_spec, o_spec, zg_spec, w_spec],
        out_specs=[full, full, pl.BlockSpec((SUBLANES, GW), lambda j, i: (0, j))],
        out_shape=[SDS((L, W), BF16), SDS((L, W), BF16), SDS((SUBLANES, W), F32)],
        compiler_params=_cp(("arbitrary", "arbitrary"), 2 * tm * GW * 24))(dmix, y, o, proj, snw)


def adamw(parts, w, m, v, name):
    R, C = w.shape
    n_slot = parts.shape[0]
    tr = _pick(R, max(SUBLANES, (1 << 18) // C // SUBLANES * SUBLANES), SUBLANES)
    c1, c2 = 1.0 - ADAM_B1 ** ADAM_STEP, 1.0 - ADAM_B2 ** ADAM_STEP

    def body(p_ref, w_ref, m_ref, v_ref, g_ref, d_ref, m2_ref, v2_ref):
        g = p_ref[0].astype(F32)
        for s in range(1, n_slot):
            g = g + p_ref[s].astype(F32)
        m2 = ADAM_B1 * m_ref[...] + (1.0 - ADAM_B1) * g
        v2 = ADAM_B2 * v_ref[...] + (1.0 - ADAM_B2) * (g * g)
        g_ref[...] = g
        m2_ref[...] = m2
        v2_ref[...] = v2
        d_ref[...] = -ADAM_LR * ((m2 / c1) / (jnp.sqrt(v2 / c2) + ADAM_EPS) + ADAM_WD * w_ref[...])

    blk = pl.BlockSpec((tr, C), lambda i: (i, 0))
    return pl.pallas_call(
        body, name=name, grid=(R // tr,),
        in_specs=[pl.BlockSpec((n_slot, tr, C), lambda i: (0, i, 0)), blk, blk, blk],
        out_specs=[blk] * 4, out_shape=[SDS((R, C), F32)] * 4,
        compiler_params=_cp(("arbitrary",), 2 * tr * C * (n_slot * 4 + 28)))(parts, w, m, v)


N_CHIP = 4


def _place():
    x, y, c = lax.axis_index("x"), lax.axis_index("y"), lax.axis_index("c")
    return x, y, c, [(1 - x, y), (x, 1 - y), (1 - x, 1 - y)]


def _comm_call(body, arrays, out_shape, n_sems, n_local, name):
    hbm = pl.BlockSpec(memory_space=pltpu.HBM)
    return pl.pallas_call(
        body, name=name, in_specs=[hbm] * len(arrays), out_specs=[hbm] * len(out_shape), out_shape=out_shape,
        scratch_shapes=[pltpu.SemaphoreType.DMA((n_sems,)), pltpu.SemaphoreType.DMA((n_sems,)),
                        pltpu.SemaphoreType.DMA((n_local,))],
        compiler_params=pltpu.CompilerParams(has_side_effects=True))(*arrays)


def gather_weights(arrays, name):
    n, per = len(arrays), N_DEV - 1

    def body(*refs):
        srcs, dsts = refs[:n], refs[n:2 * n]
        send_sems, recv_sems, local_sems = refs[2 * n:]
        start, finish = _gather_halves(srcs, dsts, send_sems, recv_sems, local_sems)
        start()
        finish()

    out_shape = [SDS((N_DEV,) + a.shape, a.dtype) for a in arrays]
    return _comm_call(body, arrays, out_shape, n * per, n, name)


def _gather_halves(srcs, dsts, send_sems, recv_sems, local_sems):
    n, per = len(srcs), N_DEV - 1

    def parts():
        x, y, c, chips = _place()
        me, sib = 4 * x + 2 * y + c, (x, y, 1 - c)

        def cp(a, k, block, to, src=None):
            return pltpu.make_async_remote_copy(
                src_ref=dsts[a].at[block] if src is None else src, dst_ref=dsts[a].at[block],
                send_sem=send_sems.at[a * per + k], recv_sem=recv_sems.at[a * per + k],
                device_id=to, device_id_type=MESH)

        own = [pltpu.make_async_copy(srcs[a], dsts[a].at[me], local_sems.at[a]) for a in range(n)]
        first = []
        for a in range(n):
            first.append(cp(a, 0, me, sib, src=srcs[a]))
            first += [cp(a, 1 + j, me, (px, py, c), src=srcs[a]) for j, (px, py) in enumerate(chips)]
        return x, y, c, chips, sib, cp, own, first

    def start():
        *_, own, first = parts()
        for o in own:
            o.start()
        for f in first:
            f.start()

    def finish():
        x, y, c, chips, sib, cp, own, first = parts()
        passed = []
        for j, (px, py) in enumerate(chips):
            block = 4 * px + 2 * py + c
            for a in range(n):
                cp(a, 1 + j, block, sib).wait_recv()
                fwd = cp(a, 4 + j, block, sib)
                fwd.start()
                passed.append(fwd)
        for a in range(n):
            cp(a, 0, 4 * x + 2 * y + 1 - c, sib).wait_recv()
            for j, (px, py) in enumerate(chips):
                cp(a, 4 + j, 4 * px + 2 * py + 1 - c, sib).wait_recv()
        for f in first + passed:
            f.wait_send()
        for o in own:
            o.wait()

    return start, finish


def gather_beside(shards):
    shards = list(shards)
    return Exchange(arrays=shards, out_shape=[SDS((N_DEV,) + s.shape, s.dtype) for s in shards],
                    n_sems=len(shards) * (N_DEV - 1), n_local=len(shards), halves=_gather_halves)


def pair_exchange(arrays, name):
    n = len(arrays)

    def body(*refs):
        srcs, dsts = refs[:n], refs[n:2 * n]
        send_sems, recv_sems, _ = refs[2 * n:]
        x, y, c, _chips = _place()
        sib = (x, y, 1 - c)

        def cp(a, k):
            return pltpu.make_async_remote_copy(
                src_ref=srcs[a].at[2 * k + 1 - c], dst_ref=dsts[a].at[k],
                send_sem=send_sems.at[a * N_CHIP + k], recv_sem=recv_sems.at[a * N_CHIP + k],
                device_id=sib, device_id_type=MESH)

        cps = [cp(a, k) for k in range(N_CHIP) for a in range(n)]
        for p in cps:
            p.start()
        for p in cps:
            p.wait_recv()
        for p in cps:
            p.wait_send()

    out_shape = [SDS((N_CHIP,) + a.shape[1:], a.dtype) for a in arrays]
    return _comm_call(body, arrays, out_shape, n * N_CHIP, 1, name)


def pair_add(parts, got, name):
    _, R, C = parts.shape
    tr = _pick(R, max(16, (1 << 19) // C // 16 * 16), 16)
    core = lax.axis_index("c").astype(jnp.int32).reshape(1)

    def body(c_ref, p_ref, g_ref, o_ref):
        o_ref[...] = (p_ref[...].astype(F32) + g_ref[...].astype(F32)).astype(o_ref.dtype)

    grid_spec = pltpu.PrefetchScalarGridSpec(
        num_scalar_prefetch=1, grid=(N_CHIP, R // tr),
        in_specs=[pl.BlockSpec((1, tr, C), lambda k, i, c_ref: (2 * k + c_ref[0], i, 0)),
                  pl.BlockSpec((1, tr, C), lambda k, i, c_ref: (k, i, 0))],
        out_specs=pl.BlockSpec((1, tr, C), lambda k, i, c_ref: (k, i, 0)))
    return pl.pallas_call(
        body, name=name, grid_spec=grid_spec, out_shape=SDS((N_CHIP, R, C), parts.dtype),
        compiler_params=_cp(("arbitrary", "arbitrary"), 2 * 3 * tr * C * 2 + 3 * tr * C * 4))(core, parts, got)


def chip_exchange_spec(sums, full=(), split=()):
    ns, nf, nsp = len(sums), len(full), len(split)
    n_sem = 3 * ns + (N_DEV - 1) * (nf + nsp)

    def copies(srcs, dsts, send_sems, recv_sems, local_sems):
        x, y, c, chips = _place()
        me, my_chip = 4 * x + 2 * y + c, 2 * x + y
        started, arrivals, own = [], [], []
        for a in range(ns):
            own.append(pltpu.make_async_copy(srcs[a].at[my_chip], dsts[a].at[my_chip], local_sems.at[a]))
            for j, (px, py) in enumerate(chips):
                k = 2 * px + py
                sem = 3 * a + j
                started.append(pltpu.make_async_remote_copy(
                    src_ref=srcs[a].at[k], dst_ref=dsts[a].at[my_chip],
                    send_sem=send_sems.at[sem], recv_sem=recv_sems.at[sem],
                    device_id=(px, py, c), device_id_type=MESH))
                arrivals.append(dict(
                    src_ref=srcs[a].at[my_chip], dst_ref=dsts[a].at[k],
                    send_sem=send_sems.at[sem], recv_sem=recv_sems.at[sem],
                    device_id=(px, py, c), device_id_type=MESH))
        for b in range(nf + nsp):
            a = ns + b
            is_split = b >= nf
            own.append(pltpu.make_async_copy(srcs[a].at[me] if is_split else srcs[a], dsts[a].at[me],
                                             local_sems.at[a]))
            for rel in range(1, N_DEV):
                px, py, pc = x ^ ((rel >> 2) & 1), y ^ ((rel >> 1) & 1), c ^ (rel & 1)
                pidx = 4 * px + 2 * py + pc
                sem = 3 * ns + b * (N_DEV - 1) + rel - 1
                started.append(pltpu.make_async_remote_copy(
                    src_ref=srcs[a].at[pidx] if is_split else srcs[a], dst_ref=dsts[a].at[me],
                    send_sem=send_sems.at[sem], recv_sem=recv_sems.at[sem],
                    device_id=(px, py, pc), device_id_type=MESH))
                arrivals.append(dict(
                    src_ref=srcs[a].at[me] if is_split else srcs[a], dst_ref=dsts[a].at[pidx],
                    send_sem=send_sems.at[sem], recv_sem=recv_sems.at[sem],
                    device_id=(px, py, pc), device_id_type=MESH))
        return own, started, arrivals

    def halves(*refs):
        def start():
            own, started, _ = copies(*refs)
            for o in own:
                o.start()
            for s in started:
                s.start()

        def finish():
            own, started, arrivals = copies(*refs)
            for r in arrivals:
                pltpu.make_async_remote_copy(**r).wait_recv()
            for s in started:
                s.wait_send()
            for o in own:
                o.wait()

        return start, finish

    out_shape = ([SDS(a.shape, a.dtype) for a in sums] + [SDS((N_DEV,) + a.shape, a.dtype) for a in full]
                 + [SDS(a.shape, a.dtype) for a in split])
    return Exchange(arrays=list(sums) + list(full) + list(split), out_shape=out_shape, n_sems=n_sem,
                    n_local=ns + nf + nsp, halves=halves)


def chip_exchange(sums, full, split, name):
    ex = chip_exchange_spec(sums, full, split)
    n = len(ex.arrays)

    def body(*refs):
        start, finish = ex.halves(refs[:n], refs[n:2 * n], *refs[2 * n:])
        start()
        finish()

    return _comm_call(body, ex.arrays, ex.out_shape, ex.n_sems, ex.n_local, name)


class LayerParams(NamedTuple):
    nw: jax.Array
    w_in: jax.Array
    cw: jax.Array
    cb: jax.Array
    par: jax.Array
    par_all: jax.Array
    snw: jax.Array
    w_out: jax.Array


def head_params(dt_bias, a_log, d_skip, dm):
    rows = jnp.stack([dt_bias, a_log, d_skip])
    par_all = jnp.pad(rows, ((0, SUBLANES - 3), (0, LANES - dm.NH)))
    par = jnp.pad(rows.reshape(3, dm.G, dm.R).transpose(1, 0, 2), ((0, 0), (0, SUBLANES - 3), (0, LANES - dm.R)))
    return par, par_all


def layer_fwd(x, p, dm, tag, next_shards=None):
    h = rms_fwd(x, p.nw, f"rms_fwd{tag}")
    gathered = None
    if next_shards is None:
        proj = mm(h, p.w_in, "nn", tm=512, tn=1920, tk=dm.D, name=f"in_proj{tag}", b_outer=True)
    else:
        proj, gathered = mm(h, p.w_in, "nn", tm=512, tn=1920, tk=dm.D, name=f"in_proj_gather{tag}", b_outer=True,
                            exchange=gather_beside(next_shards))
    xa = conv_fwd(proj, p.cw, p.cb, dm, f"conv_fwd{tag}")
    prep = ssd_prep(proj, p.par_all, dm, f"ssd_prep{tag}")
    y, hprev = ssd_fwd(xa, prep, p.par, dm, f"ssd_fwd{tag}")
    o, tot, nblk = attn_fwd(proj, dm, f"attn_fwd{tag}")
    mix = gate_fwd(y, o, proj, p.snw, dm, f"gate_fwd{tag}")
    xn = mm(mix, p.w_out, "nn", tm=512, tn=1024, tk=dm.SW + dm.AW, name=f"out_proj{tag}", res=x)
    return xn, (x, h, proj, xa, prep, y, hprev, o, tot, nblk, mix), gathered


def layer_bwd(dxn, saved, p, dm, tag, exchange=None):
    x, h, proj, xa, prep, y, hprev, o, tot, nblk, mix = saved
    dmix = mm(dxn, p.w_out, "nt", tm=512, tn=1024, tk=dm.D, name=f"d_mix{tag}", out_dtype=BF16)
    dw_out = mm(mix, dxn, "tn", tm=1024, tn=dm.D, tk=512, name=f"dw_out{tag}", out_dtype=BF16)
    dyo, dzg, dsnw = gate_bwd(dmix, y, o, proj, p.snw, dm, f"gate_bwd{tag}")
    dq, dk, dv = attn_bwd(proj, dyo, tot, nblk, dm, f"attn_bwd{tag}")
    dxs, db, dc, dac, span, ddtx, gsk = ssd_bwd(xa, prep, p.par, hprev, dyo, dm, f"ssd_bwd{tag}")
    ddt_blk, ghead = ssd_post(proj, p.par_all, dac, span, ddtx, gsk, dm, f"ssd_post{tag}")
    dcv, gconv = conv_bwd_pre(proj, dxs, db, dc, p.cw, p.cb, dm, f"conv_bwd_pre{tag}")
    dxbc = conv_bwd_in(dcv, p.cw, dm, f"conv_bwd_in{tag}")
    dproj = jnp.concatenate([dzg[:, :dm.SW], dxbc, dq, dk, dv, dzg[:, dm.SW:], ddt_blk], axis=1)
    brought = None
    if exchange is None:
        dh = mm(dproj, p.w_in, "nt", tm=512, tn=dm.D, tk=1920, name=f"d_h{tag}")
    else:
        dh, brought = mm(dproj, p.w_in, "nt", tm=512, tn=dm.D, tk=1920, name=f"d_h_exchange{tag}", exchange=exchange)
    dw_in = mm(h, dproj, "tn", tm=dm.D, tn=960, tk=2048, name=f"dw_in{tag}", out_dtype=BF16)
    dx, dnw = rms_bwd(dh, x, p.nw, dxn, f"rms_bwd{tag}")
    small = dict(norm_w=dnw[0], conv_w=gconv[:SSD_CONV], conv_b=gconv[SSD_CONV],
                 dt_bias=ghead[0, :dm.NH], a_log=ghead[1, :dm.NH], d_skip=ghead[2, :dm.NH],
                 ssd_norm_w=dsnw[0, :dm.SW])
    return dx, dw_in, dw_out, small, brought


SMALL = ("norm_w", "conv_b", "dt_bias", "a_log", "d_skip", "ssd_norm_w")


def _to_mine(w, dm):
    a, b = dm.SW + dm.CD, dm.SW + dm.CD + dm.NH
    pad = jnp.zeros((w.shape[0], LANES - dm.NH), w.dtype)
    return jnp.concatenate([w[:, :a], w[:, b:], w[:, a:b], pad], axis=1)


def _from_mine(w, dm):
    a = dm.SW + dm.CD
    return jnp.concatenate([w[:, :a], w[:, dm.dt_off:dm.dt_off + dm.NH], w[:, a:dm.dt_off]], axis=1)


def weights_to_mine(g_in, dm, name):
    _, D, ncol = g_in.shape
    tm = _pick(D, 256, 16)

    def body(g_ref, o_ref):
        full = jnp.concatenate([g_ref[j] for j in range(N_DEV)], axis=1)
        o_ref[...] = _to_mine(full, dm)

    return pl.pallas_call(
        body, name=name, grid=(D // tm,),
        in_specs=[pl.BlockSpec((N_DEV, tm, ncol), lambda i: (0, i, 0))],
        out_specs=pl.BlockSpec((tm, dm.NP), lambda i: (i, 0)), out_shape=SDS((D, dm.NP), g_in.dtype),
        compiler_params=_cp(("arbitrary",), 6 * tm * dm.NP * 2))(g_in)


def grads_from_mine(gw, dm, ncol, name):
    D = gw.shape[0]
    tm = _pick(D, 256, 16)

    def body(g_ref, o_ref):
        full = _from_mine(g_ref[...], dm)
        for j in range(N_DEV):
            o_ref[j] = full[:, j * ncol:(j + 1) * ncol]

    return pl.pallas_call(
        body, name=name, grid=(D // tm,),
        in_specs=[pl.BlockSpec((tm, dm.NP), lambda i: (i, 0))],
        out_specs=pl.BlockSpec((N_DEV, tm, ncol), lambda i: (0, i, 0)), out_shape=SDS((N_DEV, D, ncol), gw.dtype),
        compiler_params=_cp(("arbitrary",), 6 * tm * dm.NP * 2))(gw)


def _pack(pieces):
    flat = jnp.concatenate([p.reshape(-1) for p in pieces])
    rows = -(-flat.shape[0] // LANES)
    rows = -(-rows // SUBLANES) * SUBLANES
    return jnp.pad(flat, (0, rows * LANES - flat.shape[0])).reshape(rows, LANES)


def _unpack(buf, shapes):
    flat, out, at = buf.reshape(-1), [], 0
    for s in shapes:
        n = math.prod(s)
        out.append(flat[at:at + n].reshape(s))
        at += n
    return out


def kernel(x, norm_w, w_in, conv_w, conv_b, dt_bias, a_log, d_skip, ssd_norm_w, w_out, final_norm_w, loss_target, m_norm_w, m_w_in, m_conv_w, m_conv_b, m_dt_bias, m_a_log, m_d_skip, m_ssd_norm_w, m_w_out, m_final_norm_w, v_norm_w, v_w_in, v_conv_w, v_conv_b, v_dt_bias, v_a_log, v_d_skip, v_ssd_norm_w, v_w_out, v_final_norm_w):
    depth, D = norm_w.shape
    L = x.shape[1]
    NH = dt_bias.shape[1]
    SW = NH * SSD_HEAD_DIM
    CD = conv_b.shape[1]
    dm = Dims(L=L, D=D, SW=SW, G=(CD - SW) // (2 * SSD_STATE), AW=w_out.shape[1] * N_DEV - SW)
    ncol, csh, osh = w_in.shape[2], conv_w.shape[2], w_out.shape[1]
    me = 4 * lax.axis_index("x") + 2 * lax.axis_index("y") + lax.axis_index("c")

    shards = [(w_in[l].astype(BF16), w_out[l].astype(BF16), conv_w[l]) for l in range(depth)]

    def layer_params(l, gathered):
        g_in, g_out, g_cw = gathered
        full_cw = g_cw.transpose(1, 0, 2).reshape(SSD_CONV, CD)
        par, par_all = head_params(dt_bias[l], a_log[l], d_skip[l], dm)
        return LayerParams(
            nw=norm_w[l][None], w_in=weights_to_mine(g_in, dm, "weights_to_mine"), cw=full_cw, cb=conv_b[l][None],
            par=par, par_all=par_all,
            snw=ssd_norm_w[l][None], w_out=g_out.reshape(N_DEV * osh, D))

    h = x[0]
    params, saved = [], []
    gathered = gather_weights(list(shards[0]), "gather_weights")
    for l in range(depth):
        params.append(layer_params(l, gathered))
        h, s, gathered = layer_fwd(h, params[l], dm, "", shards[l + 1] if l + 1 < depth else None)
        saved.append(s)
    dh, dfw, ls = loss_head(h, final_norm_w[None], loss_target[0], "loss_head")
    loss = lax.psum(ls[0, 0], ("x", "y", "c"))
    smalls, r_in, r_out = [None] * depth, [None] * depth, [None] * depth
    pending = None
    for l in reversed(range(depth)):
        ex = None if pending is None else chip_exchange_spec(pending)
        dh, gw_in, gw_out, smalls[l], brought = layer_bwd(dh, saved[l], params[l], dm, "", ex)
        if brought is not None:
            r_in[l + 1], r_out[l + 1] = brought
        p_in = grads_from_mine(gw_in, dm, ncol, "grads_from_mine")
        p_out = gw_out.astype(BF16).reshape(N_DEV, osh, D)
        s_in, s_out = pair_exchange([p_in, p_out], "pair_exchange")
        pending = [pair_add(p_in, s_in, "pair_add_w_in"), pair_add(p_out, s_out, "pair_add_w_out")]
    grad_x = dh[None]
    rep = [jnp.stack([s[k] for s in smalls]) for k in SMALL] + [dfw[0]]
    rep_shapes = [r.shape for r in rep]
    p_rep = _pack(rep)
    p_cw = jnp.stack([s["conv_w"] for s in smalls]).reshape(depth * SSD_CONV, N_DEV, csh).transpose(1, 0, 2)
    r_in[0], r_out[0], r_rep, r_cw = chip_exchange(pending, [p_rep], [p_cw], "chip_exchange")
    r_in, r_out = jnp.concatenate(r_in, axis=1), jnp.concatenate(r_out, axis=1)

    out_in = adamw(r_in, w_in.reshape(depth * D, ncol), m_w_in.reshape(depth * D, ncol),
                   v_w_in.reshape(depth * D, ncol), "adamw_w_in")
    out_out = adamw(r_out, w_out.reshape(depth * osh, D), m_w_out.reshape(depth * osh, D),
                    v_w_out.reshape(depth * osh, D), "adamw_w_out")
    out_cw = adamw(r_cw, conv_w.reshape(depth * SSD_CONV, csh), m_conv_w.reshape(depth * SSD_CONV, csh),
                   v_conv_w.reshape(depth * SSD_CONV, csh), "adamw_conv_w")
    rep_w = [norm_w, conv_b, dt_bias, a_log, d_skip, ssd_norm_w, final_norm_w]
    rep_m = [m_norm_w, m_conv_b, m_dt_bias, m_a_log, m_d_skip, m_ssd_norm_w, m_final_norm_w]
    rep_v = [v_norm_w, v_conv_b, v_dt_bias, v_a_log, v_d_skip, v_ssd_norm_w, v_final_norm_w]
    out_rep = adamw(r_rep, _pack(rep_w), _pack(rep_m), _pack(rep_v), "adamw_replicated")

    outs = {}
    for kind, i in (("grad", 0), ("delta", 1), ("new_m", 2), ("new_v", 3)):
        r = dict(zip(SMALL + ("final_norm_w",), _unpack(out_rep[i], rep_shapes)))
        r["w_in"] = out_in[i].reshape(w_in.shape)
        r["w_out"] = out_out[i].reshape(w_out.shape)
        r["conv_w"] = out_cw[i].reshape(conv_w.shape)
        outs[kind] = r
    order = ("norm_w", "w_in", "conv_w", "conv_b", "dt_bias", "a_log", "d_skip", "ssd_norm_w", "w_out", "final_norm_w")
    return (loss, grad_x, *[outs[k][n] for k in ("grad", "delta", "new_m", "new_v") for n in order])
```

```python
import functools
import math
from typing import NamedTuple

import jax
import jax.numpy as jnp
from jax import lax
from jax.experimental import pallas as pl
from jax.experimental.pallas import tpu as pltpu

F32, BF16 = jnp.float32, jnp.bfloat16
SDS = jax.ShapeDtypeStruct
EPS = 1e-6
LANES = 128
SUBLANES = 8
VMEM_BYTES = 64 * 2 ** 20
N_DEV = 8
SSD_HEAD_DIM = 64
SSD_STATE = 128
SSD_CONV = 4
SBA_HEAD_DIM = 128
ADAM_LR, ADAM_B1, ADAM_B2, ADAM_EPS, ADAM_WD, ADAM_STEP = 0.001, 0.9, 0.999, 1e-08, 0.01, 10
SKIP_SUM = 110.0
HIGHEST = lax.Precision.HIGHEST
MESH = pl.DeviceIdType.MESH


class Dims(NamedTuple):
    L: int
    D: int
    SW: int
    G: int
    AW: int

    @property
    def NH(self): return self.SW // SSD_HEAD_DIM
    @property
    def R(self): return self.NH // self.G
    @property
    def GW(self): return self.SW // self.G
    @property
    def CD(self): return self.SW + 2 * self.G * SSD_STATE
    @property
    def AH(self): return self.AW // SBA_HEAD_DIM
    @property
    def q_off(self): return self.SW + self.CD
    @property
    def k_off(self): return self.q_off + self.AW
    @property
    def v_off(self): return self.q_off + 2 * self.AW
    @property
    def g_off(self): return self.q_off + 3 * self.AW
    @property
    def dt_off(self): return self.q_off + 4 * self.AW
    @property
    def NP(self): return self.dt_off + LANES


def _pick(n, target, mult):
    t = (min(target, n) // mult) * mult
    while t >= mult:
        if n % t == 0:
            return t
        t -= mult
    return n


def _cp(sem, vmem_est):
    limit = int(min(max(vmem_est * 5 // 4 + (4 << 20), 32 << 20), VMEM_BYTES - (8 << 20)))
    return pltpu.CompilerParams(dimension_semantics=sem, vmem_limit_bytes=limit)


def _sigmoid(x):
    return 1.0 / (1.0 + jnp.exp(-x))


def _softplus(x):
    return jnp.maximum(x, 0.0) + jnp.log(1.0 + jnp.exp(-jnp.abs(x)))


def _nbytes(shape, dtype):
    return math.prod(shape) * jnp.dtype(dtype).itemsize


_DOT_DIMS = {"nn": ((1,), (0,)), "nt": ((1,), (1,)), "tn": ((0,), (0,))}


class Exchange(NamedTuple):
    arrays: list
    out_shape: list
    n_sems: int
    n_local: int
    halves: object


def mm(a, b, mode, *, tm, tn, tk, name, res=None, b_outer=False, exchange=None, out_dtype=F32):
    if mode == "nn":
        (M, K), N = a.shape, b.shape[1]
    elif mode == "nt":
        (M, K), N = a.shape, b.shape[0]
    else:
        (K, M), N = a.shape, b.shape[1]
    tm, tn, tk = _pick(M, tm, LANES), _pick(N, tn, LANES), _pick(K, tk, LANES)
    nk = K // tk

    def ij(p0, p1):
        return (p1, p0) if b_outer else (p0, p1)

    if mode == "tn":
        a_spec = pl.BlockSpec((tk, tm), lambda p0, p1, k: (k, ij(p0, p1)[0]))
    else:
        a_spec = pl.BlockSpec((tm, tk), lambda p0, p1, k: (ij(p0, p1)[0], k))
    if mode == "nt":
        b_spec = pl.BlockSpec((tn, tk), lambda p0, p1, k: (ij(p0, p1)[1], k))
    else:
        b_spec = pl.BlockSpec((tk, tn), lambda p0, p1, k: (k, ij(p0, p1)[1]))
    o_spec = pl.BlockSpec((tm, tn), lambda p0, p1, k: ij(p0, p1))
    dims = (_DOT_DIMS[mode], ((), ()))

    grid = (N // tn, M // tm, nk) if b_outer else (M // tm, N // tn, nk)
    n_res = 0 if res is None else 1
    n_ex = 0 if exchange is None else len(exchange.arrays)

    def body(*refs):
        a_ref, b_ref, o_ref = refs[0], refs[1], refs[2 + n_res + n_ex]
        ids = [pl.program_id(d) for d in range(3)]
        if exchange is not None:
            srcs = refs[2 + n_res:2 + n_res + n_ex]
            dsts = refs[3 + n_res + n_ex:3 + n_res + 2 * n_ex]
            start, finish = exchange.halves(srcs, dsts, *refs[3 + n_res + 2 * n_ex:])

            @pl.when(jnp.logical_and(jnp.logical_and(ids[0] == 0, ids[1] == 0), ids[2] == 0))
            def _():
                start()

        part = lax.dot_general(a_ref[...].astype(BF16), b_ref[...].astype(BF16), dims,
                               preferred_element_type=F32)
        if res is not None:
            first = part + refs[2][...]
        else:
            first = part
        if nk == 1:
            o_ref[...] = first.astype(out_dtype)
        else:
            acc_ref = o_ref if out_dtype == F32 else refs[-1]

            @pl.when(ids[2] == 0)
            def _():
                acc_ref[...] = first

            @pl.when(ids[2] > 0)
            def _():
                acc_ref[...] += part

            if out_dtype != F32:
                @pl.when(ids[2] == nk - 1)
                def _():
                    o_ref[...] = acc_ref[...].astype(out_dtype)

        if exchange is not None:
            @pl.when(jnp.logical_and(jnp.logical_and(ids[0] == grid[0] - 1, ids[1] == grid[1] - 1),
                                     ids[2] == nk - 1))
            def _():
                finish()

    ins, specs = [a, b], [a_spec, b_spec]
    if res is not None:
        ins.append(res)
        specs.append(o_spec)
    est = 2 * (tm * tk * a.dtype.itemsize + tk * tn * b.dtype.itemsize + tm * tn * 4 * (2 if res is not None else 1))
    est += tm * tk * 2 + tk * tn * 2 + tm * tn * 4
    if exchange is None:
        scratch = [pltpu.VMEM((tm, tn), F32)] if (out_dtype != F32 and nk > 1) else []
        return pl.pallas_call(
            body, name=name, grid=grid, in_specs=specs, out_specs=o_spec,
            out_shape=SDS((M, N), out_dtype), scratch_shapes=scratch,
            compiler_params=_cp(("arbitrary", "arbitrary", "arbitrary"), est))(*ins)
    assert out_dtype == F32
    hbm = pl.BlockSpec(memory_space=pltpu.HBM)
    cp = _cp(("arbitrary", "arbitrary", "arbitrary"), est)
    out = pl.pallas_call(
        body, name=name, grid=grid, in_specs=specs + [hbm] * n_ex, out_specs=[o_spec] + [hbm] * n_ex,
        out_shape=[SDS((M, N), F32)] + list(exchange.out_shape),
        scratch_shapes=[pltpu.SemaphoreType.DMA((exchange.n_sems,)), pltpu.SemaphoreType.DMA((exchange.n_sems,)),
                        pltpu.SemaphoreType.DMA((exchange.n_local,))],
        compiler_params=pltpu.CompilerParams(
            dimension_semantics=cp.dimension_semantics, vmem_limit_bytes=cp.vmem_limit_bytes,
            has_side_effects=True))(*ins, *exchange.arrays)
    return out[0], out[1:]


def rms_fwd(x, nw, name):
    L, D = x.shape
    tm = _pick(L, 512, SUBLANES)

    def body(x_ref, w_ref, h_ref):
        xx = x_ref[...]
        r = lax.rsqrt(jnp.mean(xx * xx, axis=-1, keepdims=True) + EPS)
        h_ref[...] = (xx * r * w_ref[...]).astype(BF16)

    return pl.pallas_call(
        body, name=name, grid=(L // tm,),
        in_specs=[pl.BlockSpec((tm, D), lambda i: (i, 0)), pl.BlockSpec((1, D), lambda i: (0, 0))],
        out_specs=pl.BlockSpec((tm, D), lambda i: (i, 0)), out_shape=SDS((L, D), BF16),
        compiler_params=_cp(("arbitrary",), 2 * tm * D * 6))(x, nw)


def rms_bwd(dh, x, nw, dres, name):
    L, D = x.shape
    tm = _pick(L, 256, SUBLANES)

    def body(dh_ref, x_ref, w_ref, dr_ref, dx_ref, dw_ref):
        @pl.when(pl.program_id(0) == 0)
        def _():
            dw_ref[...] = jnp.zeros_like(dw_ref)

        xx, d = x_ref[...], dh_ref[...]
        r = lax.rsqrt(jnp.mean(xx * xx, axis=-1, keepdims=True) + EPS)
        xh = xx * r
        dw_ref[0:1, :] += jnp.sum(d * xh, axis=0, keepdims=True)
        dxh = d * w_ref[...]
        dx_ref[...] = dr_ref[...] + r * (dxh - xh * jnp.mean(dxh * xh, axis=-1, keepdims=True))

    row = pl.BlockSpec((tm, D), lambda i: (i, 0))
    return pl.pallas_call(
        body, name=name, grid=(L // tm,),
        in_specs=[row, row, pl.BlockSpec((1, D), lambda i: (0, 0)), row],
        out_specs=[row, pl.BlockSpec((SUBLANES, D), lambda i: (0, 0))],
        out_shape=[SDS((L, D), F32), SDS((SUBLANES, D), F32)],
        compiler_params=_cp(("arbitrary",), 2 * tm * D * 16))(dh, x, nw, dres)


def loss_head(h, fw, tgt, name):
    L, D = h.shape
    tm = _pick(L, 256, SUBLANES)

    def body(h_ref, w_ref, t_ref, dh_ref, dw_ref, ls_ref):
        @pl.when(pl.program_id(0) == 0)
        def _():
            dw_ref[...] = jnp.zeros_like(dw_ref)
            ls_ref[...] = jnp.zeros_like(ls_ref)

        xx = h_ref[...]
        r = lax.rsqrt(jnp.mean(xx * xx, axis=-1, keepdims=True) + EPS)
        xh = xx * r
        err = xh * w_ref[...] - t_ref[...]
        per_tok = jnp.mean(err * err, axis=-1, keepdims=True)
        ls_ref[...] += jnp.broadcast_to(0.5 * jnp.sum(per_tok, axis=0, keepdims=True), ls_ref.shape)
        dy = err * (1.0 / D)
        dw_ref[0:1, :] += jnp.sum(dy * xh, axis=0, keepdims=True)
        dxh = dy * w_ref[...]
        dh_ref[...] = r * (dxh - xh * jnp.mean(dxh * xh, axis=-1, keepdims=True))

    row = pl.BlockSpec((tm, D), lambda i: (i, 0))
    return pl.pallas_call(
        body, name=name, grid=(L // tm,),
        in_specs=[row, pl.BlockSpec((1, D), lambda i: (0, 0)), row],
        out_specs=[row, pl.BlockSpec((SUBLANES, D), lambda i: (0, 0)),
                   pl.BlockSpec((SUBLANES, LANES), lambda i: (0, 0))],
        out_shape=[SDS((L, D), F32), SDS((SUBLANES, D), F32), SDS((SUBLANES, LANES), F32)],
        compiler_params=_cp(("arbitrary",), 2 * tm * D * 12))(h, fw, tgt)


def _shifted(u, edge, s, back):
    n = u.shape[0]
    row = lax.broadcasted_iota(jnp.int32, (SUBLANES, u.shape[1]), 0)
    if back:
        r = pltpu.roll(u, s, 0)
        head = jnp.where(row < s, pltpu.roll(edge, s, 0), r[0:SUBLANES])
        return jnp.concatenate([head, r[SUBLANES:]], axis=0)
    r = pltpu.roll(u, n - s, 0)
    tail = jnp.where(row >= SUBLANES - s, pltpu.roll(edge, SUBLANES - s, 0), r[n - SUBLANES:])
    return jnp.concatenate([r[:n - SUBLANES], tail], axis=0)


def _conv_pre(u, prev, w, b):
    acc = b + w[SSD_CONV - 1:SSD_CONV] * u
    taps = [u]
    for s in range(1, SSD_CONV):
        us = _shifted(u, prev, s, True)
        taps.append(us)
        acc = acc + w[SSD_CONV - 1 - s:SSD_CONV - s] * us
    return acc, taps


def _conv_specs(dm, tm, tc, col0):
    rb = tm // SUBLANES
    u_spec = pl.BlockSpec((tm, tc), lambda j, i: (i, col0 + j))
    prev_spec = pl.BlockSpec((SUBLANES, tc), lambda j, i: (jnp.maximum(i * rb - 1, 0), col0 + j))
    w_spec = pl.BlockSpec((SSD_CONV, tc), lambda j, i: (0, j))
    b_spec = pl.BlockSpec((1, tc), lambda j, i: (0, j))
    return u_spec, prev_spec, w_spec, b_spec


def conv_fwd(proj, cw, cb, dm, name):
    L, CD = dm.L, dm.CD
    tm, tc = _pick(L, 1024, SUBLANES), _pick(math.gcd(CD, dm.SW), 512, LANES)
    u_spec, prev_spec, w_spec, b_spec = _conv_specs(dm, tm, tc, dm.SW // tc)

    def body(u_ref, p_ref, w_ref, b_ref, o_ref):
        prev = jnp.where(pl.program_id(1) == 0, 0.0, p_ref[...])
        c, _ = _conv_pre(u_ref[...], prev, w_ref[...], b_ref[...])
        o_ref[...] = c * _sigmoid(c)

    return pl.pallas_call(
        body, name=name, grid=(CD // tc, L // tm),
        in_specs=[u_spec, prev_spec, w_spec, b_spec],
        out_specs=pl.BlockSpec((tm, tc), lambda j, i: (i, j)), out_shape=SDS((L, CD), F32),
        compiler_params=_cp(("arbitrary", "arbitrary"), 12 * tm * tc * 4))(proj, proj, cw, cb)


def conv_bwd_pre(proj, dxs, db, dcm, cw, cb, dm, name):
    L, CD = dm.L, dm.CD
    gn = dm.G * SSD_STATE
    tm, tc = _pick(L, 1024, SUBLANES), _pick(math.gcd(gn, dm.SW), 512, LANES)
    u_spec, prev_spec, w_spec, b_spec = _conv_specs(dm, tm, tc, dm.SW // tc)
    nx, nb = dm.SW // tc, gn // tc

    def body(u_ref, p_ref, dx_ref, db_ref, dcm_ref, w_ref, b_ref, dc_ref, g_ref):
        j = pl.program_id(0)

        @pl.when(pl.program_id(1) == 0)
        def _():
            g_ref[...] = jnp.zeros_like(g_ref)

        prev = jnp.where(pl.program_id(1) == 0, 0.0, p_ref[...])
        c, taps = _conv_pre(u_ref[...], prev, w_ref[...], b_ref[...])
        sg = _sigmoid(c)
        d = jnp.where(j < nx, dx_ref[...], jnp.where(j < nx + nb, db_ref[...], dcm_ref[...]))
        dc = d * (sg * (1.0 + c * (1.0 - sg)))
        dc_ref[...] = dc
        for s in range(SSD_CONV):
            g_ref[SSD_CONV - 1 - s:SSD_CONV - s, :] += jnp.sum(dc * taps[s], axis=0, keepdims=True)
        g_ref[SSD_CONV:SSD_CONV + 1, :] += jnp.sum(dc, axis=0, keepdims=True)

    blk = pl.BlockSpec((tm, tc), lambda j, i: (i, j))
    part = lambda lo, n: pl.BlockSpec((tm, tc), lambda j, i: (i, jnp.clip(j - lo, 0, n - 1)))
    return pl.pallas_call(
        body, name=name, grid=(CD // tc, L // tm),
        in_specs=[u_spec, prev_spec, part(0, nx), part(nx, nb), part(nx + nb, nb), w_spec, b_spec],
        out_specs=[blk, pl.BlockSpec((SUBLANES, tc), lambda j, i: (0, j))],
        out_shape=[SDS((L, CD), F32), SDS((SUBLANES, CD), F32)],
        compiler_params=_cp(("arbitrary", "arbitrary"), 20 * tm * tc * 4))(proj, proj, dxs, db, dcm, cw, cb)


def conv_bwd_in(dc, cw, dm, name):
    L, CD = dm.L, dm.CD
    tm, tc = _pick(L, 1024, SUBLANES), _pick(CD, 512, LANES)
    rb, nrow = tm // SUBLANES, L // SUBLANES
    ni = L // tm

    def body(d_ref, n_ref, w_ref, o_ref):
        nxt = jnp.where(pl.program_id(1) == ni - 1, 0.0, n_ref[...])
        dc_, w = d_ref[...], w_ref[...]
        acc = w[SSD_CONV - 1:SSD_CONV] * dc_
        for s in range(1, SSD_CONV):
            acc = acc + w[SSD_CONV - 1 - s:SSD_CONV - s] * _shifted(dc_, nxt, s, False)
        o_ref[...] = acc.astype(BF16)

    blk = pl.BlockSpec((tm, tc), lambda j, i: (i, j))
    return pl.pallas_call(
        body, name=name, grid=(CD // tc, ni),
        in_specs=[blk, pl.BlockSpec((SUBLANES, tc), lambda j, i: (jnp.minimum((i + 1) * rb, nrow - 1), j)),
                  pl.BlockSpec((SSD_CONV, tc), lambda j, i: (0, j))],
        out_specs=blk, out_shape=SDS((L, CD), BF16),
        compiler_params=_cp(("arbitrary", "arbitrary"), 10 * tm * tc * 4))(dc, dc, cw)


def _ssd_chunk(dm):
    return _pick(dm.L, 512, LANES)


def _dt_parts(dtr, par):
    return _softplus(dtr + par[0:1]), -jnp.exp(par[1:2])


def _tri(Q):
    rows = lax.broadcasted_iota(jnp.int32, (Q, Q), 0)
    cols = lax.broadcasted_iota(jnp.int32, (Q, Q), 1)
    return rows, cols


def _lanes_to_group(v, g, R, axis):
    n = v.shape[axis]
    return v if g == 0 else pltpu.roll(v, n - g * R, axis)


def ssd_prep(proj, par_all, dm, name):
    Q, G, R = _ssd_chunk(dm), dm.G, dm.R
    nc = dm.L // Q

    def body(dtr_ref, par_ref, dt_ref, ac_ref, at_ref):
        dt, a = _dt_parts(dtr_ref[...], par_ref[...])
        da = dt * a
        rows, cols = _tri(Q)
        acum = jnp.dot((rows >= cols).astype(F32), da, precision=HIGHEST, preferred_element_type=F32)
        acum_t = lax.dot_general(da, (rows <= cols).astype(F32), (_DOT_DIMS["tn"], ((), ())),
                                 precision=HIGHEST, preferred_element_type=F32)
        for g in range(G):
            dt_ref[g] = _lanes_to_group(dt, g, R, 1)
            ac_ref[g] = _lanes_to_group(acum, g, R, 1)
            at_ref[g, 0] = _lanes_to_group(acum_t, g, R, 0)[0:SUBLANES]

    lane_blk = pl.BlockSpec((G, Q, LANES), lambda c: (0, c, 0))
    return pl.pallas_call(
        body, name=name, grid=(nc,),
        in_specs=[pl.BlockSpec((Q, LANES), lambda c: (c, dm.dt_off // LANES)),
                  pl.BlockSpec((SUBLANES, LANES), lambda c: (0, 0))],
        out_specs=[lane_blk, lane_blk, pl.BlockSpec((G, 1, SUBLANES, Q), lambda c: (0, c, 0, 0))],
        out_shape=[SDS((G, dm.L, LANES), F32), SDS((G, dm.L, LANES), F32), SDS((G, nc, SUBLANES, Q), F32)],
        compiler_params=_cp(("arbitrary",), 8 << 20))(proj, par_all)


def _ssd_common(xa_refs, Q):
    _, b_ref, c_ref = xa_refs
    rows, cols = _tri(Q)
    bm, cm = b_ref[...].astype(BF16), c_ref[...].astype(BF16)
    gm = lax.dot_general(cm, bm, (_DOT_DIMS["nt"], ((), ())), preferred_element_type=F32)
    return rows >= cols, bm, cm, gm


def _ssd_head(r, Q, x, dt, tri, acum, acum_t, gm):
    P = SSD_HEAD_DIM
    col = jnp.broadcast_to(acum[:, r:r + 1], (Q, Q))
    row = acum_t[r:r + 1, :]
    lam = jnp.where(tri, jnp.exp(jnp.minimum(col - row, 0.0)), 0.0)
    m = gm * lam
    xh = x[:, r * P:(r + 1) * P]
    xs = xh * dt[:, r:r + 1]
    a_last = acum_t[r:r + 1, Q - 1:Q]
    return col[:, :P], lam, m, xh, xs, a_last


def _ssd_specs(dm, Q, rev):
    nc = dm.L // Q
    cc = (lambda c: nc - 1 - c) if rev else (lambda c: c)
    nb = dm.SW // SSD_STATE
    x_spec = pl.BlockSpec((Q, dm.GW), lambda g, c: (cc(c), g))
    b_spec = pl.BlockSpec((Q, SSD_STATE), lambda g, c: (cc(c), nb + g))
    c_spec = pl.BlockSpec((Q, SSD_STATE), lambda g, c: (cc(c), nb + dm.G + g))
    dt_spec = pl.BlockSpec((1, Q, LANES), lambda g, c: (g, cc(c), 0))
    at_spec = pl.BlockSpec((1, 1, SUBLANES, Q), lambda g, c: (g, cc(c), 0, 0))
    par_spec = pl.BlockSpec((1, SUBLANES, LANES), lambda g, c: (g, 0, 0))
    h_spec = pl.BlockSpec((1, 1, dm.GW, SSD_STATE), lambda g, c: (cc(c), g, 0, 0))
    return x_spec, b_spec, c_spec, dt_spec, at_spec, par_spec, h_spec


def ssd_fwd(xa, prep, par, dm, name):
    Q = _ssd_chunk(dm)
    nc, P, R = dm.L // Q, SSD_HEAD_DIM, dm.R
    x_spec, b_spec, c_spec, dt_spec, at_spec, par_spec, h_spec = _ssd_specs(dm, Q, False)

    def body(x_ref, b_ref, c_ref, dt_ref, ac_ref, at_ref, par_ref, y_ref, hp_ref, h_scr):
        @pl.when(pl.program_id(1) == 0)
        def _():
            h_scr[...] = jnp.zeros_like(h_scr)

        tri, bm, cm, gm = _ssd_common((x_ref, b_ref, c_ref), Q)
        par, dt, acum, acum_t = par_ref[0], dt_ref[0], ac_ref[0], at_ref[0, 0]
        x = x_ref[...]
        hp = h_scr[...]
        hp_ref[0, 0] = hp
        for r in range(R):
            col, lam, m, xh, xs, a_last = _ssd_head(r, Q, x, dt, tri, acum, acum_t, gm)
            hpr = hp[r * P:(r + 1) * P, :]
            ydiag = jnp.dot(m.astype(BF16), xs.astype(BF16), preferred_element_type=F32)
            yoff = jnp.exp(col) * lax.dot_general(cm, hpr.astype(BF16), (_DOT_DIMS["nt"], ((), ())),
                                                  preferred_element_type=F32)
            dte = jnp.exp(a_last - col)
            st = lax.dot_general((xs * dte).astype(BF16), bm, (_DOT_DIMS["tn"], ((), ())),
                                 preferred_element_type=F32)
            h_scr[r * P:(r + 1) * P, :] = jnp.exp(a_last) * hpr + st
            y_ref[:, r * P:(r + 1) * P] = ydiag + yoff + par[2:3, r:r + 1] * xh

    return pl.pallas_call(
        body, name=name, grid=(dm.G, nc),
        in_specs=[x_spec, b_spec, c_spec, dt_spec, dt_spec, at_spec, par_spec],
        out_specs=[x_spec, h_spec],
        out_shape=[SDS((dm.L, dm.SW), F32), SDS((nc, dm.G, dm.GW, SSD_STATE), F32)],
        scratch_shapes=[pltpu.VMEM((dm.GW, SSD_STATE), F32)],
        compiler_params=_cp(("arbitrary", "arbitrary"), 24 << 20))(xa, xa, xa, *prep, par)


def ssd_bwd(xa, prep, par, hprev, dyo, dm, name):
    Q = _ssd_chunk(dm)
    nc, P, R, N = dm.L // Q, SSD_HEAD_DIM, dm.R, SSD_STATE
    x_spec, b_spec, c_spec, dt_spec, at_spec, par_spec, h_spec = _ssd_specs(dm, Q, True)
    bc_spec = pl.BlockSpec((Q, N), lambda g, c: (nc - 1 - c, g))
    NT, TN = (_DOT_DIMS["nt"], ((), ())), (_DOT_DIMS["tn"], ((), ()))

    def body(x_ref, b_ref, c_ref, dt_ref, ac_ref, at_ref, par_ref, hp_ref, dy_ref,
             dx_ref, db_ref, dc_ref, dac_ref, span_ref, ddt_ref, gs_ref, dh_scr):
        @pl.when(pl.program_id(1) == 0)
        def _():
            dh_scr[...] = jnp.zeros_like(dh_scr)
            gs_ref[...] = jnp.zeros_like(gs_ref)

        tri, bm, cm, gm = _ssd_common((x_ref, b_ref, c_ref), Q)
        par, dt, acum, acum_t = par_ref[0], dt_ref[0], ac_ref[0], at_ref[0, 0]
        x, dy, hp, dhn = x_ref[...], dy_ref[...].astype(F32), hp_ref[0, 0], dh_scr[...]
        lane = lax.broadcasted_iota(jnp.int32, (Q, LANES), 1)
        rowi = lax.broadcasted_iota(jnp.int32, (Q, LANES), 0)
        d_acum = jnp.zeros((Q, LANES), F32)
        d_dt = jnp.zeros((Q, LANES), F32)
        d_skip = jnp.zeros((1, LANES), F32)
        dgsum = jnp.zeros((Q, Q), F32)
        dye_all, xse_all, span_rows, da_rows, ddt_rows = [], [], [], [], []
        rq = lax.broadcasted_iota(jnp.int32, (Q, Q), 0)
        cq = lax.broadcasted_iota(jnp.int32, (Q, Q), 1)
        ue = (rq < cq).astype(BF16)
        for r in range(R):
            col, lam, m, xh, xs, a_last = _ssd_head(r, Q, x, dt, tri, acum, acum_t, gm)
            sl = slice(r * P, (r + 1) * P)
            dyh, hpr, dhr = dy[:, sl], hp[sl, :], dhn[sl, :]
            mb, xsb, dyb = m.astype(BF16), xs.astype(BF16), dyh.astype(BF16)
            e_a = jnp.exp(col)
            dte = jnp.exp(a_last - col)
            yoff = e_a * lax.dot_general(cm, hpr.astype(BF16), NT, preferred_element_type=F32)
            bdh = dte * lax.dot_general(bm, dhr.astype(BF16), NT, preferred_element_type=F32)
            dxs = lax.dot_general(mb, dyb, TN, preferred_element_type=F32) + bdh
            dm_ = lax.dot_general(dyb, xsb, NT, preferred_element_type=F32)
            dgsum = dgsum + dm_ * lam
            dye, xse = (dyh * e_a).astype(BF16), (xs * dte).astype(BF16)
            dye_all.append(dye)
            xse_all.append(xse)
            dh_scr[sl, :] = (jnp.exp(a_last) * dhr
                             + lax.dot_general(dye, cm, TN, preferred_element_type=F32))
            before = jnp.dot((dm_ * m).astype(BF16), ue, preferred_element_type=F32)
            span_rows.append(jnp.sum(jnp.where(tri, before, 0.0), axis=0, keepdims=True))
            da_rows.append(dyh * yoff - xs * bdh)
            ddt_rows.append(dxs * xh)
            da_last = (jnp.sum(xs * bdh, axis=(0, 1), keepdims=True)
                       + jnp.exp(a_last) * jnp.sum(dhr * hpr, axis=(0, 1), keepdims=True))
            d_acum = d_acum + jnp.where(jnp.logical_and(lane == r, rowi == Q - 1), da_last, 0.0)
            d_skip = d_skip + jnp.where(lane[0:1] == r, jnp.sum(dyh * xh, axis=(0, 1), keepdims=True), 0.0)
            dx_ref[:, sl] = dxs * dt[:, r:r + 1] + par[2:3, r:r + 1] * dyh
        sel = (lax.broadcasted_iota(jnp.int32, (R * P, LANES), 0) // P
               == lax.broadcasted_iota(jnp.int32, (R * P, LANES), 1)).astype(BF16)

        def head_sums(rows_):
            v = jnp.concatenate(rows_, axis=1)
            hi = v.astype(BF16)
            lo = (v - hi.astype(F32)).astype(BF16)
            return (jnp.dot(hi, sel, preferred_element_type=F32)
                    + jnp.dot(lo, sel, preferred_element_type=F32))

        d_acum = d_acum + head_sums(da_rows)
        d_dt = d_dt + head_sums(ddt_rows)
        dgb = dgsum.astype(BF16)
        dye_c = jnp.concatenate(dye_all, axis=1)
        xse_c = jnp.concatenate(xse_all, axis=1)
        dc_ref[...] = (jnp.dot(dgb, bm, preferred_element_type=F32)
                       + jnp.dot(dye_c, hp.astype(BF16), preferred_element_type=F32))
        db_ref[...] = (lax.dot_general(dgb, cm, TN, preferred_element_type=F32)
                       + jnp.dot(xse_c, dhn.astype(BF16), preferred_element_type=F32))
        dac_ref[0] = d_acum
        ddt_ref[0] = d_dt
        span_ref[0, 0] = jnp.concatenate(span_rows + [jnp.zeros((SUBLANES - R, Q), F32)] * (R < SUBLANES), axis=0)
        gs_ref[0, 2:3, :] += d_skip

    return pl.pallas_call(
        body, name=name, grid=(dm.G, nc),
        in_specs=[x_spec, b_spec, c_spec, dt_spec, dt_spec, at_spec, par_spec, h_spec, x_spec],
        out_specs=[x_spec, bc_spec, bc_spec, dt_spec, at_spec, dt_spec, par_spec],
        out_shape=[SDS((dm.L, dm.SW), F32), SDS((dm.L, dm.G * N), F32), SDS((dm.L, dm.G * N), F32),
                   SDS((dm.G, dm.L, LANES), F32), SDS((dm.G, nc, SUBLANES, Q), F32),
                   SDS((dm.G, dm.L, LANES), F32), SDS((dm.G, SUBLANES, LANES), F32)],
        scratch_shapes=[pltpu.VMEM((dm.GW, N), F32)],
        compiler_params=_cp(("arbitrary", "arbitrary"), 28 << 20))(xa, xa, xa, *prep, par, hprev, dyo)


def ssd_post(proj, par_all, dac, span, ddt, gs, dm, name):
    Q, G, R = _ssd_chunk(dm), dm.G, dm.R
    nc = dm.L // Q
    NT = (_DOT_DIMS["nt"], ((), ()))

    def body(dtr_ref, par_ref, dac_ref, span_ref, ddt_ref, gs_ref, out_ref, acc_ref):
        @pl.when(pl.program_id(0) == 0)
        def _():
            acc_ref[...] = jnp.zeros_like(acc_ref)

        par = par_ref[...]
        dt, a = _dt_parts(dtr_ref[...], par)
        lane = lax.broadcasted_iota(jnp.int32, (Q, LANES), 1)
        row8 = lax.broadcasted_iota(jnp.int32, (SUBLANES, Q), 0)

        def heads(v, g):
            v = jnp.where(lane[:v.shape[0]] < R, v, 0.0)
            return v if g == 0 else pltpu.roll(v, g * R, 1)

        d_acum = sum(heads(dac_ref[g], g) for g in range(G))
        d_dtx = sum(heads(ddt_ref[g], g) for g in range(G))
        d_skip = sum(heads(gs_ref[g][2:3], g) for g in range(G))
        span_t = jnp.zeros((LANES, Q), F32)
        for g in range(G):
            rows_g = jnp.concatenate([jnp.where(row8 < R, span_ref[g, 0], 0.0),
                                      jnp.zeros((LANES - SUBLANES, Q), F32)], axis=0)
            span_t = span_t + (rows_g if g == 0 else pltpu.roll(rows_g, g * R, 0))
        rq, cq = _tri(Q)
        d_da = (jnp.dot((rq <= cq).astype(F32), d_acum, precision=HIGHEST, preferred_element_type=F32)
                + lax.dot_general((rq == cq).astype(F32), span_t, NT, precision=HIGHEST,
                                  preferred_element_type=F32))
        d_raw = (d_dtx + d_da * a) * _sigmoid(dtr_ref[...] + par[0:1])
        out_ref[...] = d_raw.astype(BF16)
        acc_ref[0:1, :] += jnp.sum(d_raw, axis=0, keepdims=True)
        acc_ref[1:2, :] += jnp.sum(d_da * dt, axis=0, keepdims=True) * a
        acc_ref[2:3, :] = d_skip

    lane_blk = pl.BlockSpec((G, Q, LANES), lambda c: (0, c, 0))
    small = pl.BlockSpec((SUBLANES, LANES), lambda c: (0, 0))
    return pl.pallas_call(
        body, name=name, grid=(nc,),
        in_specs=[pl.BlockSpec((Q, LANES), lambda c: (c, dm.dt_off // LANES)), small, lane_blk,
                  pl.BlockSpec((G, 1, SUBLANES, Q), lambda c: (0, c, 0, 0)), lane_blk,
                  pl.BlockSpec((G, SUBLANES, LANES), lambda c: (0, 0, 0))],
        out_specs=[pl.BlockSpec((Q, LANES), lambda c: (c, 0)), small],
        out_shape=[SDS((dm.L, LANES), BF16), SDS((SUBLANES, LANES), F32)],
        compiler_params=_cp(("arbitrary",), 8 << 20))(proj, par_all, dac, span, ddt, gs)


def _attn_tile(dm):
    return _pick(dm.L, 256, LANES)


def attn_fwd(proj, dm, name):
    L, T, DH, AH = dm.L, _attn_tile(dm), SBA_HEAD_DIM, dm.AH
    nq = L // T
    scale = 1.0 / math.sqrt(DH)
    NT = (_DOT_DIMS["nt"], ((), ()))

    def body(q_ref, k_ref, v_ref, o_ref, tot_ref, nb_ref, ks, vs, o_scr, acc_scr):
        h, i = pl.program_id(0), pl.program_id(1)

        @pl.when(i == 0)
        def _():
            ks[...] = k_ref[...].astype(BF16)
            vs[...] = v_ref[...].astype(BF16)

        qb = q_ref[...].astype(BF16)
        rows = lax.broadcasted_iota(jnp.int32, (T, T), 0)
        cols = lax.broadcasted_iota(jnp.int32, (T, T), 1)
        causal = cols < rows
        u_rev = (rows >= cols).astype(BF16)

        def scores(j, masked):
            sl = pl.ds(pl.multiple_of(j * T, T), T)
            z = lax.dot_general(qb, ks[sl, :], NT, preferred_element_type=F32) * scale
            sp = _softplus(z)
            if masked:
                sp = jnp.where(causal, sp, 0.0)
            cs = jnp.dot(sp.astype(BF16), u_rev, preferred_element_type=F32)
            return sl, z, cs

        def weighted(blk, acc, masked):
            sl, z, cs = blk
            w = jnp.exp(z - cs - acc)
            if masked:
                w = jnp.where(causal, w, 0.0)
            return jnp.dot(w.astype(BF16), vs[sl, :], preferred_element_type=F32), acc + cs[:, 0:1]

        zero = jnp.zeros((T, 1), F32)

        @pl.when(i == 0)
        def _():
            o_scr[...], acc_scr[...] = weighted(scores(i, True), zero, True)

        @pl.when(i > 0)
        def _():
            diag, prev = scores(i, True), scores(i - 1, False)
            pv0, acc1 = weighted(diag, zero, True)
            pv1, acc2 = weighted(prev, acc1, False)
            o_scr[...] = pv0 + pv1
            acc_scr[...] = acc2

        def cond(c):
            return jnp.logical_and(c[0] >= 0, c[1] < SKIP_SUM)

        def loop(c):
            pv, acc = weighted(scores(c[0], False), acc_scr[...], False)
            o_scr[...] += pv
            acc_scr[...] = acc
            return c[0] - 1, jnp.min(acc)

        j_end, _ = lax.while_loop(cond, loop, (jnp.where(i > 0, i - 2, -1), jnp.min(acc_scr[...])))
        o_ref[...] = o_scr[...]
        tot_ref[0] = jnp.broadcast_to(acc_scr[...], (T, LANES))
        nb_ref[h, i] = i - j_end

    kv = lambda off: pl.BlockSpec((L, DH), lambda h, i: (0, off // DH + h))
    est = 2 * 2 * L * DH * 4 + 2 * L * DH * 2 + 12 * T * T * 4
    return pl.pallas_call(
        body, name=name, grid=(AH, nq),
        in_specs=[pl.BlockSpec((T, DH), lambda h, i: (i, dm.q_off // DH + h)), kv(dm.k_off), kv(dm.v_off)],
        out_specs=[pl.BlockSpec((T, DH), lambda h, i: (i, h)),
                   pl.BlockSpec((1, T, LANES), lambda h, i: (h, i, 0)),
                   pl.BlockSpec(memory_space=pltpu.SMEM)],
        out_shape=[SDS((L, dm.AW), F32), SDS((AH, L, LANES), F32), SDS((AH, nq), jnp.int32)],
        scratch_shapes=[pltpu.VMEM((L, DH), BF16), pltpu.VMEM((L, DH), BF16),
                        pltpu.VMEM((T, DH), F32), pltpu.VMEM((T, 1), F32)],
        compiler_params=_cp(("arbitrary", "arbitrary"), est))(proj, proj, proj)


def attn_bwd(proj, dyo, tot, nblk, dm, name):
    L, T, DH, AH = dm.L, _attn_tile(dm), SBA_HEAD_DIM, dm.AH
    nq = L // T
    scale = 1.0 / math.sqrt(DH)
    NT, TN = (_DOT_DIMS["nt"], ((), ())), (_DOT_DIMS["tn"], ((), ()))

    def body(nb_ref, q_ref, k_ref, v_ref, do_ref, tot_ref, dq_ref, dk_out, dv_out,
             ks, vs, dq_scr, p_scr, e_scr, dk_ref, dv_ref):
        h, i = pl.program_id(0), pl.program_id(1)

        @pl.when(i == 0)
        def _():
            ks[...] = k_ref[...].astype(BF16)
            vs[...] = v_ref[...].astype(BF16)
            dk_ref[...] = jnp.zeros_like(dk_ref)
            dv_ref[...] = jnp.zeros_like(dv_ref)

        qb = q_ref[...].astype(BF16)
        dob = do_ref[...].astype(BF16)
        tot_c = tot_ref[0][:, 0:1]
        rows = lax.broadcasted_iota(jnp.int32, (T, T), 0)
        cols = lax.broadcasted_iota(jnp.int32, (T, T), 1)
        causal = cols < rows
        u_fwd = (rows <= cols).astype(BF16)
        dq_scr[...] = jnp.zeros_like(dq_scr)
        p_scr[...] = jnp.zeros_like(p_scr)
        e_scr[...] = jnp.zeros_like(e_scr)

        def blocks(js, masks, before, e_before):
            sls = [pl.ds(pl.multiple_of(j * T, T), T) for j in js]
            zs = [lax.dot_general(qb, ks[sl, :], NT, preferred_element_type=F32) * scale for sl in sls]
            sps = [_softplus(z) for z in zs]
            sps = [jnp.where(causal, sp, 0.0) if m else sp for sp, m in zip(sps, masks)]
            spbs = [sp.astype(BF16) for sp in sps]
            pins = [jnp.dot(spb, u_fwd, preferred_element_type=F32) for spb in spbs]
            dws = [lax.dot_general(dob, vs[sl, :], NT, preferred_element_type=F32) for sl in sls]
            ws = []
            for z, spb, pin, m in zip(zs, spbs, pins, masks):
                cs = (tot_c - before) - (pin - spb.astype(F32))
                w = jnp.exp(z - cs)
                ws.append(jnp.where(causal, w, 0.0) if m else w)
                before = before + pin[:, T - 1:T]
            es = [dw * w for dw, w in zip(dws, ws)]
            fins = [jnp.dot(e.astype(BF16), u_fwd, preferred_element_type=F32) for e in es]
            dzbs = []
            for z, sp, e, fin, m in zip(zs, sps, es, fins, masks):
                dz = (e - jnp.exp(z - sp) * (e_before + fin)) * scale
                dzbs.append((jnp.where(causal, dz, 0.0) if m else dz).astype(BF16))
                e_before = e_before + fin[:, T - 1:T]
            dqs = [jnp.dot(dzb, ks[sl, :], preferred_element_type=F32) for dzb, sl in zip(dzbs, sls)]
            dks = [lax.dot_general(dzb, qb, TN, preferred_element_type=F32) for dzb in dzbs]
            dvs = [lax.dot_general(w.astype(BF16), dob, TN, preferred_element_type=F32) for w in ws]
            for sl, dk, dv in zip(sls, dks, dvs):
                dk_ref[sl, :] += dk
                dv_ref[sl, :] += dv
            return sum(dqs[1:], dqs[0]), before, e_before

        def loop(j, carry):
            dq, p_scr[...], e_scr[...] = blocks([j], [False], p_scr[...], e_scr[...])
            dq_scr[...] += dq
            return carry

        lax.fori_loop(i - nb_ref[h, i] + 1, i - 1, loop, 0)

        @pl.when(i == 0)
        def _():
            dq, _, _ = blocks([i], [True], p_scr[...], e_scr[...])
            dq_ref[...] = (dq_scr[...] + dq).astype(BF16)

        @pl.when(i > 0)
        def _():
            dq, _, _ = blocks([i - 1, i], [False, True], p_scr[...], e_scr[...])
            dq_ref[...] = (dq_scr[...] + dq).astype(BF16)

        @pl.when(i == nq - 1)
        def _():
            dk_out[...] = dk_ref[...].astype(BF16)
            dv_out[...] = dv_ref[...].astype(BF16)

    kv = lambda off: pl.BlockSpec((L, DH), lambda h, i, nb: (0, off // DH + h))
    qblk = lambda off: pl.BlockSpec((T, DH), lambda h, i, nb: (i, off // DH + h))
    acc = pl.BlockSpec((L, DH), lambda h, i, nb: (0, h))
    est = 2 * 2 * L * DH * 4 * 2 + 2 * L * DH * 2 + 16 * T * T * 4
    grid_spec = pltpu.PrefetchScalarGridSpec(
        num_scalar_prefetch=1, grid=(AH, nq),
        in_specs=[qblk(dm.q_off), kv(dm.k_off), kv(dm.v_off), qblk(dm.SW),
                  pl.BlockSpec((1, T, LANES), lambda h, i, nb: (h, i, 0))],
        out_specs=[qblk(0), acc, acc],
        scratch_shapes=[pltpu.VMEM((L, DH), BF16), pltpu.VMEM((L, DH), BF16),
                        pltpu.VMEM((T, DH), F32), pltpu.VMEM((T, 1), F32), pltpu.VMEM((T, 1), F32),
                        pltpu.VMEM((L, DH), F32), pltpu.VMEM((L, DH), F32)])
    return pl.pallas_call(
        body, name=name, grid_spec=grid_spec,
        out_shape=[SDS((L, dm.AW), BF16)] * 3,
        compiler_params=_cp(("arbitrary", "arbitrary"), est))(nblk, proj, proj, proj, dyo, tot)


def _gate_specs(dm, tm, order):
    GW, G = dm.GW, dm.G
    ix = (lambda a, b: (a, b)) if order == "ij" else (lambda a, b: (b, a))

    def spec(colfn):
        return pl.BlockSpec((tm, GW), lambda p0, p1: (ix(p0, p1)[0], colfn(ix(p0, p1)[1])))

    y_spec = spec(lambda j: jnp.minimum(j, G - 1))
    o_spec = spec(lambda j: jnp.maximum(j - G, 0))
    zg_spec = spec(lambda j: jnp.where(j < G, j, dm.g_off // GW + j - G))
    w_spec = pl.BlockSpec((1, GW), lambda p0, p1: (0, jnp.minimum(ix(p0, p1)[1], G - 1)))
    full = spec(lambda j: j)
    return y_spec, o_spec, zg_spec, w_spec, full


def gate_fwd(y, o, proj, snw, dm, name):
    L, GW, G = dm.L, dm.GW, dm.G
    tm = _pick(L, 1024, SUBLANES)
    ncol = (dm.SW + dm.AW) // GW
    y_spec, o_spec, zg_spec, w_spec, full = _gate_specs(dm, tm, "ij")

    def body(y_ref, o_ref, zg_ref, w_ref, m_ref):
        j = pl.program_id(1)
        zg = zg_ref[...]
        gate = zg * _sigmoid(zg)

        @pl.when(j < G)
        def _():
            yz = y_ref[...] * gate
            r = lax.rsqrt(jnp.mean(yz * yz, axis=-1, keepdims=True) + EPS)
            m_ref[...] = (yz * r * w_ref[...]).astype(BF16)

        @pl.when(j >= G)
        def _():
            m_ref[...] = (o_ref[...] * gate).astype(BF16)

    return pl.pallas_call(
        body, name=name, grid=(L // tm, ncol),
        in_specs=[y_spec, o_spec, zg_spec, w_spec], out_specs=full,
        out_shape=SDS((L, dm.SW + dm.AW), BF16),
        compiler_params=_cp(("arbitrary", "arbitrary"), 2 * tm * GW * 16))(y, o, proj, snw)


def gate_bwd(dmix, y, o, proj, snw, dm, name):
    L, GW, G = dm.L, dm.GW, dm.G
    tm = _pick(L, 1024, SUBLANES)
    W = dm.SW + dm.AW
    y_spec, o_spec, zg_spec, w_spec, full = _gate_specs(dm, tm, "ji")

    def body(d_ref, y_ref, o_ref, zg_ref, w_ref, dyo_ref, dzg_ref, dw_ref):
        j = pl.program_id(0)

        @pl.when(pl.program_id(1) == 0)
        def _():
            dw_ref[...] = jnp.zeros_like(dw_ref)

        zg, d = zg_ref[...], d_ref[...].astype(F32)
        sg = _sigmoid(zg)
        gate = zg * sg
        dgate = sg * (1.0 + zg * (1.0 - sg))

        @pl.when(j < G)
        def _():
            yv = y_ref[...]
            yz = yv * gate
            r = lax.rsqrt(jnp.mean(yz * yz, axis=-1, keepdims=True) + EPS)
            nrm = yz * r
            dw_ref[0:1, :] += jnp.sum(d * nrm, axis=0, keepdims=True)
            dn = d * w_ref[...]
            dyz = r * (dn - nrm * jnp.mean(dn * nrm, axis=-1, keepdims=True))
            dyo_ref[...] = (dyz * gate).astype(BF16)
            dzg_ref[...] = (dyz * yv * dgate).astype(BF16)

        @pl.when(j >= G)
        def _():
            dyo_ref[...] = (d * gate).astype(BF16)
            dzg_ref[...] = (d * o_ref[...] * dgate).astype(BF16)

    return pl.pallas_call(
        body, name=name, grid=(W // GW, L // tm),
        in_specs=[full, y_spec, o_spec, zg_spec, w_spec],
        out_specs=[full, full, pl.BlockSpec((SUBLANES, GW), lambda j, i: (0, j))],
        out_shape=[SDS((L, W), BF16), SDS((L, W), BF16), SDS((SUBLANES, W), F32)],
        compiler_params=_cp(("arbitrary", "arbitrary"), 2 * tm * GW * 24))(dmix, y, o, proj, snw)


def adamw(parts, w, m, v, name):
    R, C = w.shape
    n_slot = parts.shape[0]
    tr = _pick(R, max(SUBLANES, (1 << 18) // C // SUBLANES * SUBLANES), SUBLANES)
    c1, c2 = 1.0 - ADAM_B1 ** ADAM_STEP, 1.0 - ADAM_B2 ** ADAM_STEP

    def body(p_ref, w_ref, m_ref, v_ref, g_ref, d_ref, m2_ref, v2_ref):
        g = p_ref[0].astype(F32)
        for s in range(1, n_slot):
            g = g + p_ref[s].astype(F32)
        m2 = ADAM_B1 * m_ref[...] + (1.0 - ADAM_B1) * g
        v2 = ADAM_B2 * v_ref[...] + (1.0 - ADAM_B2) * (g * g)
        g_ref[...] = g
        m2_ref[...] = m2
        v2_ref[...] = v2
        d_ref[...] = -ADAM_LR * ((m2 / c1) / (jnp.sqrt(v2 / c2) + ADAM_EPS) + ADAM_WD * w_ref[...])

    blk = pl.BlockSpec((tr, C), lambda i: (i, 0))
    return pl.pallas_call(
        body, name=name, grid=(R // tr,),
        in_specs=[pl.BlockSpec((n_slot, tr, C), lambda i: (0, i, 0)), blk, blk, blk],
        out_specs=[blk] * 4, out_shape=[SDS((R, C), F32)] * 4,
        compiler_params=_cp(("arbitrary",), 2 * tr * C * (n_slot * 4 + 28)))(parts, w, m, v)


N_CHIP = 4


def _place():
    x, y, c = lax.axis_index("x"), lax.axis_index("y"), lax.axis_index("c")
    return x, y, c, [(1 - x, y), (x, 1 - y), (1 - x, 1 - y)]


def _comm_call(body, arrays, out_shape, n_sems, n_local, name):
    hbm = pl.BlockSpec(memory_space=pltpu.HBM)
    return pl.pallas_call(
        body, name=name, in_specs=[hbm] * len(arrays), out_specs=[hbm] * len(out_shape), out_shape=out_shape,
        scratch_shapes=[pltpu.SemaphoreType.DMA((n_sems,)), pltpu.SemaphoreType.DMA((n_sems,)),
                        pltpu.SemaphoreType.DMA((n_local,))],
        compiler_params=pltpu.CompilerParams(has_side_effects=True))(*arrays)


def gather_weights(arrays, name):
    n, per = len(arrays), N_DEV - 1

    def body(*refs):
        srcs, dsts = refs[:n], refs[n:2 * n]
        send_sems, recv_sems, local_sems = refs[2 * n:]
        start, finish = _gather_halves(srcs, dsts, send_sems, recv_sems, local_sems)
        start()
        finish()

    out_shape = [SDS((N_DEV,) + a.shape, a.dtype) for a in arrays]
    return _comm_call(body, arrays, out_shape, n * per, n, name)


def _gather_halves(srcs, dsts, send_sems, recv_sems, local_sems):
    n, per = len(srcs), N_DEV - 1

    def parts():
        x, y, c, chips = _place()
        me, sib = 4 * x + 2 * y + c, (x, y, 1 - c)

        def cp(a, k, block, to, src=None):
            return pltpu.make_async_remote_copy(
                src_ref=dsts[a].at[block] if src is None else src, dst_ref=dsts[a].at[block],
                send_sem=send_sems.at[a * per + k], recv_sem=recv_sems.at[a * per + k],
                device_id=to, device_id_type=MESH)

        own = [pltpu.make_async_copy(srcs[a], dsts[a].at[me], local_sems.at[a]) for a in range(n)]
        first = []
        for a in range(n):
            first.append(cp(a, 0, me, sib, src=srcs[a]))
            first += [cp(a, 1 + j, me, (px, py, c), src=srcs[a]) for j, (px, py) in enumerate(chips)]
        return x, y, c, chips, sib, cp, own, first

    def start():
        *_, own, first = parts()
        for o in own:
            o.start()
        for f in first:
            f.start()

    def finish():
        x, y, c, chips, sib, cp, own, first = parts()
        passed = []
        for j, (px, py) in enumerate(chips):
            block = 4 * px + 2 * py + c
            for a in range(n):
                cp(a, 1 + j, block, sib).wait_recv()
                fwd = cp(a, 4 + j, block, sib)
                fwd.start()
                passed.append(fwd)
        for a in range(n):
            cp(a, 0, 4 * x + 2 * y + 1 - c, sib).wait_recv()
            for j, (px, py) in enumerate(chips):
                cp(a, 4 + j, 4 * px + 2 * py + 1 - c, sib).wait_recv()
        for f in first + passed:
            f.wait_send()
        for o in own:
            o.wait()

    return start, finish


def gather_beside(shards):
    shards = list(shards)
    return Exchange(arrays=shards, out_shape=[SDS((N_DEV,) + s.shape, s.dtype) for s in shards],
                    n_sems=len(shards) * (N_DEV - 1), n_local=len(shards), halves=_gather_halves)


def pair_exchange(arrays, name):
    n = len(arrays)

    def body(*refs):
        srcs, dsts = refs[:n], refs[n:2 * n]
        send_sems, recv_sems, _ = refs[2 * n:]
        x, y, c, _chips = _place()
        sib = (x, y, 1 - c)

        def cp(a, k):
            return pltpu.make_async_remote_copy(
                src_ref=srcs[a].at[2 * k + 1 - c], dst_ref=dsts[a].at[k],
                send_sem=send_sems.at[a * N_CHIP + k], recv_sem=recv_sems.at[a * N_CHIP + k],
                device_id=sib, device_id_type=MESH)

        cps = [cp(a, k) for k in range(N_CHIP) for a in range(n)]
        for p in cps:
            p.start()
        for p in cps:
            p.wait_recv()
        for p in cps:
            p.wait_send()

    out_shape = [SDS((N_CHIP,) + a.shape[1:], a.dtype) for a in arrays]
    return _comm_call(body, arrays, out_shape, n * N_CHIP, 1, name)


def pair_add(parts, got, name):
    _, R, C = parts.shape
    tr = _pick(R, max(16, (1 << 19) // C // 16 * 16), 16)
    core = lax.axis_index("c").astype(jnp.int32).reshape(1)

    def body(c_ref, p_ref, g_ref, o_ref):
        o_ref[...] = (p_ref[...].astype(F32) + g_ref[...].astype(F32)).astype(o_ref.dtype)

    grid_spec = pltpu.PrefetchScalarGridSpec(
        num_scalar_prefetch=1, grid=(N_CHIP, R // tr),
        in_specs=[pl.BlockSpec((1, tr, C), lambda k, i, c_ref: (2 * k + c_ref[0], i, 0)),
                  pl.BlockSpec((1, tr, C), lambda k, i, c_ref: (k, i, 0))],
        out_specs=pl.BlockSpec((1, tr, C), lambda k, i, c_ref: (k, i, 0)))
    return pl.pallas_call(
        body, name=name, grid_spec=grid_spec, out_shape=SDS((N_CHIP, R, C), parts.dtype),
        compiler_params=_cp(("arbitrary", "arbitrary"), 2 * 3 * tr * C * 2 + 3 * tr * C * 4))(core, parts, got)


def chip_exchange_spec(sums, full=(), split=()):
    ns, nf, nsp = len(sums), len(full), len(split)
    n_sem = 3 * ns + (N_DEV - 1) * (nf + nsp)

    def copies(srcs, dsts, send_sems, recv_sems, local_sems):
        x, y, c, chips = _place()
        me, my_chip = 4 * x + 2 * y + c, 2 * x + y
        started, arrivals, own = [], [], []
        for a in range(ns):
            own.append(pltpu.make_async_copy(srcs[a].at[my_chip], dsts[a].at[my_chip], local_sems.at[a]))
            for j, (px, py) in enumerate(chips):
                k = 2 * px + py
                sem = 3 * a + j
                started.append(pltpu.make_async_remote_copy(
                    src_ref=srcs[a].at[k], dst_ref=dsts[a].at[my_chip],
                    send_sem=send_sems.at[sem], recv_sem=recv_sems.at[sem],
                    device_id=(px, py, c), device_id_type=MESH))
                arrivals.append(dict(
                    src_ref=srcs[a].at[my_chip], dst_ref=dsts[a].at[k],
                    send_sem=send_sems.at[sem], recv_sem=recv_sems.at[sem],
                    device_id=(px, py, c), device_id_type=MESH))
        for b in range(nf + nsp):
            a = ns + b
            is_split = b >= nf
            own.append(pltpu.make_async_copy(srcs[a].at[me] if is_split else srcs[a], dsts[a].at[me],
                                             local_sems.at[a]))
            for rel in range(1, N_DEV):
                px, py, pc = x ^ ((rel >> 2) & 1), y ^ ((rel >> 1) & 1), c ^ (rel & 1)
                pidx = 4 * px + 2 * py + pc
                sem = 3 * ns + b * (N_DEV - 1) + rel - 1
                started.append(pltpu.make_async_remote_copy(
                    src_ref=srcs[a].at[pidx] if is_split else srcs[a], dst_ref=dsts[a].at[me],
                    send_sem=send_sems.at[sem], recv_sem=recv_sems.at[sem],
                    device_id=(px, py, pc), device_id_type=MESH))
                arrivals.append(dict(
                    src_ref=srcs[a].at[me] if is_split else srcs[a], dst_ref=dsts[a].at[pidx],
                    send_sem=send_sems.at[sem], recv_sem=recv_sems.at[sem],
                    device_id=(px, py, pc), device_id_type=MESH))
        return own, started, arrivals

    def halves(*refs):
        def start():
            own, started, _ = copies(*refs)
            for o in own:
                o.start()
            for s in started:
                s.start()

        def finish():
            own, started, arrivals = copies(*refs)
            for r in arrivals:
                pltpu.make_async_remote_copy(**r).wait_recv()
            for s in started:
                s.wait_send()
            for o in own:
                o.wait()

        return start, finish

    out_shape = ([SDS(a.shape, a.dtype) for a in sums] + [SDS((N_DEV,) + a.shape, a.dtype) for a in full]
                 + [SDS(a.shape, a.dtype) for a in split])
    return Exchange(arrays=list(sums) + list(full) + list(split), out_shape=out_shape, n_sems=n_sem,
                    n_local=ns + nf + nsp, halves=halves)


def chip_exchange(sums, full, split, name):
    ex = chip_exchange_spec(sums, full, split)
    n = len(ex.arrays)

    def body(*refs):
        start, finish = ex.halves(refs[:n], refs[n:2 * n], *refs[2 * n:])
        start()
        finish()

    return _comm_call(body, ex.arrays, ex.out_shape, ex.n_sems, ex.n_local, name)


class LayerParams(NamedTuple):
    nw: jax.Array
    w_in: jax.Array
    cw: jax.Array
    cb: jax.Array
    par: jax.Array
    par_all: jax.Array
    snw: jax.Array
    w_out: jax.Array


def head_params(dt_bias, a_log, d_skip, dm):
    rows = jnp.stack([dt_bias, a_log, d_skip])
    par_all = jnp.pad(rows, ((0, SUBLANES - 3), (0, LANES - dm.NH)))
    par = jnp.pad(rows.reshape(3, dm.G, dm.R).transpose(1, 0, 2), ((0, 0), (0, SUBLANES - 3), (0, LANES - dm.R)))
    return par, par_all


def layer_fwd(x, p, dm, tag, next_shards=None):
    h = rms_fwd(x, p.nw, f"rms_fwd{tag}")
    gathered = None
    if next_shards is None:
        proj = mm(h, p.w_in, "nn", tm=512, tn=1920, tk=dm.D, name=f"in_proj{tag}", b_outer=True)
    else:
        proj, gathered = mm(h, p.w_in, "nn", tm=512, tn=1920, tk=dm.D, name=f"in_proj_gather{tag}", b_outer=True,
                            exchange=gather_beside(next_shards))
    xa = conv_fwd(proj, p.cw, p.cb, dm, f"conv_fwd{tag}")
    prep = ssd_prep(proj, p.par_all, dm, f"ssd_prep{tag}")
    y, hprev = ssd_fwd(xa, prep, p.par, dm, f"ssd_fwd{tag}")
    o, tot, nblk = attn_fwd(proj, dm, f"attn_fwd{tag}")
    mix = gate_fwd(y, o, proj, p.snw, dm, f"gate_fwd{tag}")
    xn = mm(mix, p.w_out, "nn", tm=512, tn=1024, tk=dm.SW + dm.AW, name=f"out_proj{tag}", res=x)
    return xn, (x, h, proj, xa, prep, y, hprev, o, tot, nblk, mix), gathered


def layer_bwd(dxn, saved, p, dm, tag, exchange=None):
    x, h, proj, xa, prep, y, hprev, o, tot, nblk, mix = saved
    dmix = mm(dxn, p.w_out, "nt", tm=512, tn=1024, tk=dm.D, name=f"d_mix{tag}", out_dtype=BF16)
    dw_out = mm(mix, dxn, "tn", tm=1024, tn=dm.D, tk=512, name=f"dw_out{tag}", out_dtype=BF16)
    dyo, dzg, dsnw = gate_bwd(dmix, y, o, proj, p.snw, dm, f"gate_bwd{tag}")
    dq, dk, dv = attn_bwd(proj, dyo, tot, nblk, dm, f"attn_bwd{tag}")
    dxs, db, dc, dac, span, ddtx, gsk = ssd_bwd(xa, prep, p.par, hprev, dyo, dm, f"ssd_bwd{tag}")
    ddt_blk, ghead = ssd_post(proj, p.par_all, dac, span, ddtx, gsk, dm, f"ssd_post{tag}")
    dcv, gconv = conv_bwd_pre(proj, dxs, db, dc, p.cw, p.cb, dm, f"conv_bwd_pre{tag}")
    dxbc = conv_bwd_in(dcv, p.cw, dm, f"conv_bwd_in{tag}")
    dproj = jnp.concatenate([dzg[:, :dm.SW], dxbc, dq, dk, dv, dzg[:, dm.SW:], ddt_blk], axis=1)
    brought = None
    if exchange is None:
        dh = mm(dproj, p.w_in, "nt", tm=512, tn=dm.D, tk=1920, name=f"d_h{tag}")
    else:
        dh, brought = mm(dproj, p.w_in, "nt", tm=512, tn=dm.D, tk=1920, name=f"d_h_exchange{tag}", exchange=exchange)
    dw_in = mm(h, dproj, "tn", tm=dm.D, tn=960, tk=2048, name=f"dw_in{tag}", out_dtype=BF16)
    dx, dnw = rms_bwd(dh, x, p.nw, dxn, f"rms_bwd{tag}")
    small = dict(norm_w=dnw[0], conv_w=gconv[:SSD_CONV], conv_b=gconv[SSD_CONV],
                 dt_bias=ghead[0, :dm.NH], a_log=ghead[1, :dm.NH], d_skip=ghead[2, :dm.NH],
                 ssd_norm_w=dsnw[0, :dm.SW])
    return dx, dw_in, dw_out, small, brought


SMALL = ("norm_w", "conv_b", "dt_bias", "a_log", "d_skip", "ssd_norm_w")


def _to_mine(w, dm):
    a, b = dm.SW + dm.CD, dm.SW + dm.CD + dm.NH
    pad = jnp.zeros((w.shape[0], LANES - dm.NH), w.dtype)
    return jnp.concatenate([w[:, :a], w[:, b:], w[:, a:b], pad], axis=1)


def _from_mine(w, dm):
    a = dm.SW + dm.CD
    return jnp.concatenate([w[:, :a], w[:, dm.dt_off:dm.dt_off + dm.NH], w[:, a:dm.dt_off]], axis=1)


def weights_to_mine(g_in, dm, name):
    _, D, ncol = g_in.shape
    tm = _pick(D, 256, 16)

    def body(g_ref, o_ref):
        full = jnp.concatenate([g_ref[j] for j in range(N_DEV)], axis=1)
        o_ref[...] = _to_mine(full, dm)

    return pl.pallas_call(
        body, name=name, grid=(D // tm,),
        in_specs=[pl.BlockSpec((N_DEV, tm, ncol), lambda i: (0, i, 0))],
        out_specs=pl.BlockSpec((tm, dm.NP), lambda i: (i, 0)), out_shape=SDS((D, dm.NP), g_in.dtype),
        compiler_params=_cp(("arbitrary",), 6 * tm * dm.NP * 2))(g_in)


def grads_from_mine(gw, dm, ncol, name):
    D = gw.shape[0]
    tm = _pick(D, 256, 16)

    def body(g_ref, o_ref):
        full = _from_mine(g_ref[...], dm)
        for j in range(N_DEV):
            o_ref[j] = full[:, j * ncol:(j + 1) * ncol]

    return pl.pallas_call(
        body, name=name, grid=(D // tm,),
        in_specs=[pl.BlockSpec((tm, dm.NP), lambda i: (i, 0))],
        out_specs=pl.BlockSpec((N_DEV, tm, ncol), lambda i: (0, i, 0)), out_shape=SDS((N_DEV, D, ncol), gw.dtype),
        compiler_params=_cp(("arbitrary",), 6 * tm * dm.NP * 2))(gw)


def _pack(pieces):
    flat = jnp.concatenate([p.reshape(-1) for p in pieces])
    rows = -(-flat.shape[0] // LANES)
    rows = -(-rows // SUBLANES) * SUBLANES
    return jnp.pad(flat, (0, rows * LANES - flat.shape[0])).reshape(rows, LANES)


def _unpack(buf, shapes):
    flat, out, at = buf.reshape(-1), [], 0
    for s in shapes:
        n = math.prod(s)
        out.append(flat[at:at + n].reshape(s))
        at += n
    return out


def kernel(x, norm_w, w_in, conv_w, conv_b, dt_bias, a_log, d_skip, ssd_norm_w, w_out, final_norm_w, loss_target, m_norm_w, m_w_in, m_conv_w, m_conv_b, m_dt_bias, m_a_log, m_d_skip, m_ssd_norm_w, m_w_out, m_final_norm_w, v_norm_w, v_w_in, v_conv_w, v_conv_b, v_dt_bias, v_a_log, v_d_skip, v_ssd_norm_w, v_w_out, v_final_norm_w):
    depth, D = norm_w.shape
    L = x.shape[1]
    NH = dt_bias.shape[1]
    SW = NH * SSD_HEAD_DIM
    CD = conv_b.shape[1]
    dm = Dims(L=L, D=D, SW=SW, G=(CD - SW) // (2 * SSD_STATE), AW=w_out.shape[1] * N_DEV - SW)
    ncol, csh, osh = w_in.shape[2], conv_w.shape[2], w_out.shape[1]
    me = 4 * lax.axis_index("x") + 2 * lax.axis_index("y") + lax.axis_index("c")

    shards = [(w_in[l].astype(BF16), w_out[l].astype(BF16), conv_w[l]) for l in range(depth)]

    def layer_params(l, gathered):
        g_in, g_out, g_cw = gathered
        full_cw = g_cw.transpose(1, 0, 2).reshape(SSD_CONV, CD)
        par, par_all = head_params(dt_bias[l], a_log[l], d_skip[l], dm)
        return LayerParams(
            nw=norm_w[l][None], w_in=weights_to_mine(g_in, dm, "weights_to_mine"), cw=full_cw, cb=conv_b[l][None],
            par=par, par_all=par_all,
            snw=ssd_norm_w[l][None], w_out=g_out.reshape(N_DEV * osh, D))

    h = x[0]
    params, saved = [], []
    gathered = gather_weights(list(shards[0]), "gather_weights")
    for l in range(depth):
        params.append(layer_params(l, gathered))
        h, s, gathered = layer_fwd(h, params[l], dm, "", shards[l + 1] if l + 1 < depth else None)
        saved.append(s)
    dh, dfw, ls = loss_head(h, final_norm_w[None], loss_target[0], "loss_head")
    loss = lax.psum(ls[0, 0], ("x", "y", "c"))
    smalls, r_in, r_out = [None] * depth, [None] * depth, [None] * depth
    pending = None
    for l in reversed(range(depth)):
        ex = None if pending is None else chip_exchange_spec(pending)
        dh, gw_in, gw_out, smalls[l], brought = layer_bwd(dh, saved[l], params[l], dm, "", ex)
        if brought is not None:
            r_in[l + 1], r_out[l + 1] = brought
        p_in = grads_from_mine(gw_in, dm, ncol, "grads_from_mine")
        p_out = gw_out.astype(BF16).reshape(N_DEV, osh, D)
        s_in, s_out = pair_exchange([p_in, p_out], "pair_exchange")
        pending = [pair_add(p_in, s_in, "pair_add_w_in"), pair_add(p_out, s_out, "pair_add_w_out")]
    grad_x = dh[None]
    rep = [jnp.stack([s[k] for s in smalls]) for k in SMALL] + [dfw[0]]
    rep_shapes = [r.shape for r in rep]
    p_rep = _pack(rep)
    p_cw = jnp.stack([s["conv_w"] for s in smalls]).reshape(depth * SSD_CONV, N_DEV, csh).transpose(1, 0, 2)
    r_in[0], r_out[0], r_rep, r_cw = chip_exchange(pending, [p_rep], [p_cw], "chip_exchange")
    r_in, r_out = jnp.concatenate(r_in, axis=1), jnp.concatenate(r_out, axis=1)

    out_in = adamw(r_in, w_in.reshape(depth * D, ncol), m_w_in.reshape(depth * D, ncol),
                   v_w_in.reshape(depth * D, ncol), "adamw_w_in")
    out_out = adamw(r_out, w_out.reshape(depth * osh, D), m_w_out.reshape(depth * osh, D),
                    v_w_out.reshape(depth * osh, D), "adamw_w_out")
    out_cw = adamw(r_cw, conv_w.reshape(depth * SSD_CONV, csh), m_conv_w.reshape(depth * SSD_CONV, csh),
                   v_conv_w.reshape(depth * SSD_CONV, csh), "adamw_conv_w")
    rep_w = [norm_w, conv_b, dt_bias, a_log, d_skip, ssd_norm_w, final_norm_w]
    rep_m = [m_norm_w, m_conv_b, m_dt_bias, m_a_log, m_d_skip, m_ssd_norm_w, m_final_norm_w]
    rep_v = [v_norm_w, v_conv_b, v_dt_bias, v_a_log, v_d_skip, v_ssd_norm_w, v_final_norm_w]
    out_rep = adamw(r_rep, _pack(rep_w), _pack(rep_m), _pack(rep_v), "adamw_replicated")

    outs = {}
    for kind, i in (("grad", 0), ("delta", 1), ("new_m", 2), ("new_v", 3)):
        r = dict(zip(SMALL + ("final_norm_w",), _unpack(out_rep[i], rep_shapes)))
        r["w_in"] = out_in[i].reshape(w_in.shape)
        r["w_out"] = out_out[i].reshape(w_out.shape)
        r["conv_w"] = out_cw[i].reshape(conv_w.shape)
        outs[kind] = r
    order = ("norm_w", "w_in", "conv_w", "conv_b", "dt_bias", "a_log", "d_skip", "ssd_norm_w", "w_out", "final_norm_w")
    return (loss, grad_x, *[outs[k][n] for k in ("grad", "delta", "new_m", "new_v") for n in order])
```

```python
import math
from typing import NamedTuple

import jax
import jax.numpy as jnp
from jax import lax
from jax.experimental import pallas as pl
from jax.experimental.pallas import tpu as pltpu

F32, BF16 = jnp.float32, jnp.bfloat16
SDS = jax.ShapeDtypeStruct
EPS = 1e-6
LANES = 128
SUBLANES = 8
VMEM_BYTES = 64 * 2 ** 20
N_DEV = 8
SSD_HEAD_DIM = 64
SSD_STATE = 128
SSD_CONV = 4
SBA_HEAD_DIM = 128
ADAM_LR, ADAM_B1, ADAM_B2, ADAM_EPS, ADAM_WD, ADAM_STEP = 0.001, 0.9, 0.999, 1e-08, 0.01, 10
SKIP_SUM = 110.0
HIGHEST = lax.Precision.HIGHEST
MESH = pl.DeviceIdType.MESH


class Dims(NamedTuple):
    L: int
    D: int
    SW: int
    G: int
    AW: int

    @property
    def NH(self): return self.SW // SSD_HEAD_DIM
    @property
    def R(self): return self.NH // self.G
    @property
    def GW(self): return self.SW // self.G
    @property
    def CD(self): return self.SW + 2 * self.G * SSD_STATE
    @property
    def AH(self): return self.AW // SBA_HEAD_DIM
    @property
    def q_off(self): return self.SW + self.CD
    @property
    def k_off(self): return self.q_off + self.AW
    @property
    def v_off(self): return self.q_off + 2 * self.AW
    @property
    def g_off(self): return self.q_off + 3 * self.AW
    @property
    def dt_off(self): return self.q_off + 4 * self.AW
    @property
    def NP(self): return self.dt_off + LANES


def _pick(n, target, mult):
    t = (min(target, n) // mult) * mult
    while t >= mult:
        if n % t == 0:
            return t
        t -= mult
    return n


def _cp(sem, vmem_est):
    limit = int(min(max(vmem_est * 5 // 4 + (4 << 20), 32 << 20), VMEM_BYTES - (8 << 20)))
    return pltpu.CompilerParams(dimension_semantics=sem, vmem_limit_bytes=limit)


def _sigmoid(x):
    return 1.0 / (1.0 + jnp.exp(-x))


def _softplus(x):
    return jnp.maximum(x, 0.0) + jnp.log(1.0 + jnp.exp(-jnp.abs(x)))


_DOT_DIMS = {"nn": ((1,), (0,)), "nt": ((1,), (1,)), "tn": ((0,), (0,))}


class Exchange(NamedTuple):
    arrays: list
    out_shape: list
    n_sems: int
    n_local: int
    halves: object


def mm(a, b, mode, *, tm, tn, tk, name, res=None, b_outer=False, exchange=None, out_dtype=F32):
    if mode == "nn":
        (M, K), N = a.shape, b.shape[1]
    elif mode == "nt":
        (M, K), N = a.shape, b.shape[0]
    else:
        (K, M), N = a.shape, b.shape[1]
    tm, tn, tk = _pick(M, tm, LANES), _pick(N, tn, LANES), _pick(K, tk, LANES)
    nk = K // tk

    def ij(p0, p1):
        return (p1, p0) if b_outer else (p0, p1)

    if mode == "tn":
        a_spec = pl.BlockSpec((tk, tm), lambda p0, p1, k: (k, ij(p0, p1)[0]))
    else:
        a_spec = pl.BlockSpec((tm, tk), lambda p0, p1, k: (ij(p0, p1)[0], k))
    if mode == "nt":
        b_spec = pl.BlockSpec((tn, tk), lambda p0, p1, k: (ij(p0, p1)[1], k))
    else:
        b_spec = pl.BlockSpec((tk, tn), lambda p0, p1, k: (k, ij(p0, p1)[1]))
    o_spec = pl.BlockSpec((tm, tn), lambda p0, p1, k: ij(p0, p1))
    dims = (_DOT_DIMS[mode], ((), ()))

    grid = (N // tn, M // tm, nk) if b_outer else (M // tm, N // tn, nk)
    n_res = 0 if res is None else 1
    n_ex = 0 if exchange is None else len(exchange.arrays)

    def body(*refs):
        a_ref, b_ref, o_ref = refs[0], refs[1], refs[2 + n_res + n_ex]
        ids = [pl.program_id(d) for d in range(3)]
        if exchange is not None:
            srcs = refs[2 + n_res:2 + n_res + n_ex]
            dsts = refs[3 + n_res + n_ex:3 + n_res + 2 * n_ex]
            start, finish = exchange.halves(srcs, dsts, *refs[3 + n_res + 2 * n_ex:])

            @pl.when(jnp.logical_and(jnp.logical_and(ids[0] == 0, ids[1] == 0), ids[2] == 0))
            def _():
                start()

        part = lax.dot_general(a_ref[...].astype(BF16), b_ref[...].astype(BF16), dims,
                               preferred_element_type=F32)
        if res is not None:
            first = part + refs[2][...]
        else:
            first = part
        if nk == 1:
            o_ref[...] = first.astype(out_dtype)
        else:
            acc_ref = o_ref if out_dtype == F32 else refs[-1]

            @pl.when(ids[2] == 0)
            def _():
                acc_ref[...] = first

            @pl.when(ids[2] > 0)
            def _():
                acc_ref[...] += part

            if out_dtype != F32:
                @pl.when(ids[2] == nk - 1)
                def _():
                    o_ref[...] = acc_ref[...].astype(out_dtype)

        if exchange is not None:
            @pl.when(jnp.logical_and(jnp.logical_and(ids[0] == grid[0] - 1, ids[1] == grid[1] - 1),
                                     ids[2] == nk - 1))
            def _():
                finish()

    ins, specs = [a, b], [a_spec, b_spec]
    if res is not None:
        ins.append(res)
        specs.append(o_spec)
    est = 2 * (tm * tk * a.dtype.itemsize + tk * tn * b.dtype.itemsize + tm * tn * 4 * (2 if res is not None else 1))
    est += tm * tk * 2 + tk * tn * 2 + tm * tn * 4
    if exchange is None:
        scratch = [pltpu.VMEM((tm, tn), F32)] if (out_dtype != F32 and nk > 1) else []
        return pl.pallas_call(
            body, name=name, grid=grid, in_specs=specs, out_specs=o_spec,
            out_shape=SDS((M, N), out_dtype), scratch_shapes=scratch,
            compiler_params=_cp(("arbitrary", "arbitrary", "arbitrary"), est))(*ins)
    assert out_dtype == F32
    hbm = pl.BlockSpec(memory_space=pltpu.HBM)
    cp = _cp(("arbitrary", "arbitrary", "arbitrary"), est)
    out = pl.pallas_call(
        body, name=name, grid=grid, in_specs=specs + [hbm] * n_ex, out_specs=[o_spec] + [hbm] * n_ex,
        out_shape=[SDS((M, N), F32)] + list(exchange.out_shape),
        scratch_shapes=[pltpu.SemaphoreType.DMA((exchange.n_sems,)), pltpu.SemaphoreType.DMA((exchange.n_sems,)),
                        pltpu.SemaphoreType.DMA((exchange.n_local,))],
        compiler_params=pltpu.CompilerParams(
            dimension_semantics=cp.dimension_semantics, vmem_limit_bytes=cp.vmem_limit_bytes,
            has_side_effects=True))(*ins, *exchange.arrays)
    return out[0], out[1:]


def rms_fwd(x, nw, name):
    L, D = x.shape
    tm = _pick(L, 512, SUBLANES)

    def body(x_ref, w_ref, h_ref):
        xx = x_ref[...]
        r = lax.rsqrt(jnp.mean(xx * xx, axis=-1, keepdims=True) + EPS)
        h_ref[...] = (xx * r * w_ref[...]).astype(BF16)

    return pl.pallas_call(
        body, name=name, grid=(L // tm,),
        in_specs=[pl.BlockSpec((tm, D), lambda i: (i, 0)), pl.BlockSpec((1, D), lambda i: (0, 0))],
        out_specs=pl.BlockSpec((tm, D), lambda i: (i, 0)), out_shape=SDS((L, D), BF16),
        compiler_params=_cp(("arbitrary",), 2 * tm * D * 6))(x, nw)


def rms_bwd(dh, x, nw, dres, name):
    L, D = x.shape
    tm = _pick(L, 256, SUBLANES)

    def body(dh_ref, x_ref, w_ref, dr_ref, dx_ref, dw_ref):
        @pl.when(pl.program_id(0) == 0)
        def _():
            dw_ref[...] = jnp.zeros_like(dw_ref)

        xx, d = x_ref[...], dh_ref[...]
        r = lax.rsqrt(jnp.mean(xx * xx, axis=-1, keepdims=True) + EPS)
        xh = xx * r
        dw_ref[0:1, :] += jnp.sum(d * xh, axis=0, keepdims=True)
        dxh = d * w_ref[...]
        dx_ref[...] = dr_ref[...] + r * (dxh - xh * jnp.mean(dxh * xh, axis=-1, keepdims=True))

    row = pl.BlockSpec((tm, D), lambda i: (i, 0))
    return pl.pallas_call(
        body, name=name, grid=(L // tm,),
        in_specs=[row, row, pl.BlockSpec((1, D), lambda i: (0, 0)), row],
        out_specs=[row, pl.BlockSpec((SUBLANES, D), lambda i: (0, 0))],
        out_shape=[SDS((L, D), F32), SDS((SUBLANES, D), F32)],
        compiler_params=_cp(("arbitrary",), 2 * tm * D * 16))(dh, x, nw, dres)


def loss_head(h, fw, tgt, name):
    L, D = h.shape
    tm = _pick(L, 256, SUBLANES)

    def body(h_ref, w_ref, t_ref, dh_ref, dw_ref, ls_ref):
        @pl.when(pl.program_id(0) == 0)
        def _():
            dw_ref[...] = jnp.zeros_like(dw_ref)
            ls_ref[...] = jnp.zeros_like(ls_ref)

        xx = h_ref[...]
        r = lax.rsqrt(jnp.mean(xx * xx, axis=-1, keepdims=True) + EPS)
        xh = xx * r
        err = xh * w_ref[...] - t_ref[...]
        per_tok = jnp.mean(err * err, axis=-1, keepdims=True)
        ls_ref[...] += jnp.broadcast_to(0.5 * jnp.sum(per_tok, axis=0, keepdims=True), ls_ref.shape)
        dy = err * (1.0 / D)
        dw_ref[0:1, :] += jnp.sum(dy * xh, axis=0, keepdims=True)
        dxh = dy * w_ref[...]
        dh_ref[...] = r * (dxh - xh * jnp.mean(dxh * xh, axis=-1, keepdims=True))

    row = pl.BlockSpec((tm, D), lambda i: (i, 0))
    return pl.pallas_call(
        body, name=name, grid=(L // tm,),
        in_specs=[row, pl.BlockSpec((1, D), lambda i: (0, 0)), row],
        out_specs=[row, pl.BlockSpec((SUBLANES, D), lambda i: (0, 0)),
                   pl.BlockSpec((SUBLANES, LANES), lambda i: (0, 0))],
        out_shape=[SDS((L, D), F32), SDS((SUBLANES, D), F32), SDS((SUBLANES, LANES), F32)],
        compiler_params=_cp(("arbitrary",), 2 * tm * D * 12))(h, fw, tgt)


def _shifted(u, edge, s, back):
    n = u.shape[0]
    row = lax.broadcasted_iota(jnp.int32, (SUBLANES, u.shape[1]), 0)
    if back:
        r = pltpu.roll(u, s, 0)
        head = jnp.where(row < s, pltpu.roll(edge, s, 0), r[0:SUBLANES])
        return jnp.concatenate([head, r[SUBLANES:]], axis=0)
    r = pltpu.roll(u, n - s, 0)
    tail = jnp.where(row >= SUBLANES - s, pltpu.roll(edge, SUBLANES - s, 0), r[n - SUBLANES:])
    return jnp.concatenate([r[:n - SUBLANES], tail], axis=0)


def _conv_pre(u, prev, w, b):
    acc = b + w[SSD_CONV - 1:SSD_CONV] * u
    taps = [u]
    for s in range(1, SSD_CONV):
        us = _shifted(u, prev, s, True)
        taps.append(us)
        acc = acc + w[SSD_CONV - 1 - s:SSD_CONV - s] * us
    return acc, taps


def _conv_specs(dm, tm, tc, col0):
    rb = tm // SUBLANES
    u_spec = pl.BlockSpec((tm, tc), lambda j, i: (i, col0 + j))
    prev_spec = pl.BlockSpec((SUBLANES, tc), lambda j, i: (jnp.maximum(i * rb - 1, 0), col0 + j))
    w_spec = pl.BlockSpec((SSD_CONV, tc), lambda j, i: (0, j))
    b_spec = pl.BlockSpec((1, tc), lambda j, i: (0, j))
    return u_spec, prev_spec, w_spec, b_spec


def conv_fwd(proj, cw, cb, dm, name):
    L, CD = dm.L, dm.CD
    tm, tc = _pick(L, 1024, SUBLANES), _pick(math.gcd(CD, dm.SW), 512, LANES)
    u_spec, prev_spec, w_spec, b_spec = _conv_specs(dm, tm, tc, dm.SW // tc)

    def body(u_ref, p_ref, w_ref, b_ref, o_ref):
        prev = jnp.where(pl.program_id(1) == 0, 0.0, p_ref[...])
        c, _ = _conv_pre(u_ref[...], prev, w_ref[...], b_ref[...])
        o_ref[...] = c * _sigmoid(c)

    return pl.pallas_call(
        body, name=name, grid=(CD // tc, L // tm),
        in_specs=[u_spec, prev_spec, w_spec, b_spec],
        out_specs=pl.BlockSpec((tm, tc), lambda j, i: (i, j)), out_shape=SDS((L, CD), F32),
        compiler_params=_cp(("arbitrary", "arbitrary"), 12 * tm * tc * 4))(proj, proj, cw, cb)


def conv_bwd_pre(proj, dxs, db, dcm, cw, cb, dm, name):
    L, CD = dm.L, dm.CD
    gn = dm.G * SSD_STATE
    tm, tc = _pick(L, 1024, SUBLANES), _pick(math.gcd(gn, dm.SW), 512, LANES)
    u_spec, prev_spec, w_spec, b_spec = _conv_specs(dm, tm, tc, dm.SW // tc)
    nx, nb = dm.SW // tc, gn // tc

    def body(u_ref, p_ref, dx_ref, db_ref, dcm_ref, w_ref, b_ref, dc_ref, g_ref):
        j = pl.program_id(0)

        @pl.when(pl.program_id(1) == 0)
        def _():
            g_ref[...] = jnp.zeros_like(g_ref)

        prev = jnp.where(pl.program_id(1) == 0, 0.0, p_ref[...])
        c, taps = _conv_pre(u_ref[...], prev, w_ref[...], b_ref[...])
        sg = _sigmoid(c)
        d = jnp.where(j < nx, dx_ref[...], jnp.where(j < nx + nb, db_ref[...], dcm_ref[...]))
        dc = d * (sg * (1.0 + c * (1.0 - sg)))
        dc_ref[...] = dc
        for s in range(SSD_CONV):
            g_ref[SSD_CONV - 1 - s:SSD_CONV - s, :] += jnp.sum(dc * taps[s], axis=0, keepdims=True)
        g_ref[SSD_CONV:SSD_CONV + 1, :] += jnp.sum(dc, axis=0, keepdims=True)

    blk = pl.BlockSpec((tm, tc), lambda j, i: (i, j))
    part = lambda lo, n: pl.BlockSpec((tm, tc), lambda j, i: (i, jnp.clip(j - lo, 0, n - 1)))
    return pl.pallas_call(
        body, name=name, grid=(CD // tc, L // tm),
        in_specs=[u_spec, prev_spec, part(0, nx), part(nx, nb), part(nx + nb, nb), w_spec, b_spec],
        out_specs=[blk, pl.BlockSpec((SUBLANES, tc), lambda j, i: (0, j))],
        out_shape=[SDS((L, CD), F32), SDS((SUBLANES, CD), F32)],
        compiler_params=_cp(("arbitrary", "arbitrary"), 20 * tm * tc * 4))(proj, proj, dxs, db, dcm, cw, cb)


def conv_bwd_in(dc, cw, dm, name):
    L, CD = dm.L, dm.CD
    tm, tc = _pick(L, 1024, SUBLANES), _pick(CD, 512, LANES)
    rb, nrow = tm // SUBLANES, L // SUBLANES
    ni = L // tm

    def body(d_ref, n_ref, w_ref, o_ref):
        nxt = jnp.where(pl.program_id(1) == ni - 1, 0.0, n_ref[...])
        dc_, w = d_ref[...], w_ref[...]
        acc = w[SSD_CONV - 1:SSD_CONV] * dc_
        for s in range(1, SSD_CONV):
            acc = acc + w[SSD_CONV - 1 - s:SSD_CONV - s] * _shifted(dc_, nxt, s, False)
        o_ref[...] = acc.astype(BF16)

    blk = pl.BlockSpec((tm, tc), lambda j, i: (i, j))
    return pl.pallas_call(
        body, name=name, grid=(CD // tc, ni),
        in_specs=[blk, pl.BlockSpec((SUBLANES, tc), lambda j, i: (jnp.minimum((i + 1) * rb, nrow - 1), j)),
                  pl.BlockSpec((SSD_CONV, tc), lambda j, i: (0, j))],
        out_specs=blk, out_shape=SDS((L, CD), BF16),
        compiler_params=_cp(("arbitrary", "arbitrary"), 10 * tm * tc * 4))(dc, dc, cw)


def _ssd_chunk(dm):
    return _pick(dm.L, 512, LANES)


def _dt_parts(dtr, par):
    return _softplus(dtr + par[0:1]), -jnp.exp(par[1:2])


def _tri(Q):
    rows = lax.broadcasted_iota(jnp.int32, (Q, Q), 0)
    cols = lax.broadcasted_iota(jnp.int32, (Q, Q), 1)
    return rows, cols


def _lanes_to_group(v, g, R, axis):
    n = v.shape[axis]
    return v if g == 0 else pltpu.roll(v, n - g * R, axis)


def ssd_prep(proj, par_all, dm, name):
    Q, G, R = _ssd_chunk(dm), dm.G, dm.R
    nc = dm.L // Q

    def body(dtr_ref, par_ref, dt_ref, ac_ref, at_ref):
        dt, a = _dt_parts(dtr_ref[...], par_ref[...])
        da = dt * a
        rows, cols = _tri(Q)
        acum = jnp.dot((rows >= cols).astype(F32), da, precision=HIGHEST, preferred_element_type=F32)
        acum_t = lax.dot_general(da, (rows <= cols).astype(F32), (_DOT_DIMS["tn"], ((), ())),
                                 precision=HIGHEST, preferred_element_type=F32)
        for g in range(G):
            dt_ref[g] = _lanes_to_group(dt, g, R, 1)
            ac_ref[g] = _lanes_to_group(acum, g, R, 1)
            at_ref[g, 0] = _lanes_to_group(acum_t, g, R, 0)[0:SUBLANES]

    lane_blk = pl.BlockSpec((G, Q, LANES), lambda c: (0, c, 0))
    return pl.pallas_call(
        body, name=name, grid=(nc,),
        in_specs=[pl.BlockSpec((Q, LANES), lambda c: (c, dm.dt_off // LANES)),
                  pl.BlockSpec((SUBLANES, LANES), lambda c: (0, 0))],
        out_specs=[lane_blk, lane_blk, pl.BlockSpec((G, 1, SUBLANES, Q), lambda c: (0, c, 0, 0))],
        out_shape=[SDS((G, dm.L, LANES), F32), SDS((G, dm.L, LANES), F32), SDS((G, nc, SUBLANES, Q), F32)],
        compiler_params=_cp(("arbitrary",), 8 << 20))(proj, par_all)


def _ssd_common(xa_refs, Q):
    _, b_ref, c_ref = xa_refs
    rows, cols = _tri(Q)
    bm, cm = b_ref[...].astype(BF16), c_ref[...].astype(BF16)
    gm = lax.dot_general(cm, bm, (_DOT_DIMS["nt"], ((), ())), preferred_element_type=F32)
    return rows >= cols, bm, cm, gm


def _expand_heads(v, R):
    P = SSD_HEAD_DIM
    sel = (lax.broadcasted_iota(jnp.int32, (LANES, R * P), 0)
           == lax.broadcasted_iota(jnp.int32, (LANES, R * P), 1) // P).astype(BF16)
    hi = v.astype(BF16)
    lo = (v - hi.astype(F32)).astype(BF16)
    return jnp.dot(hi, sel, preferred_element_type=F32) + jnp.dot(lo, sel, preferred_element_type=F32)


def _ssd_head(r, Q, x, dt_x, tri, acum, acum_t, gm):
    P = SSD_HEAD_DIM
    col = jnp.broadcast_to(acum[:, r:r + 1], (Q, Q))
    row = acum_t[r:r + 1, :]
    lam = jnp.where(tri, jnp.exp(jnp.minimum(col - row, 0.0)), 0.0)
    m = gm * lam
    xh = x[:, r * P:(r + 1) * P]
    xs = xh * dt_x[:, r * P:(r + 1) * P]
    a_last = acum_t[r:r + 1, Q - 1:Q]
    return col[:, :P], lam, m, xh, xs, a_last


def _ssd_specs(dm, Q, rev):
    nc = dm.L // Q
    cc = (lambda c: nc - 1 - c) if rev else (lambda c: c)
    nb = dm.SW // SSD_STATE
    x_spec = pl.BlockSpec((Q, dm.GW), lambda g, c: (cc(c), g))
    b_spec = pl.BlockSpec((Q, SSD_STATE), lambda g, c: (cc(c), nb + g))
    c_spec = pl.BlockSpec((Q, SSD_STATE), lambda g, c: (cc(c), nb + dm.G + g))
    dt_spec = pl.BlockSpec((1, Q, LANES), lambda g, c: (g, cc(c), 0))
    at_spec = pl.BlockSpec((1, 1, SUBLANES, Q), lambda g, c: (g, cc(c), 0, 0))
    par_spec = pl.BlockSpec((1, SUBLANES, LANES), lambda g, c: (g, 0, 0))
    h_spec = pl.BlockSpec((1, 1, dm.GW, SSD_STATE), lambda g, c: (cc(c), g, 0, 0))
    return x_spec, b_spec, c_spec, dt_spec, at_spec, par_spec, h_spec


def ssd_fwd(xa, prep, par, dm, name):
    Q = _ssd_chunk(dm)
    nc, P, R = dm.L // Q, SSD_HEAD_DIM, dm.R
    x_spec, b_spec, c_spec, dt_spec, at_spec, par_spec, h_spec = _ssd_specs(dm, Q, False)

    def body(x_ref, b_ref, c_ref, dt_ref, ac_ref, at_ref, par_ref, y_ref, hp_ref, h_scr):
        @pl.when(pl.program_id(1) == 0)
        def _():
            h_scr[...] = jnp.zeros_like(h_scr)

        tri, bm, cm, gm = _ssd_common((x_ref, b_ref, c_ref), Q)
        par, dt, acum, acum_t = par_ref[0], dt_ref[0], ac_ref[0], at_ref[0, 0]
        x = x_ref[...]
        dt_x = _expand_heads(dt, R)
        hp = h_scr[...]
        hp_ref[0, 0] = hp
        for r in range(R):
            col, lam, m, xh, xs, a_last = _ssd_head(r, Q, x, dt_x, tri, acum, acum_t, gm)
            hpr = hp[r * P:(r + 1) * P, :]
            ydiag = jnp.dot(m.astype(BF16), xs.astype(BF16), preferred_element_type=F32)
            yoff = jnp.exp(col) * lax.dot_general(cm, hpr.astype(BF16), (_DOT_DIMS["nt"], ((), ())),
                                                  preferred_element_type=F32)
            dte = jnp.exp(a_last - col)
            st = lax.dot_general((xs * dte).astype(BF16), bm, (_DOT_DIMS["tn"], ((), ())),
                                 preferred_element_type=F32)
            h_scr[r * P:(r + 1) * P, :] = jnp.exp(a_last) * hpr + st
            y_ref[:, r * P:(r + 1) * P] = ydiag + yoff + par[2:3, r:r + 1] * xh

    return pl.pallas_call(
        body, name=name, grid=(dm.G, nc),
        in_specs=[x_spec, b_spec, c_spec, dt_spec, dt_spec, at_spec, par_spec],
        out_specs=[x_spec, h_spec],
        out_shape=[SDS((dm.L, dm.SW), F32), SDS((nc, dm.G, dm.GW, SSD_STATE), F32)],
        scratch_shapes=[pltpu.VMEM((dm.GW, SSD_STATE), F32)],
        compiler_params=_cp(("arbitrary", "arbitrary"), 24 << 20))(xa, xa, xa, *prep, par)


def ssd_bwd(xa, prep, par, hprev, dyo, dm, name):
    Q = _ssd_chunk(dm)
    nc, P, R, N = dm.L // Q, SSD_HEAD_DIM, dm.R, SSD_STATE
    x_spec, b_spec, c_spec, dt_spec, at_spec, par_spec, h_spec = _ssd_specs(dm, Q, True)
    bc_spec = pl.BlockSpec((Q, N), lambda g, c: (nc - 1 - c, g))
    NT, TN = (_DOT_DIMS["nt"], ((), ())), (_DOT_DIMS["tn"], ((), ()))

    def body(x_ref, b_ref, c_ref, dt_ref, ac_ref, at_ref, par_ref, hp_ref, dy_ref,
             dx_ref, db_ref, dc_ref, dac_ref, span_ref, ddt_ref, gs_ref, dh_scr):
        @pl.when(pl.program_id(1) == 0)
        def _():
            dh_scr[...] = jnp.zeros_like(dh_scr)
            gs_ref[...] = jnp.zeros_like(gs_ref)

        tri, bm, cm, gm = _ssd_common((x_ref, b_ref, c_ref), Q)
        par, dt, acum, acum_t = par_ref[0], dt_ref[0], ac_ref[0], at_ref[0, 0]
        x, dy, hp, dhn = x_ref[...], dy_ref[...].astype(F32), hp_ref[0, 0], dh_scr[...]
        dt_x = _expand_heads(dt, R)
        lane = lax.broadcasted_iota(jnp.int32, (Q, LANES), 1)
        rowi = lax.broadcasted_iota(jnp.int32, (Q, LANES), 0)
        d_acum = jnp.zeros((Q, LANES), F32)
        d_dt = jnp.zeros((Q, LANES), F32)
        d_skip = jnp.zeros((1, LANES), F32)
        dgsum = jnp.zeros((Q, Q), F32)
        dye_all, xse_all, span_rows, da_rows, ddt_rows = [], [], [], [], []
        rq = lax.broadcasted_iota(jnp.int32, (Q, Q), 0)
        cq = lax.broadcasted_iota(jnp.int32, (Q, Q), 1)
        ue = (rq < cq).astype(BF16)
        for r in range(R):
            col, lam, m, xh, xs, a_last = _ssd_head(r, Q, x, dt_x, tri, acum, acum_t, gm)
            sl = slice(r * P, (r + 1) * P)
            dyh, hpr, dhr = dy[:, sl], hp[sl, :], dhn[sl, :]
            mb, xsb, dyb = m.astype(BF16), xs.astype(BF16), dyh.astype(BF16)
            e_a = jnp.exp(col)
            dte = jnp.exp(a_last - col)
            yoff = e_a * lax.dot_general(cm, hpr.astype(BF16), NT, preferred_element_type=F32)
            bdh = dte * lax.dot_general(bm, dhr.astype(BF16), NT, preferred_element_type=F32)
            dxs = lax.dot_general(mb, dyb, TN, preferred_element_type=F32) + bdh
            dm_ = lax.dot_general(dyb, xsb, NT, preferred_element_type=F32)
            dgsum = dgsum + dm_ * lam
            dye, xse = (dyh * e_a).astype(BF16), (xs * dte).astype(BF16)
            dye_all.append(dye)
            xse_all.append(xse)
            dh_scr[sl, :] = (jnp.exp(a_last) * dhr
                             + lax.dot_general(dye, cm, TN, preferred_element_type=F32))
            before = jnp.dot((dm_ * m).astype(BF16), ue, preferred_element_type=F32)
            span_rows.append(jnp.sum(jnp.where(tri, before, 0.0), axis=0, keepdims=True))
            da_rows.append(dyh * yoff - xs * bdh)
            ddt_rows.append(dxs * xh)
            da_last = (jnp.sum(xs * bdh, axis=(0, 1), keepdims=True)
                       + jnp.exp(a_last) * jnp.sum(dhr * hpr, axis=(0, 1), keepdims=True))
            d_acum = d_acum + jnp.where(jnp.logical_and(lane == r, rowi == Q - 1), da_last, 0.0)
            d_skip = d_skip + jnp.where(lane[0:1] == r, jnp.sum(dyh * xh, axis=(0, 1), keepdims=True), 0.0)
            dx_ref[:, sl] = dxs * dt_x[:, sl] + par[2:3, r:r + 1] * dyh
        sel = (lax.broadcasted_iota(jnp.int32, (R * P, LANES), 0) // P
               == lax.broadcasted_iota(jnp.int32, (R * P, LANES), 1)).astype(BF16)

        def head_sums(rows_):
            v = jnp.concatenate(rows_, axis=1)
            hi = v.astype(BF16)
            lo = (v - hi.astype(F32)).astype(BF16)
            return (jnp.dot(hi, sel, preferred_element_type=F32)
                    + jnp.dot(lo, sel, preferred_element_type=F32))

        d_acum = d_acum + head_sums(da_rows)
        d_dt = d_dt + head_sums(ddt_rows)
        dgb = dgsum.astype(BF16)
        dye_c = jnp.concatenate(dye_all, axis=1)
        xse_c = jnp.concatenate(xse_all, axis=1)
        dc_ref[...] = (jnp.dot(dgb, bm, preferred_element_type=F32)
                       + jnp.dot(dye_c, hp.astype(BF16), preferred_element_type=F32))
        db_ref[...] = (lax.dot_general(dgb, cm, TN, preferred_element_type=F32)
                       + jnp.dot(xse_c, dhn.astype(BF16), preferred_element_type=F32))
        dac_ref[0] = d_acum
        ddt_ref[0] = d_dt
        span_ref[0, 0] = jnp.concatenate(span_rows + [jnp.zeros((SUBLANES - R, Q), F32)] * (R < SUBLANES), axis=0)
        gs_ref[0, 2:3, :] += d_skip

    return pl.pallas_call(
        body, name=name, grid=(dm.G, nc),
        in_specs=[x_spec, b_spec, c_spec, dt_spec, dt_spec, at_spec, par_spec, h_spec, x_spec],
        out_specs=[x_spec, bc_spec, bc_spec, dt_spec, at_spec, dt_spec, par_spec],
        out_shape=[SDS((dm.L, dm.SW), F32), SDS((dm.L, dm.G * N), F32), SDS((dm.L, dm.G * N), F32),
                   SDS((dm.G, dm.L, LANES), F32), SDS((dm.G, nc, SUBLANES, Q), F32),
                   SDS((dm.G, dm.L, LANES), F32), SDS((dm.G, SUBLANES, LANES), F32)],
        scratch_shapes=[pltpu.VMEM((dm.GW, N), F32)],
        compiler_params=_cp(("arbitrary", "arbitrary"), 28 << 20))(xa, xa, xa, *prep, par, hprev, dyo)


def ssd_post(proj, par_all, dac, span, ddt, gs, dm, name):
    Q, G, R = _ssd_chunk(dm), dm.G, dm.R
    nc = dm.L // Q
    NT = (_DOT_DIMS["nt"], ((), ()))

    def body(dtr_ref, par_ref, dac_ref, span_ref, ddt_ref, gs_ref, out_ref, acc_ref):
        @pl.when(pl.program_id(0) == 0)
        def _():
            acc_ref[...] = jnp.zeros_like(acc_ref)

        par = par_ref[...]
        dt, a = _dt_parts(dtr_ref[...], par)
        lane = lax.broadcasted_iota(jnp.int32, (Q, LANES), 1)
        row8 = lax.broadcasted_iota(jnp.int32, (SUBLANES, Q), 0)

        def heads(v, g):
            v = jnp.where(lane[:v.shape[0]] < R, v, 0.0)
            return v if g == 0 else pltpu.roll(v, g * R, 1)

        d_acum = sum(heads(dac_ref[g], g) for g in range(G))
        d_dtx = sum(heads(ddt_ref[g], g) for g in range(G))
        d_skip = sum(heads(gs_ref[g][2:3], g) for g in range(G))
        span_t = jnp.zeros((LANES, Q), F32)
        for g in range(G):
            rows_g = jnp.concatenate([jnp.where(row8 < R, span_ref[g, 0], 0.0),
                                      jnp.zeros((LANES - SUBLANES, Q), F32)], axis=0)
            span_t = span_t + (rows_g if g == 0 else pltpu.roll(rows_g, g * R, 0))
        rq, cq = _tri(Q)
        d_da = (jnp.dot((rq <= cq).astype(F32), d_acum, precision=HIGHEST, preferred_element_type=F32)
                + lax.dot_general((rq == cq).astype(F32), span_t, NT, precision=HIGHEST,
                                  preferred_element_type=F32))
        d_raw = (d_dtx + d_da * a) * _sigmoid(dtr_ref[...] + par[0:1])
        out_ref[...] = d_raw.astype(BF16)
        acc_ref[0:1, :] += jnp.sum(d_raw, axis=0, keepdims=True)
        acc_ref[1:2, :] += jnp.sum(d_da * dt, axis=0, keepdims=True) * a
        acc_ref[2:3, :] = d_skip

    lane_blk = pl.BlockSpec((G, Q, LANES), lambda c: (0, c, 0))
    small = pl.BlockSpec((SUBLANES, LANES), lambda c: (0, 0))
    return pl.pallas_call(
        body, name=name, grid=(nc,),
        in_specs=[pl.BlockSpec((Q, LANES), lambda c: (c, dm.dt_off // LANES)), small, lane_blk,
                  pl.BlockSpec((G, 1, SUBLANES, Q), lambda c: (0, c, 0, 0)), lane_blk,
                  pl.BlockSpec((G, SUBLANES, LANES), lambda c: (0, 0, 0))],
        out_specs=[pl.BlockSpec((Q, LANES), lambda c: (c, 0)), small],
        out_shape=[SDS((dm.L, LANES), BF16), SDS((SUBLANES, LANES), F32)],
        compiler_params=_cp(("arbitrary",), 8 << 20))(proj, par_all, dac, span, ddt, gs)


def _attn_tile(dm):
    return _pick(dm.L, 256, LANES)


def attn_fwd(proj, dm, name):
    L, T, DH, AH = dm.L, _attn_tile(dm), SBA_HEAD_DIM, dm.AH
    nq = L // T
    scale = 1.0 / math.sqrt(DH)
    NT = (_DOT_DIMS["nt"], ((), ()))

    def body(q_ref, k_ref, v_ref, o_ref, tot_ref, nb_ref, ks, vs, o_scr, acc_scr):
        h, i = pl.program_id(0), pl.program_id(1)

        @pl.when(i == 0)
        def _():
            ks[...] = k_ref[...].astype(BF16)
            vs[...] = v_ref[...].astype(BF16)

        qb = q_ref[...].astype(BF16)
        rows = lax.broadcasted_iota(jnp.int32, (T, T), 0)
        cols = lax.broadcasted_iota(jnp.int32, (T, T), 1)
        causal = cols < rows
        u_rev = (rows >= cols).astype(BF16)

        def scores(j, masked):
            sl = pl.ds(pl.multiple_of(j * T, T), T)
            z = lax.dot_general(qb, ks[sl, :], NT, preferred_element_type=F32) * scale
            sp = _softplus(z)
            if masked:
                sp = jnp.where(causal, sp, 0.0)
            cs = jnp.dot(sp.astype(BF16), u_rev, preferred_element_type=F32)
            return sl, z, cs

        def weighted(blk, acc, masked):
            sl, z, cs = blk
            w = jnp.exp(z - cs - acc)
            if masked:
                w = jnp.where(causal, w, 0.0)
            return jnp.dot(w.astype(BF16), vs[sl, :], preferred_element_type=F32), acc + cs[:, 0:1]

        zero = jnp.zeros((T, 1), F32)

        @pl.when(i == 0)
        def _():
            o_scr[...], acc_scr[...] = weighted(scores(i, True), zero, True)

        @pl.when(i > 0)
        def _():
            diag, prev = scores(i, True), scores(i - 1, False)
            pv0, acc1 = weighted(diag, zero, True)
            pv1, acc2 = weighted(prev, acc1, False)
            o_scr[...] = pv0 + pv1
            acc_scr[...] = acc2

        def cond(c):
            return jnp.logical_and(c[0] >= 0, c[1] < SKIP_SUM)

        def loop(c):
            pv, acc = weighted(scores(c[0], False), acc_scr[...], False)
            o_scr[...] += pv
            acc_scr[...] = acc
            return c[0] - 1, jnp.min(acc)

        j_end, _ = lax.while_loop(cond, loop, (jnp.where(i > 0, i - 2, -1), jnp.min(acc_scr[...])))
        o_ref[...] = o_scr[...]
        tot_ref[0] = jnp.broadcast_to(acc_scr[...], (T, LANES))
        nb_ref[h, i] = i - j_end

    kv = lambda off: pl.BlockSpec((L, DH), lambda h, i: (0, off // DH + h))
    est = 2 * 2 * L * DH * 4 + 2 * L * DH * 2 + 12 * T * T * 4
    return pl.pallas_call(
        body, name=name, grid=(AH, nq),
        in_specs=[pl.BlockSpec((T, DH), lambda h, i: (i, dm.q_off // DH + h)), kv(dm.k_off), kv(dm.v_off)],
        out_specs=[pl.BlockSpec((T, DH), lambda h, i: (i, h)),
                   pl.BlockSpec((1, T, LANES), lambda h, i: (h, i, 0)),
                   pl.BlockSpec(memory_space=pltpu.SMEM)],
        out_shape=[SDS((L, dm.AW), F32), SDS((AH, L, LANES), F32), SDS((AH, nq), jnp.int32)],
        scratch_shapes=[pltpu.VMEM((L, DH), BF16), pltpu.VMEM((L, DH), BF16),
                        pltpu.VMEM((T, DH), F32), pltpu.VMEM((T, 1), F32)],
        compiler_params=_cp(("arbitrary", "arbitrary"), est))(proj, proj, proj)


def attn_bwd(proj, dyo, tot, nblk, dm, name):
    L, T, DH, AH = dm.L, _attn_tile(dm), SBA_HEAD_DIM, dm.AH
    nq = L // T
    scale = 1.0 / math.sqrt(DH)
    NT, TN = (_DOT_DIMS["nt"], ((), ())), (_DOT_DIMS["tn"], ((), ()))

    def body(nb_ref, q_ref, k_ref, v_ref, do_ref, tot_ref, dq_ref, dk_out, dv_out,
             ks, vs, dq_scr, p_scr, e_scr, dk_ref, dv_ref):
        h, i = pl.program_id(0), pl.program_id(1)

        @pl.when(i == 0)
        def _():
            ks[...] = k_ref[...].astype(BF16)
            vs[...] = v_ref[...].astype(BF16)
            dk_ref[...] = jnp.zeros_like(dk_ref)
            dv_ref[...] = jnp.zeros_like(dv_ref)

        qb = q_ref[...].astype(BF16)
        dob = do_ref[...].astype(BF16)
        tot_c = tot_ref[0][:, 0:1]
        rows = lax.broadcasted_iota(jnp.int32, (T, T), 0)
        cols = lax.broadcasted_iota(jnp.int32, (T, T), 1)
        causal = cols < rows
        u_fwd = (rows <= cols).astype(BF16)
        dq_scr[...] = jnp.zeros_like(dq_scr)
        p_scr[...] = jnp.zeros_like(p_scr)
        e_scr[...] = jnp.zeros_like(e_scr)

        def blocks(js, masks, before, e_before):
            sls = [pl.ds(pl.multiple_of(j * T, T), T) for j in js]
            zs = [lax.dot_general(qb, ks[sl, :], NT, preferred_element_type=F32) * scale for sl in sls]
            sps = [_softplus(z) for z in zs]
            sps = [jnp.where(causal, sp, 0.0) if m else sp for sp, m in zip(sps, masks)]
            spbs = [sp.astype(BF16) for sp in sps]
            pins = [jnp.dot(spb, u_fwd, preferred_element_type=F32) for spb in spbs]
            dws = [lax.dot_general(dob, vs[sl, :], NT, preferred_element_type=F32) for sl in sls]
            ws = []
            for z, spb, pin, m in zip(zs, spbs, pins, masks):
                cs = (tot_c - before) - (pin - spb.astype(F32))
                w = jnp.exp(z - cs)
                ws.append(jnp.where(causal, w, 0.0) if m else w)
                before = before + pin[:, T - 1:T]
            es = [dw * w for dw, w in zip(dws, ws)]
            fins = [jnp.dot(e.astype(BF16), u_fwd, preferred_element_type=F32) for e in es]
            dzbs = []
            for z, sp, e, fin, m in zip(zs, sps, es, fins, masks):
                dz = (e - jnp.exp(z - sp) * (e_before + fin)) * scale
                dzbs.append((jnp.where(causal, dz, 0.0) if m else dz).astype(BF16))
                e_before = e_before + fin[:, T - 1:T]
            dqs = [jnp.dot(dzb, ks[sl, :], preferred_element_type=F32) for dzb, sl in zip(dzbs, sls)]
            dks = [lax.dot_general(dzb, qb, TN, preferred_element_type=F32) for dzb in dzbs]
            dvs = [lax.dot_general(w.astype(BF16), dob, TN, preferred_element_type=F32) for w in ws]
            for sl, dk, dv in zip(sls, dks, dvs):
                dk_ref[sl, :] += dk
                dv_ref[sl, :] += dv
            return sum(dqs[1:], dqs[0]), before, e_before

        def loop(j, carry):
            dq, p_scr[...], e_scr[...] = blocks([j], [False], p_scr[...], e_scr[...])
            dq_scr[...] += dq
            return carry

        lax.fori_loop(i - nb_ref[h, i] + 1, i - 1, loop, 0)

        @pl.when(i == 0)
        def _():
            dq, _, _ = blocks([i], [True], p_scr[...], e_scr[...])
            dq_ref[...] = (dq_scr[...] + dq).astype(BF16)

        @pl.when(i > 0)
        def _():
            dq, _, _ = blocks([i - 1, i], [False, True], p_scr[...], e_scr[...])
            dq_ref[...] = (dq_scr[...] + dq).astype(BF16)

        @pl.when(i == nq - 1)
        def _():
            dk_out[...] = dk_ref[...].astype(BF16)
            dv_out[...] = dv_ref[...].astype(BF16)

    kv = lambda off: pl.BlockSpec((L, DH), lambda h, i, nb: (0, off // DH + h))
    qblk = lambda off: pl.BlockSpec((T, DH), lambda h, i, nb: (i, off // DH + h))
    acc = pl.BlockSpec((L, DH), lambda h, i, nb: (0, h))
    est = 2 * 2 * L * DH * 4 * 2 + 2 * L * DH * 2 + 16 * T * T * 4
    grid_spec = pltpu.PrefetchScalarGridSpec(
        num_scalar_prefetch=1, grid=(AH, nq),
        in_specs=[qblk(dm.q_off), kv(dm.k_off), kv(dm.v_off), qblk(dm.SW),
                  pl.BlockSpec((1, T, LANES), lambda h, i, nb: (h, i, 0))],
        out_specs=[qblk(0), acc, acc],
        scratch_shapes=[pltpu.VMEM((L, DH), BF16), pltpu.VMEM((L, DH), BF16),
                        pltpu.VMEM((T, DH), F32), pltpu.VMEM((T, 1), F32), pltpu.VMEM((T, 1), F32),
                        pltpu.VMEM((L, DH), F32), pltpu.VMEM((L, DH), F32)])
    return pl.pallas_call(
        body, name=name, grid_spec=grid_spec,
        out_shape=[SDS((L, dm.AW), BF16)] * 3,
        compiler_params=_cp(("arbitrary", "arbitrary"), est))(nblk, proj, proj, proj, dyo, tot)


def _gate_specs(dm, tm, order):
    GW, G = dm.GW, dm.G
    ix = (lambda a, b: (a, b)) if order == "ij" else (lambda a, b: (b, a))

    def spec(colfn):
        return pl.BlockSpec((tm, GW), lambda p0, p1: (ix(p0, p1)[0], colfn(ix(p0, p1)[1])))

    y_spec = spec(lambda j: jnp.minimum(j, G - 1))
    o_spec = spec(lambda j: jnp.maximum(j - G, 0))
    zg_spec = spec(lambda j: jnp.where(j < G, j, dm.g_off // GW + j - G))
    w_spec = pl.BlockSpec((1, GW), lambda p0, p1: (0, jnp.minimum(ix(p0, p1)[1], G - 1)))
    full = spec(lambda j: j)
    return y_spec, o_spec, zg_spec, w_spec, full


def gate_fwd(y, o, proj, snw, dm, name):
    L, GW, G = dm.L, dm.GW, dm.G
    tm = _pick(L, 1024, SUBLANES)
    ncol = (dm.SW + dm.AW) // GW
    y_spec, o_spec, zg_spec, w_spec, full = _gate_specs(dm, tm, "ij")

    def body(y_ref, o_ref, zg_ref, w_ref, m_ref):
        j = pl.program_id(1)
        zg = zg_ref[...]
        gate = zg * _sigmoid(zg)

        @pl.when(j < G)
        def _():
            yz = y_ref[...] * gate
            r = lax.rsqrt(jnp.mean(yz * yz, axis=-1, keepdims=True) + EPS)
            m_ref[...] = (yz * r * w_ref[...]).astype(BF16)

        @pl.when(j >= G)
        def _():
            m_ref[...] = (o_ref[...] * gate).astype(BF16)

    return pl.pallas_call(
        body, name=name, grid=(L // tm, ncol),
        in_specs=[y_spec, o_spec, zg_spec, w_spec], out_specs=full,
        out_shape=SDS((L, dm.SW + dm.AW), BF16),
        compiler_params=_cp(("arbitrary", "arbitrary"), 2 * tm * GW * 16))(y, o, proj, snw)


def gate_bwd(dmix, y, o, proj, snw, dm, name):
    L, GW, G = dm.L, dm.GW, dm.G
    tm = _pick(L, 1024, SUBLANES)
    W = dm.SW + dm.AW
    y_spec, o_spec, zg_spec, w_spec, full = _gate_specs(dm, tm, "ji")

    def body(d_ref, y_ref, o_ref, zg_ref, w_ref, dyo_ref, dzg_ref, dw_ref):
        j = pl.program_id(0)

        @pl.when(pl.program_id(1) == 0)
        def _():
            dw_ref[...] = jnp.zeros_like(dw_ref)

        zg, d = zg_ref[...], d_ref[...].astype(F32)
        sg = _sigmoid(zg)
        gate = zg * sg
        dgate = sg * (1.0 + zg * (1.0 - sg))

        @pl.when(j < G)
        def _():
            yv = y_ref[...]
            yz = yv * gate
            r = lax.rsqrt(jnp.mean(yz * yz, axis=-1, keepdims=True) + EPS)
            nrm = yz * r
            dw_ref[0:1, :] += jnp.sum(d * nrm, axis=0, keepdims=True)
            dn = d * w_ref[...]
            dyz = r * (dn - nrm * jnp.mean(dn * nrm, axis=-1, keepdims=True))
            dyo_ref[...] = (dyz * gate).astype(BF16)
            dzg_ref[...] = (dyz * yv * dgate).astype(BF16)

        @pl.when(j >= G)
        def _():
            dyo_ref[...] = (d * gate).astype(BF16)
            dzg_ref[...] = (d * o_ref[...] * dgate).astype(BF16)

    return pl.pallas_call(
        body, name=name, grid=(W // GW, L // tm),
        in_specs=[full, y_spec, o_spec, zg_spec, w_spec],
        out_specs=[full, full, pl.BlockSpec((SUBLANES, GW), lambda j, i: (0, j))],
        out_shape=[SDS((L, W), BF16), SDS((L, W), BF16), SDS((SUBLANES, W), F32)],
        compiler_params=_cp(("arbitrary", "arbitrary"), 2 * tm * GW * 24))(dmix, y, o, proj, snw)


def adamw(parts, w, m, v, name):
    R, C = w.shape
    n_slot = parts.shape[0]
    tr = _pick(R, max(SUBLANES, (1 << 18) // C // SUBLANES * SUBLANES), SUBLANES)
    c1, c2 = 1.0 - ADAM_B1 ** ADAM_STEP, 1.0 - ADAM_B2 ** ADAM_STEP

    def body(p_ref, w_ref, m_ref, v_ref, g_ref, d_ref, m2_ref, v2_ref):
        g = p_ref[0].astype(F32)
        for s in range(1, n_slot):
            g = g + p_ref[s].astype(F32)
        m2 = ADAM_B1 * m_ref[...] + (1.0 - ADAM_B1) * g
        v2 = ADAM_B2 * v_ref[...] + (1.0 - ADAM_B2) * (g * g)
        g_ref[...] = g
        m2_ref[...] = m2
        v2_ref[...] = v2
        d_ref[...] = -ADAM_LR * ((m2 / c1) / (jnp.sqrt(v2 / c2) + ADAM_EPS) + ADAM_WD * w_ref[...])

    blk = pl.BlockSpec((tr, C), lambda i: (i, 0))
    return pl.pallas_call(
        body, name=name, grid=(R // tr,),
        in_specs=[pl.BlockSpec((n_slot, tr, C), lambda i: (0, i, 0)), blk, blk, blk],
        out_specs=[blk] * 4, out_shape=[SDS((R, C), F32)] * 4,
        compiler_params=_cp(("arbitrary",), 2 * tr * C * (n_slot * 4 + 28)))(parts, w, m, v)


N_CHIP = 4


def _place():
    x, y, c = lax.axis_index("x"), lax.axis_index("y"), lax.axis_index("c")
    return x, y, c, [(1 - x, y), (x, 1 - y), (1 - x, 1 - y)]


def _comm_call(body, arrays, out_shape, n_sems, n_local, name):
    hbm = pl.BlockSpec(memory_space=pltpu.HBM)
    return pl.pallas_call(
        body, name=name, in_specs=[hbm] * len(arrays), out_specs=[hbm] * len(out_shape), out_shape=out_shape,
        scratch_shapes=[pltpu.SemaphoreType.DMA((n_sems,)), pltpu.SemaphoreType.DMA((n_sems,)),
                        pltpu.SemaphoreType.DMA((n_local,))],
        compiler_params=pltpu.CompilerParams(has_side_effects=True))(*arrays)


def gather_weights(arrays, name):
    n, per = len(arrays), N_DEV - 1

    def body(*refs):
        srcs, dsts = refs[:n], refs[n:2 * n]
        send_sems, recv_sems, local_sems = refs[2 * n:]
        start, finish = _gather_halves(srcs, dsts, send_sems, recv_sems, local_sems)
        start()
        finish()

    out_shape = [SDS((N_DEV,) + a.shape, a.dtype) for a in arrays]
    return _comm_call(body, arrays, out_shape, n * per, n, name)


def _gather_halves(srcs, dsts, send_sems, recv_sems, local_sems):
    n, per = len(srcs), N_DEV - 1

    def parts():
        x, y, c, chips = _place()
        me, sib = 4 * x + 2 * y + c, (x, y, 1 - c)

        def cp(a, k, block, to, src=None):
            return pltpu.make_async_remote_copy(
                src_ref=dsts[a].at[block] if src is None else src, dst_ref=dsts[a].at[block],
                send_sem=send_sems.at[a * per + k], recv_sem=recv_sems.at[a * per + k],
                device_id=to, device_id_type=MESH)

        own = [pltpu.make_async_copy(srcs[a], dsts[a].at[me], local_sems.at[a]) for a in range(n)]
        first = []
        for a in range(n):
            first.append(cp(a, 0, me, sib, src=srcs[a]))
            first += [cp(a, 1 + j, me, (px, py, c), src=srcs[a]) for j, (px, py) in enumerate(chips)]
        return x, y, c, chips, sib, cp, own, first

    def start():
        *_, own, first = parts()
        for o in own:
            o.start()
        for f in first:
            f.start()

    def finish():
        x, y, c, chips, sib, cp, own, first = parts()
        passed = []
        for j, (px, py) in enumerate(chips):
            block = 4 * px + 2 * py + c
            for a in range(n):
                cp(a, 1 + j, block, sib).wait_recv()
                fwd = cp(a, 4 + j, block, sib)
                fwd.start()
                passed.append(fwd)
        for a in range(n):
            cp(a, 0, 4 * x + 2 * y + 1 - c, sib).wait_recv()
            for j, (px, py) in enumerate(chips):
                cp(a, 4 + j, 4 * px + 2 * py + 1 - c, sib).wait_recv()
        for f in first + passed:
            f.wait_send()
        for o in own:
            o.wait()

    return start, finish


def gather_beside(shards):
    shards = list(shards)
    return Exchange(arrays=shards, out_shape=[SDS((N_DEV,) + s.shape, s.dtype) for s in shards],
                    n_sems=len(shards) * (N_DEV - 1), n_local=len(shards), halves=_gather_halves)


def pair_exchange(arrays, name):
    n = len(arrays)

    def body(*refs):
        srcs, dsts = refs[:n], refs[n:2 * n]
        send_sems, recv_sems, _ = refs[2 * n:]
        x, y, c, _chips = _place()
        sib = (x, y, 1 - c)

        def cp(a, k):
            return pltpu.make_async_remote_copy(
                src_ref=srcs[a].at[2 * k + 1 - c], dst_ref=dsts[a].at[k],
                send_sem=send_sems.at[a * N_CHIP + k], recv_sem=recv_sems.at[a * N_CHIP + k],
                device_id=sib, device_id_type=MESH)

        cps = [cp(a, k) for k in range(N_CHIP) for a in range(n)]
        for p in cps:
            p.start()
        for p in cps:
            p.wait_recv()
        for p in cps:
            p.wait_send()

    out_shape = [SDS((N_CHIP,) + a.shape[1:], a.dtype) for a in arrays]
    return _comm_call(body, arrays, out_shape, n * N_CHIP, 1, name)


def pair_add(parts, got, name):
    _, R, C = parts.shape
    tr = _pick(R, max(16, (1 << 19) // C // 16 * 16), 16)
    core = lax.axis_index("c").astype(jnp.int32).reshape(1)

    def body(c_ref, p_ref, g_ref, o_ref):
        o_ref[...] = (p_ref[...].astype(F32) + g_ref[...].astype(F32)).astype(o_ref.dtype)

    grid_spec = pltpu.PrefetchScalarGridSpec(
        num_scalar_prefetch=1, grid=(N_CHIP, R // tr),
        in_specs=[pl.BlockSpec((1, tr, C), lambda k, i, c_ref: (2 * k + c_ref[0], i, 0)),
                  pl.BlockSpec((1, tr, C), lambda k, i, c_ref: (k, i, 0))],
        out_specs=pl.BlockSpec((1, tr, C), lambda k, i, c_ref: (k, i, 0)))
    return pl.pallas_call(
        body, name=name, grid_spec=grid_spec, out_shape=SDS((N_CHIP, R, C), parts.dtype),
        compiler_params=_cp(("arbitrary", "arbitrary"), 2 * 3 * tr * C * 2 + 3 * tr * C * 4))(core, parts, got)


def chip_exchange_spec(sums, full=(), split=()):
    ns, nf, nsp = len(sums), len(full), len(split)
    n_sem = 3 * ns + (N_DEV - 1) * (nf + nsp)

    def copies(srcs, dsts, send_sems, recv_sems, local_sems):
        x, y, c, chips = _place()
        me, my_chip = 4 * x + 2 * y + c, 2 * x + y
        started, arrivals, own = [], [], []
        for a in range(ns):
            own.append(pltpu.make_async_copy(srcs[a].at[my_chip], dsts[a].at[my_chip], local_sems.at[a]))
            for j, (px, py) in enumerate(chips):
                k = 2 * px + py
                sem = 3 * a + j
                started.append(pltpu.make_async_remote_copy(
                    src_ref=srcs[a].at[k], dst_ref=dsts[a].at[my_chip],
                    send_sem=send_sems.at[sem], recv_sem=recv_sems.at[sem],
                    device_id=(px, py, c), device_id_type=MESH))
                arrivals.append(dict(
                    src_ref=srcs[a].at[my_chip], dst_ref=dsts[a].at[k],
                    send_sem=send_sems.at[sem], recv_sem=recv_sems.at[sem],
                    device_id=(px, py, c), device_id_type=MESH))
        for b in range(nf + nsp):
            a = ns + b
            is_split = b >= nf
            own.append(pltpu.make_async_copy(srcs[a].at[me] if is_split else srcs[a], dsts[a].at[me],
                                             local_sems.at[a]))
            for rel in range(1, N_DEV):
                px, py, pc = x ^ ((rel >> 2) & 1), y ^ ((rel >> 1) & 1), c ^ (rel & 1)
                pidx = 4 * px + 2 * py + pc
                sem = 3 * ns + b * (N_DEV - 1) + rel - 1
                started.append(pltpu.make_async_remote_copy(
                    src_ref=srcs[a].at[pidx] if is_split else srcs[a], dst_ref=dsts[a].at[me],
                    send_sem=send_sems.at[sem], recv_sem=recv_sems.at[sem],
                    device_id=(px, py, pc), device_id_type=MESH))
                arrivals.append(dict(
                    src_ref=srcs[a].at[me] if is_split else srcs[a], dst_ref=dsts[a].at[pidx],
                    send_sem=send_sems.at[sem], recv_sem=recv_sems.at[sem],
                    device_id=(px, py, pc), device_id_type=MESH))
        return own, started, arrivals

    def halves(*refs):
        def start():
            own, started, _ = copies(*refs)
            for o in own:
                o.start()
            for s in started:
                s.start()

        def finish():
            own, started, arrivals = copies(*refs)
            for r in arrivals:
                pltpu.make_async_remote_copy(**r).wait_recv()
            for s in started:
                s.wait_send()
            for o in own:
                o.wait()

        return start, finish

    out_shape = ([SDS(a.shape, a.dtype) for a in sums] + [SDS((N_DEV,) + a.shape, a.dtype) for a in full]
                 + [SDS(a.shape, a.dtype) for a in split])
    return Exchange(arrays=list(sums) + list(full) + list(split), out_shape=out_shape, n_sems=n_sem,
                    n_local=ns + nf + nsp, halves=halves)


def chip_exchange(sums, full, split, name):
    ex = chip_exchange_spec(sums, full, split)
    n = len(ex.arrays)

    def body(*refs):
        start, finish = ex.halves(refs[:n], refs[n:2 * n], *refs[2 * n:])
        start()
        finish()

    return _comm_call(body, ex.arrays, ex.out_shape, ex.n_sems, ex.n_local, name)


class LayerParams(NamedTuple):
    nw: jax.Array
    w_in: jax.Array
    cw: jax.Array
    cb: jax.Array
    par: jax.Array
    par_all: jax.Array
    snw: jax.Array
    w_out: jax.Array


def head_params(dt_bias, a_log, d_skip, dm):
    rows = jnp.stack([dt_bias, a_log, d_skip])
    par_all = jnp.pad(rows, ((0, SUBLANES - 3), (0, LANES - dm.NH)))
    par = jnp.pad(rows.reshape(3, dm.G, dm.R).transpose(1, 0, 2), ((0, 0), (0, SUBLANES - 3), (0, LANES - dm.R)))
    return par, par_all


def layer_fwd(x, p, dm, tag, next_shards=None):
    h = rms_fwd(x, p.nw, f"rms_fwd{tag}")
    gathered = None
    if next_shards is None:
        proj = mm(h, p.w_in, "nn", tm=512, tn=1920, tk=dm.D, name=f"in_proj{tag}", b_outer=True)
    else:
        proj, gathered = mm(h, p.w_in, "nn", tm=512, tn=1920, tk=dm.D, name=f"in_proj_gather{tag}", b_outer=True,
                            exchange=gather_beside(next_shards))
    xa = conv_fwd(proj, p.cw, p.cb, dm, f"conv_fwd{tag}")
    prep = ssd_prep(proj, p.par_all, dm, f"ssd_prep{tag}")
    y, hprev = ssd_fwd(xa, prep, p.par, dm, f"ssd_fwd{tag}")
    o, tot, nblk = attn_fwd(proj, dm, f"attn_fwd{tag}")
    mix = gate_fwd(y, o, proj, p.snw, dm, f"gate_fwd{tag}")
    xn = mm(mix, p.w_out, "nn", tm=512, tn=1024, tk=dm.SW + dm.AW, name=f"out_proj{tag}", res=x)
    return xn, (x, h, proj, xa, prep, y, hprev, o, tot, nblk, mix), gathered


def layer_bwd(dxn, saved, p, dm, tag, exchange=None):
    x, h, proj, xa, prep, y, hprev, o, tot, nblk, mix = saved
    dmix = mm(dxn, p.w_out, "nt", tm=512, tn=1024, tk=dm.D, name=f"d_mix{tag}", out_dtype=BF16)
    dw_out = mm(mix, dxn, "tn", tm=1024, tn=dm.D, tk=512, name=f"dw_out{tag}", out_dtype=BF16)
    dyo, dzg, dsnw = gate_bwd(dmix, y, o, proj, p.snw, dm, f"gate_bwd{tag}")
    dq, dk, dv = attn_bwd(proj, dyo, tot, nblk, dm, f"attn_bwd{tag}")
    dxs, db, dc, dac, span, ddtx, gsk = ssd_bwd(xa, prep, p.par, hprev, dyo, dm, f"ssd_bwd{tag}")
    ddt_blk, ghead = ssd_post(proj, p.par_all, dac, span, ddtx, gsk, dm, f"ssd_post{tag}")
    dcv, gconv = conv_bwd_pre(proj, dxs, db, dc, p.cw, p.cb, dm, f"conv_bwd_pre{tag}")
    dxbc = conv_bwd_in(dcv, p.cw, dm, f"conv_bwd_in{tag}")
    dproj = jnp.concatenate([dzg[:, :dm.SW], dxbc, dq, dk, dv, dzg[:, dm.SW:], ddt_blk], axis=1)
    brought = None
    if exchange is None:
        dh = mm(dproj, p.w_in, "nt", tm=512, tn=dm.D, tk=1920, name=f"d_h{tag}")
    else:
        dh, brought = mm(dproj, p.w_in, "nt", tm=512, tn=dm.D, tk=1920, name=f"d_h_exchange{tag}", exchange=exchange)
    dw_in = mm(h, dproj, "tn", tm=dm.D, tn=960, tk=2048, name=f"dw_in{tag}", out_dtype=BF16)
    dx, dnw = rms_bwd(dh, x, p.nw, dxn, f"rms_bwd{tag}")
    small = dict(norm_w=dnw[0], conv_w=gconv[:SSD_CONV], conv_b=gconv[SSD_CONV],
                 dt_bias=ghead[0, :dm.NH], a_log=ghead[1, :dm.NH], d_skip=ghead[2, :dm.NH],
                 ssd_norm_w=dsnw[0, :dm.SW])
    return dx, dw_in, dw_out, small, brought


SMALL = ("norm_w", "conv_b", "dt_bias", "a_log", "d_skip", "ssd_norm_w")


def _to_mine(w, dm):
    a, b = dm.SW + dm.CD, dm.SW + dm.CD + dm.NH
    pad = jnp.zeros((w.shape[0], LANES - dm.NH), w.dtype)
    return jnp.concatenate([w[:, :a], w[:, b:], w[:, a:b], pad], axis=1)


def _from_mine(w, dm):
    a = dm.SW + dm.CD
    return jnp.concatenate([w[:, :a], w[:, dm.dt_off:dm.dt_off + dm.NH], w[:, a:dm.dt_off]], axis=1)


def weights_to_mine(g_in, dm, name):
    _, D, ncol = g_in.shape
    tm = _pick(D, 256, 16)

    def body(g_ref, o_ref):
        full = jnp.concatenate([g_ref[j] for j in range(N_DEV)], axis=1)
        o_ref[...] = _to_mine(full, dm)

    return pl.pallas_call(
        body, name=name, grid=(D // tm,),
        in_specs=[pl.BlockSpec((N_DEV, tm, ncol), lambda i: (0, i, 0))],
        out_specs=pl.BlockSpec((tm, dm.NP), lambda i: (i, 0)), out_shape=SDS((D, dm.NP), g_in.dtype),
        compiler_params=_cp(("arbitrary",), 6 * tm * dm.NP * 2))(g_in)


def grads_from_mine(gw, dm, ncol, name):
    D = gw.shape[0]
    tm = _pick(D, 256, 16)

    def body(g_ref, o_ref):
        full = _from_mine(g_ref[...], dm)
        for j in range(N_DEV):
            o_ref[j] = full[:, j * ncol:(j + 1) * ncol]

    return pl.pallas_call(
        body, name=name, grid=(D // tm,),
        in_specs=[pl.BlockSpec((tm, dm.NP), lambda i: (i, 0))],
        out_specs=pl.BlockSpec((N_DEV, tm, ncol), lambda i: (0, i, 0)), out_shape=SDS((N_DEV, D, ncol), gw.dtype),
        compiler_params=_cp(("arbitrary",), 6 * tm * dm.NP * 2))(gw)


def _pack(pieces):
    flat = jnp.concatenate([p.reshape(-1) for p in pieces])
    rows = -(-flat.shape[0] // LANES)
    rows = -(-rows // SUBLANES) * SUBLANES
    return jnp.pad(flat, (0, rows * LANES - flat.shape[0])).reshape(rows, LANES)


def _unpack(buf, shapes):
    flat, out, at = buf.reshape(-1), [], 0
    for s in shapes:
        n = math.prod(s)
        out.append(flat[at:at + n].reshape(s))
        at += n
    return out


def kernel(x, norm_w, w_in, conv_w, conv_b, dt_bias, a_log, d_skip, ssd_norm_w, w_out, final_norm_w, loss_target, m_norm_w, m_w_in, m_conv_w, m_conv_b, m_dt_bias, m_a_log, m_d_skip, m_ssd_norm_w, m_w_out, m_final_norm_w, v_norm_w, v_w_in, v_conv_w, v_conv_b, v_dt_bias, v_a_log, v_d_skip, v_ssd_norm_w, v_w_out, v_final_norm_w):
    depth, D = norm_w.shape
    L = x.shape[1]
    NH = dt_bias.shape[1]
    SW = NH * SSD_HEAD_DIM
    CD = conv_b.shape[1]
    dm = Dims(L=L, D=D, SW=SW, G=(CD - SW) // (2 * SSD_STATE), AW=w_out.shape[1] * N_DEV - SW)
    ncol, csh, osh = w_in.shape[2], conv_w.shape[2], w_out.shape[1]
    me = 4 * lax.axis_index("x") + 2 * lax.axis_index("y") + lax.axis_index("c")

    shards = [(w_in[l].astype(BF16), w_out[l].astype(BF16), conv_w[l]) for l in range(depth)]

    def layer_params(l, gathered):
        g_in, g_out, g_cw = gathered
        full_cw = g_cw.transpose(1, 0, 2).reshape(SSD_CONV, CD)
        par, par_all = head_params(dt_bias[l], a_log[l], d_skip[l], dm)
        return LayerParams(
            nw=norm_w[l][None], w_in=weights_to_mine(g_in, dm, "weights_to_mine"), cw=full_cw, cb=conv_b[l][None],
            par=par, par_all=par_all,
            snw=ssd_norm_w[l][None], w_out=g_out.reshape(N_DEV * osh, D))

    h = x[0]
    params, saved = [], []
    gathered = gather_weights(list(shards[0]), "gather_weights")
    for l in range(depth):
        params.append(layer_params(l, gathered))
        h, s, gathered = layer_fwd(h, params[l], dm, "", shards[l + 1] if l + 1 < depth else None)
        saved.append(s)
    dh, dfw, ls = loss_head(h, final_norm_w[None], loss_target[0], "loss_head")
    loss = lax.psum(ls[0, 0], ("x", "y", "c"))
    smalls, r_in, r_out = [None] * depth, [None] * depth, [None] * depth
    pending = None
    for l in reversed(range(depth)):
        ex = None if pending is None else chip_exchange_spec(pending)
        dh, gw_in, gw_out, smalls[l], brought = layer_bwd(dh, saved[l], params[l], dm, "", ex)
        if brought is not None:
            r_in[l + 1], r_out[l + 1] = brought
        p_in = grads_from_mine(gw_in, dm, ncol, "grads_from_mine")
        p_out = gw_out.astype(BF16).reshape(N_DEV, osh, D)
        s_in, s_out = pair_exchange([p_in, p_out], "pair_exchange")
        pending = [pair_add(p_in, s_in, "pair_add_w_in"), pair_add(p_out, s_out, "pair_add_w_out")]
    grad_x = dh[None]
    rep = [jnp.stack([s[k] for s in smalls]) for k in SMALL] + [dfw[0]]
    rep_shapes = [r.shape for r in rep]
    p_rep = _pack(rep)
    p_cw = jnp.stack([s["conv_w"] for s in smalls]).reshape(depth * SSD_CONV, N_DEV, csh).transpose(1, 0, 2)
    r_in[0], r_out[0], r_rep, r_cw = chip_exchange(pending, [p_rep], [p_cw], "chip_exchange")
    r_in, r_out = jnp.concatenate(r_in, axis=1), jnp.concatenate(r_out, axis=1)

    out_in = adamw(r_in, w_in.reshape(depth * D, ncol), m_w_in.reshape(depth * D, ncol),
                   v_w_in.reshape(depth * D, ncol), "adamw_w_in")
    out_out = adamw(r_out, w_out.reshape(depth * osh, D), m_w_out.reshape(depth * osh, D),
                    v_w_out.reshape(depth * osh, D), "adamw_w_out")
    out_cw = adamw(r_cw, conv_w.reshape(depth * SSD_CONV, csh), m_conv_w.reshape(depth * SSD_CONV, csh),
                   v_conv_w.reshape(depth * SSD_CONV, csh), "adamw_conv_w")
    rep_w = [norm_w, conv_b, dt_bias, a_log, d_skip, ssd_norm_w, final_norm_w]
    rep_m = [m_norm_w, m_conv_b, m_dt_bias, m_a_log, m_d_skip, m_ssd_norm_w, m_final_norm_w]
    rep_v = [v_norm_w, v_conv_b, v_dt_bias, v_a_log, v_d_skip, v_ssd_norm_w, v_final_norm_w]
    out_rep = adamw(r_rep, _pack(rep_w), _pack(rep_m), _pack(rep_v), "adamw_replicated")

    outs = {}
    for kind, i in (("grad", 0), ("delta", 1), ("new_m", 2), ("new_v", 3)):
        r = dict(zip(SMALL + ("final_norm_w",), _unpack(out_rep[i], rep_shapes)))
        r["w_in"] = out_in[i].reshape(w_in.shape)
        r["w_out"] = out_out[i].reshape(w_out.shape)
        r["conv_w"] = out_cw[i].reshape(conv_w.shape)
        outs[kind] = r
    order = ("norm_w", "w_in", "conv_w", "conv_b", "dt_bias", "a_log", "d_skip", "ssd_norm_w", "w_out", "final_norm_w")
    return (loss, grad_x, *[outs[k][n] for k in ("grad", "delta", "new_m", "new_v") for n in order])
```

```python
import math
from typing import NamedTuple

import jax
import jax.numpy as jnp
from jax import lax
from jax.experimental import pallas as pl
from jax.experimental.pallas import tpu as pltpu

F32, BF16 = jnp.float32, jnp.bfloat16
SDS = jax.ShapeDtypeStruct
EPS = 1e-6
LANES = 128
SUBLANES = 8
VMEM_BYTES = 64 * 2 ** 20
N_DEV = 8
SSD_HEAD_DIM = 64
SSD_STATE = 128
SSD_CONV = 4
SBA_HEAD_DIM = 128
ADAM_LR, ADAM_B1, ADAM_B2, ADAM_EPS, ADAM_WD, ADAM_STEP = 0.001, 0.9, 0.999, 1e-08, 0.01, 10
SKIP_SUM = 110.0
HIGHEST = lax.Precision.HIGHEST
MESH = pl.DeviceIdType.MESH


class Dims(NamedTuple):
    L: int
    D: int
    SW: int
    G: int
    AW: int

    @property
    def NH(self): return self.SW // SSD_HEAD_DIM
    @property
    def R(self): return self.NH // self.G
    @property
    def GW(self): return self.SW // self.G
    @property
    def CD(self): return self.SW + 2 * self.G * SSD_STATE
    @property
    def AH(self): return self.AW // SBA_HEAD_DIM
    @property
    def q_off(self): return self.SW + self.CD
    @property
    def k_off(self): return self.q_off + self.AW
    @property
    def v_off(self): return self.q_off + 2 * self.AW
    @property
    def g_off(self): return self.q_off + 3 * self.AW
    @property
    def dt_off(self): return self.q_off + 4 * self.AW
    @property
    def NP(self): return self.dt_off + LANES


def _pick(n, target, mult):
    t = (min(target, n) // mult) * mult
    while t >= mult:
        if n % t == 0:
            return t
        t -= mult
    return n


def _cp(sem, vmem_est):
    limit = int(min(max(vmem_est * 5 // 4 + (4 << 20), 32 << 20), VMEM_BYTES - (8 << 20)))
    return pltpu.CompilerParams(dimension_semantics=sem, vmem_limit_bytes=limit)


def _sigmoid(x):
    return 1.0 / (1.0 + jnp.exp(-x))


def _softplus(x):
    return jnp.maximum(x, 0.0) + jnp.log(1.0 + jnp.exp(-jnp.abs(x)))


_DOT_DIMS = {"nn": ((1,), (0,)), "nt": ((1,), (1,)), "tn": ((0,), (0,))}


class Exchange(NamedTuple):
    arrays: list
    out_shape: list
    n_sems: int
    n_local: int
    halves: object


def mm(a, b, mode, *, tm, tn, tk, name, res=None, b_outer=False, exchange=None, out_dtype=F32):
    if mode == "nn":
        (M, K), N = a.shape, b.shape[1]
    elif mode == "nt":
        (M, K), N = a.shape, b.shape[0]
    else:
        (K, M), N = a.shape, b.shape[1]
    tm, tn, tk = _pick(M, tm, LANES), _pick(N, tn, LANES), _pick(K, tk, LANES)
    nk = K // tk

    def ij(p0, p1):
        return (p1, p0) if b_outer else (p0, p1)

    if mode == "tn":
        a_spec = pl.BlockSpec((tk, tm), lambda p0, p1, k: (k, ij(p0, p1)[0]))
    else:
        a_spec = pl.BlockSpec((tm, tk), lambda p0, p1, k: (ij(p0, p1)[0], k))
    if mode == "nt":
        b_spec = pl.BlockSpec((tn, tk), lambda p0, p1, k: (ij(p0, p1)[1], k))
    else:
        b_spec = pl.BlockSpec((tk, tn), lambda p0, p1, k: (k, ij(p0, p1)[1]))
    o_spec = pl.BlockSpec((tm, tn), lambda p0, p1, k: ij(p0, p1))
    dims = (_DOT_DIMS[mode], ((), ()))

    grid = (N // tn, M // tm, nk) if b_outer else (M // tm, N // tn, nk)
    n_res = 0 if res is None else 1
    n_ex = 0 if exchange is None else len(exchange.arrays)

    def body(*refs):
        a_ref, b_ref, o_ref = refs[0], refs[1], refs[2 + n_res + n_ex]
        ids = [pl.program_id(d) for d in range(3)]
        if exchange is not None:
            srcs = refs[2 + n_res:2 + n_res + n_ex]
            dsts = refs[3 + n_res + n_ex:3 + n_res + 2 * n_ex]
            start, finish = exchange.halves(srcs, dsts, *refs[3 + n_res + 2 * n_ex:])

            @pl.when(jnp.logical_and(jnp.logical_and(ids[0] == 0, ids[1] == 0), ids[2] == 0))
            def _():
                start()

        part = lax.dot_general(a_ref[...].astype(BF16), b_ref[...].astype(BF16), dims,
                               preferred_element_type=F32)
        if res is not None:
            first = part + refs[2][...]
        else:
            first = part
        if nk == 1:
            o_ref[...] = first.astype(out_dtype)
        else:
            acc_ref = o_ref if out_dtype == F32 else refs[-1]

            @pl.when(ids[2] == 0)
            def _():
                acc_ref[...] = first

            @pl.when(ids[2] > 0)
            def _():
                acc_ref[...] += part

            if out_dtype != F32:
                @pl.when(ids[2] == nk - 1)
                def _():
                    o_ref[...] = acc_ref[...].astype(out_dtype)

        if exchange is not None:
            @pl.when(jnp.logical_and(jnp.logical_and(ids[0] == grid[0] - 1, ids[1] == grid[1] - 1),
                                     ids[2] == nk - 1))
            def _():
                finish()

    ins, specs = [a, b], [a_spec, b_spec]
    if res is not None:
        ins.append(res)
        specs.append(o_spec)
    est = 2 * (tm * tk * a.dtype.itemsize + tk * tn * b.dtype.itemsize + tm * tn * 4 * (2 if res is not None else 1))
    est += tm * tk * 2 + tk * tn * 2 + tm * tn * 4
    if exchange is None:
        scratch = [pltpu.VMEM((tm, tn), F32)] if (out_dtype != F32 and nk > 1) else []
        return pl.pallas_call(
            body, name=name, grid=grid, in_specs=specs, out_specs=o_spec,
            out_shape=SDS((M, N), out_dtype), scratch_shapes=scratch,
            compiler_params=_cp(("arbitrary", "arbitrary", "arbitrary"), est))(*ins)
    assert out_dtype == F32 or nk == 1
    hbm = pl.BlockSpec(memory_space=pltpu.HBM)
    cp = _cp(("arbitrary", "arbitrary", "arbitrary"), est)
    out = pl.pallas_call(
        body, name=name, grid=grid, in_specs=specs + [hbm] * n_ex, out_specs=[o_spec] + [hbm] * n_ex,
        out_shape=[SDS((M, N), out_dtype)] + list(exchange.out_shape),
        scratch_shapes=[pltpu.SemaphoreType.DMA((exchange.n_sems,)), pltpu.SemaphoreType.DMA((exchange.n_sems,)),
                        pltpu.SemaphoreType.DMA((exchange.n_local,))],
        compiler_params=pltpu.CompilerParams(
            dimension_semantics=cp.dimension_semantics, vmem_limit_bytes=cp.vmem_limit_bytes,
            has_side_effects=True))(*ins, *exchange.arrays)
    return out[0], out[1:]


def rms_fwd(x, nw, name):
    L, D = x.shape
    tm = _pick(L, 512, SUBLANES)

    def body(x_ref, w_ref, h_ref):
        xx = x_ref[...]
        r = lax.rsqrt(jnp.mean(xx * xx, axis=-1, keepdims=True) + EPS)
        h_ref[...] = (xx * r * w_ref[...]).astype(BF16)

    return pl.pallas_call(
        body, name=name, grid=(L // tm,),
        in_specs=[pl.BlockSpec((tm, D), lambda i: (i, 0)), pl.BlockSpec((1, D), lambda i: (0, 0))],
        out_specs=pl.BlockSpec((tm, D), lambda i: (i, 0)), out_shape=SDS((L, D), BF16),
        compiler_params=_cp(("arbitrary",), 2 * tm * D * 6))(x, nw)


def rms_bwd(dh, x, nw, dres, name):
    L, D = x.shape
    tm = _pick(L, 256, SUBLANES)

    def body(dh_ref, x_ref, w_ref, dr_ref, dx_ref, dw_ref):
        @pl.when(pl.program_id(0) == 0)
        def _():
            dw_ref[...] = jnp.zeros_like(dw_ref)

        xx, d = x_ref[...], dh_ref[...]
        r = lax.rsqrt(jnp.mean(xx * xx, axis=-1, keepdims=True) + EPS)
        xh = xx * r
        dw_ref[0:1, :] += jnp.sum(d * xh, axis=0, keepdims=True)
        dxh = d * w_ref[...]
        dx_ref[...] = dr_ref[...] + r * (dxh - xh * jnp.mean(dxh * xh, axis=-1, keepdims=True))

    row = pl.BlockSpec((tm, D), lambda i: (i, 0))
    return pl.pallas_call(
        body, name=name, grid=(L // tm,),
        in_specs=[row, row, pl.BlockSpec((1, D), lambda i: (0, 0)), row],
        out_specs=[row, pl.BlockSpec((SUBLANES, D), lambda i: (0, 0))],
        out_shape=[SDS((L, D), F32), SDS((SUBLANES, D), F32)],
        compiler_params=_cp(("arbitrary",), 2 * tm * D * 16))(dh, x, nw, dres)


def loss_head(h, fw, tgt, name):
    L, D = h.shape
    tm = _pick(L, 256, SUBLANES)

    def body(h_ref, w_ref, t_ref, dh_ref, dw_ref, ls_ref):
        @pl.when(pl.program_id(0) == 0)
        def _():
            dw_ref[...] = jnp.zeros_like(dw_ref)
            ls_ref[...] = jnp.zeros_like(ls_ref)

        xx = h_ref[...]
        r = lax.rsqrt(jnp.mean(xx * xx, axis=-1, keepdims=True) + EPS)
        xh = xx * r
        err = xh * w_ref[...] - t_ref[...]
        per_tok = jnp.mean(err * err, axis=-1, keepdims=True)
        ls_ref[...] += jnp.broadcast_to(0.5 * jnp.sum(per_tok, axis=0, keepdims=True), ls_ref.shape)
        dy = err * (1.0 / D)
        dw_ref[0:1, :] += jnp.sum(dy * xh, axis=0, keepdims=True)
        dxh = dy * w_ref[...]
        dh_ref[...] = r * (dxh - xh * jnp.mean(dxh * xh, axis=-1, keepdims=True))

    row = pl.BlockSpec((tm, D), lambda i: (i, 0))
    return pl.pallas_call(
        body, name=name, grid=(L // tm,),
        in_specs=[row, pl.BlockSpec((1, D), lambda i: (0, 0)), row],
        out_specs=[row, pl.BlockSpec((SUBLANES, D), lambda i: (0, 0)),
                   pl.BlockSpec((SUBLANES, LANES), lambda i: (0, 0))],
        out_shape=[SDS((L, D), F32), SDS((SUBLANES, D), F32), SDS((SUBLANES, LANES), F32)],
        compiler_params=_cp(("arbitrary",), 2 * tm * D * 12))(h, fw, tgt)


def _shifted(u, edge, s, back):
    n = u.shape[0]
    row = lax.broadcasted_iota(jnp.int32, (SUBLANES, u.shape[1]), 0)
    if back:
        r = pltpu.roll(u, s, 0)
        head = jnp.where(row < s, pltpu.roll(edge, s, 0), r[0:SUBLANES])
        return jnp.concatenate([head, r[SUBLANES:]], axis=0)
    r = pltpu.roll(u, n - s, 0)
    tail = jnp.where(row >= SUBLANES - s, pltpu.roll(edge, SUBLANES - s, 0), r[n - SUBLANES:])
    return jnp.concatenate([r[:n - SUBLANES], tail], axis=0)


def _conv_pre(u, prev, w, b):
    acc = b + w[SSD_CONV - 1:SSD_CONV] * u
    taps = [u]
    for s in range(1, SSD_CONV):
        us = _shifted(u, prev, s, True)
        taps.append(us)
        acc = acc + w[SSD_CONV - 1 - s:SSD_CONV - s] * us
    return acc, taps


def _conv_specs(dm, tm, tc, col0):
    rb = tm // SUBLANES
    u_spec = pl.BlockSpec((tm, tc), lambda j, i: (i, col0 + j))
    prev_spec = pl.BlockSpec((SUBLANES, tc), lambda j, i: (jnp.maximum(i * rb - 1, 0), col0 + j))
    w_spec = pl.BlockSpec((SSD_CONV, tc), lambda j, i: (0, j))
    b_spec = pl.BlockSpec((1, tc), lambda j, i: (0, j))
    return u_spec, prev_spec, w_spec, b_spec


def conv_fwd(proj, cw, cb, dm, name):
    L, CD = dm.L, dm.CD
    tm, tc = _pick(L, 1024, SUBLANES), _pick(math.gcd(CD, dm.SW), 512, LANES)
    u_spec, prev_spec, w_spec, b_spec = _conv_specs(dm, tm, tc, dm.SW // tc)

    def body(u_ref, p_ref, w_ref, b_ref, o_ref):
        prev = jnp.where(pl.program_id(1) == 0, 0.0, p_ref[...])
        c, _ = _conv_pre(u_ref[...], prev, w_ref[...], b_ref[...])
        o_ref[...] = c * _sigmoid(c)

    return pl.pallas_call(
        body, name=name, grid=(CD // tc, L // tm),
        in_specs=[u_spec, prev_spec, w_spec, b_spec],
        out_specs=pl.BlockSpec((tm, tc), lambda j, i: (i, j)), out_shape=SDS((L, CD), F32),
        compiler_params=_cp(("arbitrary", "arbitrary"), 12 * tm * tc * 4))(proj, proj, cw, cb)


def conv_bwd_pre(proj, dxs, db, dcm, cw, cb, dm, name):
    L, CD = dm.L, dm.CD
    gn = dm.G * SSD_STATE
    tm, tc = _pick(L, 1024, SUBLANES), _pick(math.gcd(gn, dm.SW), 512, LANES)
    u_spec, prev_spec, w_spec, b_spec = _conv_specs(dm, tm, tc, dm.SW // tc)
    nx, nb = dm.SW // tc, gn // tc

    def body(u_ref, p_ref, dx_ref, db_ref, dcm_ref, w_ref, b_ref, dc_ref, g_ref):
        j = pl.program_id(0)

        @pl.when(pl.program_id(1) == 0)
        def _():
            g_ref[...] = jnp.zeros_like(g_ref)

        prev = jnp.where(pl.program_id(1) == 0, 0.0, p_ref[...])
        c, taps = _conv_pre(u_ref[...], prev, w_ref[...], b_ref[...])
        sg = _sigmoid(c)
        d = jnp.where(j < nx, dx_ref[...], jnp.where(j < nx + nb, db_ref[...], dcm_ref[...]))
        dc = d * (sg * (1.0 + c * (1.0 - sg)))
        dc_ref[...] = dc
        for s in range(SSD_CONV):
            g_ref[SSD_CONV - 1 - s:SSD_CONV - s, :] += jnp.sum(dc * taps[s], axis=0, keepdims=True)
        g_ref[SSD_CONV:SSD_CONV + 1, :] += jnp.sum(dc, axis=0, keepdims=True)

    blk = pl.BlockSpec((tm, tc), lambda j, i: (i, j))
    part = lambda lo, n: pl.BlockSpec((tm, tc), lambda j, i: (i, jnp.clip(j - lo, 0, n - 1)))
    return pl.pallas_call(
        body, name=name, grid=(CD // tc, L // tm),
        in_specs=[u_spec, prev_spec, part(0, nx), part(nx, nb), part(nx + nb, nb), w_spec, b_spec],
        out_specs=[blk, pl.BlockSpec((SUBLANES, tc), lambda j, i: (0, j))],
        out_shape=[SDS((L, CD), F32), SDS((SUBLANES, CD), F32)],
        compiler_params=_cp(("arbitrary", "arbitrary"), 20 * tm * tc * 4))(proj, proj, dxs, db, dcm, cw, cb)


def conv_bwd_in(dc, cw, dm, name):
    L, CD = dm.L, dm.CD
    tm, tc = _pick(L, 1024, SUBLANES), _pick(CD, 512, LANES)
    rb, nrow = tm // SUBLANES, L // SUBLANES
    ni = L // tm

    def body(d_ref, n_ref, w_ref, o_ref):
        nxt = jnp.where(pl.program_id(1) == ni - 1, 0.0, n_ref[...])
        dc_, w = d_ref[...], w_ref[...]
        acc = w[SSD_CONV - 1:SSD_CONV] * dc_
        for s in range(1, SSD_CONV):
            acc = acc + w[SSD_CONV - 1 - s:SSD_CONV - s] * _shifted(dc_, nxt, s, False)
        o_ref[...] = acc.astype(BF16)

    blk = pl.BlockSpec((tm, tc), lambda j, i: (i, j))
    return pl.pallas_call(
        body, name=name, grid=(CD // tc, ni),
        in_specs=[blk, pl.BlockSpec((SUBLANES, tc), lambda j, i: (jnp.minimum((i + 1) * rb, nrow - 1), j)),
                  pl.BlockSpec((SSD_CONV, tc), lambda j, i: (0, j))],
        out_specs=blk, out_shape=SDS((L, CD), BF16),
        compiler_params=_cp(("arbitrary", "arbitrary"), 10 * tm * tc * 4))(dc, dc, cw)


def _ssd_chunk(dm):
    return _pick(dm.L, 512, LANES)


def _dt_parts(dtr, par):
    return _softplus(dtr + par[0:1]), -jnp.exp(par[1:2])


def _tri(Q):
    rows = lax.broadcasted_iota(jnp.int32, (Q, Q), 0)
    cols = lax.broadcasted_iota(jnp.int32, (Q, Q), 1)
    return rows, cols


def _lanes_to_group(v, g, R, axis):
    n = v.shape[axis]
    return v if g == 0 else pltpu.roll(v, n - g * R, axis)


def ssd_prep(proj, par_all, dm, name):
    Q, G, R = _ssd_chunk(dm), dm.G, dm.R
    nc = dm.L // Q

    def body(dtr_ref, par_ref, dt_ref, ac_ref, at_ref):
        dt, a = _dt_parts(dtr_ref[...], par_ref[...])
        da = dt * a
        rows, cols = _tri(Q)
        acum = jnp.dot((rows >= cols).astype(F32), da, precision=HIGHEST, preferred_element_type=F32)
        acum_t = lax.dot_general(da, (rows <= cols).astype(F32), (_DOT_DIMS["tn"], ((), ())),
                                 precision=HIGHEST, preferred_element_type=F32)
        for g in range(G):
            dt_ref[g] = _lanes_to_group(dt, g, R, 1)
            ac_ref[g] = _lanes_to_group(acum, g, R, 1)
            at_ref[g, 0] = _lanes_to_group(acum_t, g, R, 0)[0:SUBLANES]

    lane_blk = pl.BlockSpec((G, Q, LANES), lambda c: (0, c, 0))
    return pl.pallas_call(
        body, name=name, grid=(nc,),
        in_specs=[pl.BlockSpec((Q, LANES), lambda c: (c, dm.dt_off // LANES)),
                  pl.BlockSpec((SUBLANES, LANES), lambda c: (0, 0))],
        out_specs=[lane_blk, lane_blk, pl.BlockSpec((G, 1, SUBLANES, Q), lambda c: (0, c, 0, 0))],
        out_shape=[SDS((G, dm.L, LANES), F32), SDS((G, dm.L, LANES), F32), SDS((G, nc, SUBLANES, Q), F32)],
        compiler_params=_cp(("arbitrary",), 8 << 20))(proj, par_all)


def _ssd_common(xa_refs, Q):
    _, b_ref, c_ref = xa_refs
    rows, cols = _tri(Q)
    bm, cm = b_ref[...].astype(BF16), c_ref[...].astype(BF16)
    gm = lax.dot_general(cm, bm, (_DOT_DIMS["nt"], ((), ())), preferred_element_type=F32)
    return rows >= cols, bm, cm, gm


def _expand_heads(v, R):
    P = SSD_HEAD_DIM
    sel = (lax.broadcasted_iota(jnp.int32, (LANES, R * P), 0)
           == lax.broadcasted_iota(jnp.int32, (LANES, R * P), 1) // P).astype(BF16)
    hi = v.astype(BF16)
    lo = (v - hi.astype(F32)).astype(BF16)
    return jnp.dot(hi, sel, preferred_element_type=F32) + jnp.dot(lo, sel, preferred_element_type=F32)


def _ssd_head(r, Q, x, dt_x, tri, acum, acum_t, gm):
    P = SSD_HEAD_DIM
    col = jnp.broadcast_to(acum[:, r:r + 1], (Q, Q))
    row = acum_t[r:r + 1, :]
    lam = jnp.where(tri, jnp.exp(jnp.minimum(col - row, 0.0)), 0.0)
    m = gm * lam
    xh = x[:, r * P:(r + 1) * P]
    xs = xh * dt_x[:, r * P:(r + 1) * P]
    a_last = acum_t[r:r + 1, Q - 1:Q]
    return col[:, :P], lam, m, xh, xs, a_last


def _ssd_specs(dm, Q, rev):
    nc = dm.L // Q
    cc = (lambda c: nc - 1 - c) if rev else (lambda c: c)
    nb = dm.SW // SSD_STATE
    x_spec = pl.BlockSpec((Q, dm.GW), lambda g, c: (cc(c), g))
    b_spec = pl.BlockSpec((Q, SSD_STATE), lambda g, c: (cc(c), nb + g))
    c_spec = pl.BlockSpec((Q, SSD_STATE), lambda g, c: (cc(c), nb + dm.G + g))
    dt_spec = pl.BlockSpec((1, Q, LANES), lambda g, c: (g, cc(c), 0))
    at_spec = pl.BlockSpec((1, 1, SUBLANES, Q), lambda g, c: (g, cc(c), 0, 0))
    par_spec = pl.BlockSpec((1, SUBLANES, LANES), lambda g, c: (g, 0, 0))
    h_spec = pl.BlockSpec((1, 1, dm.GW, SSD_STATE), lambda g, c: (cc(c), g, 0, 0))
    return x_spec, b_spec, c_spec, dt_spec, at_spec, par_spec, h_spec


def ssd_fwd(xa, prep, par, dm, name):
    Q = _ssd_chunk(dm)
    nc, P, R = dm.L // Q, SSD_HEAD_DIM, dm.R
    x_spec, b_spec, c_spec, dt_spec, at_spec, par_spec, h_spec = _ssd_specs(dm, Q, False)

    def body(x_ref, b_ref, c_ref, dt_ref, ac_ref, at_ref, par_ref, y_ref, hp_ref, h_scr):
        @pl.when(pl.program_id(1) == 0)
        def _():
            h_scr[...] = jnp.zeros_like(h_scr)

        tri, bm, cm, gm = _ssd_common((x_ref, b_ref, c_ref), Q)
        par, dt, acum, acum_t = par_ref[0], dt_ref[0], ac_ref[0], at_ref[0, 0]
        x = x_ref[...]
        dt_x = _expand_heads(dt, R)
        hp = h_scr[...]
        hp_ref[0, 0] = hp
        for r in range(R):
            col, lam, m, xh, xs, a_last = _ssd_head(r, Q, x, dt_x, tri, acum, acum_t, gm)
            hpr = hp[r * P:(r + 1) * P, :]
            ydiag = jnp.dot(m.astype(BF16), xs.astype(BF16), preferred_element_type=F32)
            yoff = jnp.exp(col) * lax.dot_general(cm, hpr.astype(BF16), (_DOT_DIMS["nt"], ((), ())),
                                                  preferred_element_type=F32)
            dte = jnp.exp(a_last - col)
            st = lax.dot_general((xs * dte).astype(BF16), bm, (_DOT_DIMS["tn"], ((), ())),
                                 preferred_element_type=F32)
            h_scr[r * P:(r + 1) * P, :] = jnp.exp(a_last) * hpr + st
            y_ref[:, r * P:(r + 1) * P] = ydiag + yoff + par[2:3, r:r + 1] * xh

    return pl.pallas_call(
        body, name=name, grid=(dm.G, nc),
        in_specs=[x_spec, b_spec, c_spec, dt_spec, dt_spec, at_spec, par_spec],
        out_specs=[x_spec, h_spec],
        out_shape=[SDS((dm.L, dm.SW), F32), SDS((nc, dm.G, dm.GW, SSD_STATE), F32)],
        scratch_shapes=[pltpu.VMEM((dm.GW, SSD_STATE), F32)],
        compiler_params=_cp(("arbitrary", "arbitrary"), 24 << 20))(xa, xa, xa, *prep, par)


def ssd_bwd(xa, prep, par, hprev, dyo, dm, name):
    Q = _ssd_chunk(dm)
    nc, P, R, N = dm.L // Q, SSD_HEAD_DIM, dm.R, SSD_STATE
    x_spec, b_spec, c_spec, dt_spec, at_spec, par_spec, h_spec = _ssd_specs(dm, Q, True)
    bc_spec = pl.BlockSpec((Q, N), lambda g, c: (nc - 1 - c, g))
    NT, TN = (_DOT_DIMS["nt"], ((), ())), (_DOT_DIMS["tn"], ((), ()))

    def body(x_ref, b_ref, c_ref, dt_ref, ac_ref, at_ref, par_ref, hp_ref, dy_ref,
             dx_ref, db_ref, dc_ref, dac_ref, span_ref, ddt_ref, gs_ref, dh_scr):
        @pl.when(pl.program_id(1) == 0)
        def _():
            dh_scr[...] = jnp.zeros_like(dh_scr)
            gs_ref[...] = jnp.zeros_like(gs_ref)

        tri, bm, cm, gm = _ssd_common((x_ref, b_ref, c_ref), Q)
        par, dt, acum, acum_t = par_ref[0], dt_ref[0], ac_ref[0], at_ref[0, 0]
        x, dy, hp, dhn = x_ref[...], dy_ref[...].astype(F32), hp_ref[0, 0], dh_scr[...]
        dt_x = _expand_heads(dt, R)
        lane = lax.broadcasted_iota(jnp.int32, (Q, LANES), 1)
        rowi = lax.broadcasted_iota(jnp.int32, (Q, LANES), 0)
        d_acum = jnp.zeros((Q, LANES), F32)
        d_dt = jnp.zeros((Q, LANES), F32)
        d_skip = jnp.zeros((1, LANES), F32)
        dgsum = jnp.zeros((Q, Q), F32)
        dye_all, xse_all, span_rows, da_rows, ddt_rows = [], [], [], [], []
        rq = lax.broadcasted_iota(jnp.int32, (Q, Q), 0)
        cq = lax.broadcasted_iota(jnp.int32, (Q, Q), 1)
        ue = (rq < cq).astype(BF16)
        for r in range(R):
            col, lam, m, xh, xs, a_last = _ssd_head(r, Q, x, dt_x, tri, acum, acum_t, gm)
            sl = slice(r * P, (r + 1) * P)
            dyh, hpr, dhr = dy[:, sl], hp[sl, :], dhn[sl, :]
            mb, xsb, dyb = m.astype(BF16), xs.astype(BF16), dyh.astype(BF16)
            e_a = jnp.exp(col)
            dte = jnp.exp(a_last - col)
            yoff = e_a * lax.dot_general(cm, hpr.astype(BF16), NT, preferred_element_type=F32)
            bdh = dte * lax.dot_general(bm, dhr.astype(BF16), NT, preferred_element_type=F32)
            dxs = lax.dot_general(mb, dyb, TN, preferred_element_type=F32) + bdh
            dm_ = lax.dot_general(dyb, xsb, NT, preferred_element_type=F32)
            dgsum = dgsum + dm_ * lam
            dye, xse = (dyh * e_a).astype(BF16), (xs * dte).astype(BF16)
            dye_all.append(dye)
            xse_all.append(xse)
            dh_scr[sl, :] = (jnp.exp(a_last) * dhr
                             + lax.dot_general(dye, cm, TN, preferred_element_type=F32))
            before = jnp.dot((dm_ * m).astype(BF16), ue, preferred_element_type=F32)
            span_rows.append(jnp.sum(jnp.where(tri, before, 0.0), axis=0, keepdims=True))
            da_rows.append(dyh * yoff - xs * bdh)
            ddt_rows.append(dxs * xh)
            da_last = (jnp.sum(xs * bdh, axis=(0, 1), keepdims=True)
                       + jnp.exp(a_last) * jnp.sum(dhr * hpr, axis=(0, 1), keepdims=True))
            d_acum = d_acum + jnp.where(jnp.logical_and(lane == r, rowi == Q - 1), da_last, 0.0)
            d_skip = d_skip + jnp.where(lane[0:1] == r, jnp.sum(dyh * xh, axis=(0, 1), keepdims=True), 0.0)
            dx_ref[:, sl] = dxs * dt_x[:, sl] + par[2:3, r:r + 1] * dyh
        sel = (lax.broadcasted_iota(jnp.int32, (R * P, LANES), 0) // P
               == lax.broadcasted_iota(jnp.int32, (R * P, LANES), 1)).astype(BF16)

        def head_sums(rows_):
            v = jnp.concatenate(rows_, axis=1)
            hi = v.astype(BF16)
            lo = (v - hi.astype(F32)).astype(BF16)
            return (jnp.dot(hi, sel, preferred_element_type=F32)
                    + jnp.dot(lo, sel, preferred_element_type=F32))

        d_acum = d_acum + head_sums(da_rows)
        d_dt = d_dt + head_sums(ddt_rows)
        dgb = dgsum.astype(BF16)
        dye_c = jnp.concatenate(dye_all, axis=1)
        xse_c = jnp.concatenate(xse_all, axis=1)
        dc_ref[...] = (jnp.dot(dgb, bm, preferred_element_type=F32)
                       + jnp.dot(dye_c, hp.astype(BF16), preferred_element_type=F32))
        db_ref[...] = (lax.dot_general(dgb, cm, TN, preferred_element_type=F32)
                       + jnp.dot(xse_c, dhn.astype(BF16), preferred_element_type=F32))
        dac_ref[0] = d_acum
        ddt_ref[0] = d_dt
        span_ref[0, 0] = jnp.concatenate(span_rows + [jnp.zeros((SUBLANES - R, Q), F32)] * (R < SUBLANES), axis=0)
        gs_ref[0, 2:3, :] += d_skip

    return pl.pallas_call(
        body, name=name, grid=(dm.G, nc),
        in_specs=[x_spec, b_spec, c_spec, dt_spec, dt_spec, at_spec, par_spec, h_spec, x_spec],
        out_specs=[x_spec, bc_spec, bc_spec, dt_spec, at_spec, dt_spec, par_spec],
        out_shape=[SDS((dm.L, dm.SW), F32), SDS((dm.L, dm.G * N), F32), SDS((dm.L, dm.G * N), F32),
                   SDS((dm.G, dm.L, LANES), F32), SDS((dm.G, nc, SUBLANES, Q), F32),
                   SDS((dm.G, dm.L, LANES), F32), SDS((dm.G, SUBLANES, LANES), F32)],
        scratch_shapes=[pltpu.VMEM((dm.GW, N), F32)],
        compiler_params=_cp(("arbitrary", "arbitrary"), 28 << 20))(xa, xa, xa, *prep, par, hprev, dyo)


def ssd_post(proj, par_all, dac, span, ddt, gs, dm, name):
    Q, G, R = _ssd_chunk(dm), dm.G, dm.R
    nc = dm.L // Q
    NT = (_DOT_DIMS["nt"], ((), ()))

    def body(dtr_ref, par_ref, dac_ref, span_ref, ddt_ref, gs_ref, out_ref, acc_ref):
        @pl.when(pl.program_id(0) == 0)
        def _():
            acc_ref[...] = jnp.zeros_like(acc_ref)

        par = par_ref[...]
        dt, a = _dt_parts(dtr_ref[...], par)
        lane = lax.broadcasted_iota(jnp.int32, (Q, LANES), 1)
        row8 = lax.broadcasted_iota(jnp.int32, (SUBLANES, Q), 0)

        def heads(v, g):
            v = jnp.where(lane[:v.shape[0]] < R, v, 0.0)
            return v if g == 0 else pltpu.roll(v, g * R, 1)

        d_acum = sum(heads(dac_ref[g], g) for g in range(G))
        d_dtx = sum(heads(ddt_ref[g], g) for g in range(G))
        d_skip = sum(heads(gs_ref[g][2:3], g) for g in range(G))
        span_t = jnp.zeros((LANES, Q), F32)
        for g in range(G):
            rows_g = jnp.concatenate([jnp.where(row8 < R, span_ref[g, 0], 0.0),
                                      jnp.zeros((LANES - SUBLANES, Q), F32)], axis=0)
            span_t = span_t + (rows_g if g == 0 else pltpu.roll(rows_g, g * R, 0))
        rq, cq = _tri(Q)
        d_da = (jnp.dot((rq <= cq).astype(F32), d_acum, precision=HIGHEST, preferred_element_type=F32)
                + lax.dot_general((rq == cq).astype(F32), span_t, NT, precision=HIGHEST,
                                  preferred_element_type=F32))
        d_raw = (d_dtx + d_da * a) * _sigmoid(dtr_ref[...] + par[0:1])
        out_ref[...] = d_raw.astype(BF16)
        acc_ref[0:1, :] += jnp.sum(d_raw, axis=0, keepdims=True)
        acc_ref[1:2, :] += jnp.sum(d_da * dt, axis=0, keepdims=True) * a
        acc_ref[2:3, :] = d_skip

    lane_blk = pl.BlockSpec((G, Q, LANES), lambda c: (0, c, 0))
    small = pl.BlockSpec((SUBLANES, LANES), lambda c: (0, 0))
    return pl.pallas_call(
        body, name=name, grid=(nc,),
        in_specs=[pl.BlockSpec((Q, LANES), lambda c: (c, dm.dt_off // LANES)), small, lane_blk,
                  pl.BlockSpec((G, 1, SUBLANES, Q), lambda c: (0, c, 0, 0)), lane_blk,
                  pl.BlockSpec((G, SUBLANES, LANES), lambda c: (0, 0, 0))],
        out_specs=[pl.BlockSpec((Q, LANES), lambda c: (c, 0)), small],
        out_shape=[SDS((dm.L, LANES), BF16), SDS((SUBLANES, LANES), F32)],
        compiler_params=_cp(("arbitrary",), 8 << 20))(proj, par_all, dac, span, ddt, gs)


def _attn_tile(dm):
    return _pick(dm.L, 256, LANES)


def attn_fwd(proj, dm, name):
    L, T, DH, AH = dm.L, _attn_tile(dm), SBA_HEAD_DIM, dm.AH
    nq = L // T
    scale = 1.0 / math.sqrt(DH)
    NT = (_DOT_DIMS["nt"], ((), ()))

    def body(q_ref, k_ref, v_ref, o_ref, tot_ref, nb_ref, ks, vs, o_scr, acc_scr):
        h, i = pl.program_id(0), pl.program_id(1)

        @pl.when(i == 0)
        def _():
            ks[...] = k_ref[...].astype(BF16)
            vs[...] = v_ref[...].astype(BF16)

        qb = q_ref[...].astype(BF16)
        rows = lax.broadcasted_iota(jnp.int32, (T, T), 0)
        cols = lax.broadcasted_iota(jnp.int32, (T, T), 1)
        causal = cols < rows
        u_rev = (rows >= cols).astype(BF16)

        def scores(j, masked):
            sl = pl.ds(pl.multiple_of(j * T, T), T)
            z = lax.dot_general(qb, ks[sl, :], NT, preferred_element_type=F32) * scale
            sp = _softplus(z)
            if masked:
                sp = jnp.where(causal, sp, 0.0)
            cs = jnp.dot(sp.astype(BF16), u_rev, preferred_element_type=F32)
            return sl, z, cs

        def weighted(blk, acc, masked):
            sl, z, cs = blk
            w = jnp.exp(z - cs - acc)
            if masked:
                w = jnp.where(causal, w, 0.0)
            return jnp.dot(w.astype(BF16), vs[sl, :], preferred_element_type=F32), acc + cs[:, 0:1]

        zero = jnp.zeros((T, 1), F32)

        @pl.when(i == 0)
        def _():
            o_scr[...], acc_scr[...] = weighted(scores(i, True), zero, True)

        @pl.when(i > 0)
        def _():
            diag, prev = scores(i, True), scores(i - 1, False)
            pv0, acc1 = weighted(diag, zero, True)
            pv1, acc2 = weighted(prev, acc1, False)
            o_scr[...] = pv0 + pv1
            acc_scr[...] = acc2

        def cond(c):
            return jnp.logical_and(c[0] >= 0, c[1] < SKIP_SUM)

        def loop(c):
            pv, acc = weighted(scores(c[0], False), acc_scr[...], False)
            o_scr[...] += pv
            acc_scr[...] = acc
            return c[0] - 1, jnp.min(acc)

        j_end, _ = lax.while_loop(cond, loop, (jnp.where(i > 0, i - 2, -1), jnp.min(acc_scr[...])))
        o_ref[...] = o_scr[...]
        tot_ref[0] = jnp.broadcast_to(acc_scr[...], (T, LANES))
        nb_ref[h, i] = i - j_end

    kv = lambda off: pl.BlockSpec((L, DH), lambda h, i: (0, off // DH + h))
    est = 2 * 2 * L * DH * 4 + 2 * L * DH * 2 + 12 * T * T * 4
    return pl.pallas_call(
        body, name=name, grid=(AH, nq),
        in_specs=[pl.BlockSpec((T, DH), lambda h, i: (i, dm.q_off // DH + h)), kv(dm.k_off), kv(dm.v_off)],
        out_specs=[pl.BlockSpec((T, DH), lambda h, i: (i, h)),
                   pl.BlockSpec((1, T, LANES), lambda h, i: (h, i, 0)),
                   pl.BlockSpec(memory_space=pltpu.SMEM)],
        out_shape=[SDS((L, dm.AW), F32), SDS((AH, L, LANES), F32), SDS((AH, nq), jnp.int32)],
        scratch_shapes=[pltpu.VMEM((L, DH), BF16), pltpu.VMEM((L, DH), BF16),
                        pltpu.VMEM((T, DH), F32), pltpu.VMEM((T, 1), F32)],
        compiler_params=_cp(("arbitrary", "arbitrary"), est))(proj, proj, proj)


def attn_bwd(proj, dyo, tot, nblk, dm, name):
    L, T, DH, AH = dm.L, _attn_tile(dm), SBA_HEAD_DIM, dm.AH
    nq = L // T
    scale = 1.0 / math.sqrt(DH)
    NT, TN = (_DOT_DIMS["nt"], ((), ())), (_DOT_DIMS["tn"], ((), ()))

    def body(nb_ref, q_ref, k_ref, v_ref, do_ref, tot_ref, dq_ref, dk_out, dv_out,
             ks, vs, dq_scr, p_scr, e_scr, dk_ref, dv_ref):
        h, i = pl.program_id(0), pl.program_id(1)

        @pl.when(i == 0)
        def _():
            ks[...] = k_ref[...].astype(BF16)
            vs[...] = v_ref[...].astype(BF16)
            dk_ref[...] = jnp.zeros_like(dk_ref)
            dv_ref[...] = jnp.zeros_like(dv_ref)

        qb = q_ref[...].astype(BF16)
        dob = do_ref[...].astype(BF16)
        tot_c = tot_ref[0][:, 0:1]
        rows = lax.broadcasted_iota(jnp.int32, (T, T), 0)
        cols = lax.broadcasted_iota(jnp.int32, (T, T), 1)
        causal = cols < rows
        u_fwd = (rows <= cols).astype(BF16)
        dq_scr[...] = jnp.zeros_like(dq_scr)
        p_scr[...] = jnp.zeros_like(p_scr)
        e_scr[...] = jnp.zeros_like(e_scr)

        def blocks(js, masks, before, e_before):
            sls = [pl.ds(pl.multiple_of(j * T, T), T) for j in js]
            zs = [lax.dot_general(qb, ks[sl, :], NT, preferred_element_type=F32) * scale for sl in sls]
            sps = [_softplus(z) for z in zs]
            sps = [jnp.where(causal, sp, 0.0) if m else sp for sp, m in zip(sps, masks)]
            spbs = [sp.astype(BF16) for sp in sps]
            pins = [jnp.dot(spb, u_fwd, preferred_element_type=F32) for spb in spbs]
            dws = [lax.dot_general(dob, vs[sl, :], NT, preferred_element_type=F32) for sl in sls]
            ws = []
            for z, spb, pin, m in zip(zs, spbs, pins, masks):
                cs = (tot_c - before) - (pin - spb.astype(F32))
                w = jnp.exp(z - cs)
                ws.append(jnp.where(causal, w, 0.0) if m else w)
                before = before + pin[:, T - 1:T]
            es = [dw * w for dw, w in zip(dws, ws)]
            fins = [jnp.dot(e.astype(BF16), u_fwd, preferred_element_type=F32) for e in es]
            dzbs = []
            for z, sp, e, fin, m in zip(zs, sps, es, fins, masks):
                dz = (e - jnp.exp(z - sp) * (e_before + fin)) * scale
                dzbs.append((jnp.where(causal, dz, 0.0) if m else dz).astype(BF16))
                e_before = e_before + fin[:, T - 1:T]
            dqs = [jnp.dot(dzb, ks[sl, :], preferred_element_type=F32) for dzb, sl in zip(dzbs, sls)]
            dks = [lax.dot_general(dzb, qb, TN, preferred_element_type=F32) for dzb in dzbs]
            dvs = [lax.dot_general(w.astype(BF16), dob, TN, preferred_element_type=F32) for w in ws]
            for sl, dk, dv in zip(sls, dks, dvs):
                dk_ref[sl, :] += dk
                dv_ref[sl, :] += dv
            return sum(dqs[1:], dqs[0]), before, e_before

        def loop(j, carry):
            dq, p_scr[...], e_scr[...] = blocks([j], [False], p_scr[...], e_scr[...])
            dq_scr[...] += dq
            return carry

        lax.fori_loop(i - nb_ref[h, i] + 1, i - 1, loop, 0)

        @pl.when(i == 0)
        def _():
            dq, _, _ = blocks([i], [True], p_scr[...], e_scr[...])
            dq_ref[...] = (dq_scr[...] + dq).astype(BF16)

        @pl.when(i > 0)
        def _():
            dq, _, _ = blocks([i - 1, i], [False, True], p_scr[...], e_scr[...])
            dq_ref[...] = (dq_scr[...] + dq).astype(BF16)

        @pl.when(i == nq - 1)
        def _():
            dk_out[...] = dk_ref[...].astype(BF16)
            dv_out[...] = dv_ref[...].astype(BF16)

    kv = lambda off: pl.BlockSpec((L, DH), lambda h, i, nb: (0, off // DH + h))
    qblk = lambda off: pl.BlockSpec((T, DH), lambda h, i, nb: (i, off // DH + h))
    acc = pl.BlockSpec((L, DH), lambda h, i, nb: (0, h))
    est = 2 * 2 * L * DH * 4 * 2 + 2 * L * DH * 2 + 16 * T * T * 4
    grid_spec = pltpu.PrefetchScalarGridSpec(
        num_scalar_prefetch=1, grid=(AH, nq),
        in_specs=[qblk(dm.q_off), kv(dm.k_off), kv(dm.v_off), qblk(dm.SW),
                  pl.BlockSpec((1, T, LANES), lambda h, i, nb: (h, i, 0))],
        out_specs=[qblk(0), acc, acc],
        scratch_shapes=[pltpu.VMEM((L, DH), BF16), pltpu.VMEM((L, DH), BF16),
                        pltpu.VMEM((T, DH), F32), pltpu.VMEM((T, 1), F32), pltpu.VMEM((T, 1), F32),
                        pltpu.VMEM((L, DH), F32), pltpu.VMEM((L, DH), F32)])
    return pl.pallas_call(
        body, name=name, grid_spec=grid_spec,
        out_shape=[SDS((L, dm.AW), BF16)] * 3,
        compiler_params=_cp(("arbitrary", "arbitrary"), est))(nblk, proj, proj, proj, dyo, tot)


def _gate_specs(dm, tm, order):
    GW, G = dm.GW, dm.G
    ix = (lambda a, b: (a, b)) if order == "ij" else (lambda a, b: (b, a))

    def spec(colfn):
        return pl.BlockSpec((tm, GW), lambda p0, p1: (ix(p0, p1)[0], colfn(ix(p0, p1)[1])))

    y_spec = spec(lambda j: jnp.minimum(j, G - 1))
    o_spec = spec(lambda j: jnp.maximum(j - G, 0))
    zg_spec = spec(lambda j: jnp.where(j < G, j, dm.g_off // GW + j - G))
    w_spec = pl.BlockSpec((1, GW), lambda p0, p1: (0, jnp.minimum(ix(p0, p1)[1], G - 1)))
    full = spec(lambda j: j)
    return y_spec, o_spec, zg_spec, w_spec, full


def gate_fwd(y, o, proj, snw, dm, name):
    L, GW, G = dm.L, dm.GW, dm.G
    tm = _pick(L, 1024, SUBLANES)
    ncol = (dm.SW + dm.AW) // GW
    y_spec, o_spec, zg_spec, w_spec, full = _gate_specs(dm, tm, "ij")

    def body(y_ref, o_ref, zg_ref, w_ref, m_ref):
        j = pl.program_id(1)
        zg = zg_ref[...]
        gate = zg * _sigmoid(zg)

        @pl.when(j < G)
        def _():
            yz = y_ref[...] * gate
            r = lax.rsqrt(jnp.mean(yz * yz, axis=-1, keepdims=True) + EPS)
            m_ref[...] = (yz * r * w_ref[...]).astype(BF16)

        @pl.when(j >= G)
        def _():
            m_ref[...] = (o_ref[...] * gate).astype(BF16)

    return pl.pallas_call(
        body, name=name, grid=(L // tm, ncol),
        in_specs=[y_spec, o_spec, zg_spec, w_spec], out_specs=full,
        out_shape=SDS((L, dm.SW + dm.AW), BF16),
        compiler_params=_cp(("arbitrary", "arbitrary"), 2 * tm * GW * 16))(y, o, proj, snw)


def gate_bwd(dmix, y, o, proj, snw, dm, name):
    L, GW, G = dm.L, dm.GW, dm.G
    tm = _pick(L, 1024, SUBLANES)
    W = dm.SW + dm.AW
    y_spec, o_spec, zg_spec, w_spec, full = _gate_specs(dm, tm, "ji")

    def body(d_ref, y_ref, o_ref, zg_ref, w_ref, dyo_ref, dzg_ref, dw_ref):
        j = pl.program_id(0)

        @pl.when(pl.program_id(1) == 0)
        def _():
            dw_ref[...] = jnp.zeros_like(dw_ref)

        zg, d = zg_ref[...], d_ref[...].astype(F32)
        sg = _sigmoid(zg)
        gate = zg * sg
        dgate = sg * (1.0 + zg * (1.0 - sg))

        @pl.when(j < G)
        def _():
            yv = y_ref[...]
            yz = yv * gate
            r = lax.rsqrt(jnp.mean(yz * yz, axis=-1, keepdims=True) + EPS)
            nrm = yz * r
            dw_ref[0:1, :] += jnp.sum(d * nrm, axis=0, keepdims=True)
            dn = d * w_ref[...]
            dyz = r * (dn - nrm * jnp.mean(dn * nrm, axis=-1, keepdims=True))
            dyo_ref[...] = (dyz * gate).astype(BF16)
            dzg_ref[...] = (dyz * yv * dgate).astype(BF16)

        @pl.when(j >= G)
        def _():
            dyo_ref[...] = (d * gate).astype(BF16)
            dzg_ref[...] = (d * o_ref[...] * dgate).astype(BF16)

    return pl.pallas_call(
        body, name=name, grid=(W // GW, L // tm),
        in_specs=[full, y_spec, o_spec, zg_spec, w_spec],
        out_specs=[full, full, pl.BlockSpec((SUBLANES, GW), lambda j, i: (0, j))],
        out_shape=[SDS((L, W), BF16), SDS((L, W), BF16), SDS((SUBLANES, W), F32)],
        compiler_params=_cp(("arbitrary", "arbitrary"), 2 * tm * GW * 24))(dmix, y, o, proj, snw)


def adamw(parts, w, m, v, name):
    R, C = w.shape
    n_slot = parts.shape[0]
    tr = _pick(R, max(SUBLANES, (1 << 18) // C // SUBLANES * SUBLANES), SUBLANES)
    c1, c2 = 1.0 - ADAM_B1 ** ADAM_STEP, 1.0 - ADAM_B2 ** ADAM_STEP

    def body(p_ref, w_ref, m_ref, v_ref, g_ref, d_ref, m2_ref, v2_ref):
        g = p_ref[0].astype(F32)
        for s in range(1, n_slot):
            g = g + p_ref[s].astype(F32)
        m2 = ADAM_B1 * m_ref[...] + (1.0 - ADAM_B1) * g
        v2 = ADAM_B2 * v_ref[...] + (1.0 - ADAM_B2) * (g * g)
        g_ref[...] = g
        m2_ref[...] = m2
        v2_ref[...] = v2
        d_ref[...] = -ADAM_LR * ((m2 / c1) / (jnp.sqrt(v2 / c2) + ADAM_EPS) + ADAM_WD * w_ref[...])

    blk = pl.BlockSpec((tr, C), lambda i: (i, 0))
    return pl.pallas_call(
        body, name=name, grid=(R // tr,),
        in_specs=[pl.BlockSpec((n_slot, tr, C), lambda i: (0, i, 0)), blk, blk, blk],
        out_specs=[blk] * 4, out_shape=[SDS((R, C), F32)] * 4,
        compiler_params=_cp(("arbitrary",), 2 * tr * C * (n_slot * 4 + 28)))(parts, w, m, v)


N_CHIP = 4


def _place():
    x, y, c = lax.axis_index("x"), lax.axis_index("y"), lax.axis_index("c")
    return x, y, c, [(1 - x, y), (x, 1 - y), (1 - x, 1 - y)]


def _comm_call(body, arrays, out_shape, n_sems, n_local, name):
    hbm = pl.BlockSpec(memory_space=pltpu.HBM)
    return pl.pallas_call(
        body, name=name, in_specs=[hbm] * len(arrays), out_specs=[hbm] * len(out_shape), out_shape=out_shape,
        scratch_shapes=[pltpu.SemaphoreType.DMA((n_sems,)), pltpu.SemaphoreType.DMA((n_sems,)),
                        pltpu.SemaphoreType.DMA((n_local,))],
        compiler_params=pltpu.CompilerParams(has_side_effects=True))(*arrays)


def gather_weights(arrays, name):
    n, per = len(arrays), N_DEV - 1

    def body(*refs):
        srcs, dsts = refs[:n], refs[n:2 * n]
        send_sems, recv_sems, local_sems = refs[2 * n:]
        start, finish = _gather_halves(srcs, dsts, send_sems, recv_sems, local_sems)
        start()
        finish()

    out_shape = [SDS((N_DEV,) + a.shape, a.dtype) for a in arrays]
    return _comm_call(body, arrays, out_shape, n * per, n, name)


def _gather_halves(srcs, dsts, send_sems, recv_sems, local_sems):
    n, per = len(srcs), N_DEV - 1

    def parts():
        x, y, c, chips = _place()
        me, sib = 4 * x + 2 * y + c, (x, y, 1 - c)

        def cp(a, k, block, to, src=None):
            return pltpu.make_async_remote_copy(
                src_ref=dsts[a].at[block] if src is None else src, dst_ref=dsts[a].at[block],
                send_sem=send_sems.at[a * per + k], recv_sem=recv_sems.at[a * per + k],
                device_id=to, device_id_type=MESH)

        own = [pltpu.make_async_copy(srcs[a], dsts[a].at[me], local_sems.at[a]) for a in range(n)]
        first = []
        for a in range(n):
            first.append(cp(a, 0, me, sib, src=srcs[a]))
            first += [cp(a, 1 + j, me, (px, py, c), src=srcs[a]) for j, (px, py) in enumerate(chips)]
        return x, y, c, chips, sib, cp, own, first

    def start():
        *_, own, first = parts()
        for o in own:
            o.start()
        for f in first:
            f.start()

    def finish():
        x, y, c, chips, sib, cp, own, first = parts()
        passed = []
        for j, (px, py) in enumerate(chips):
            block = 4 * px + 2 * py + c
            for a in range(n):
                cp(a, 1 + j, block, sib).wait_recv()
                fwd = cp(a, 4 + j, block, sib)
                fwd.start()
                passed.append(fwd)
        for a in range(n):
            cp(a, 0, 4 * x + 2 * y + 1 - c, sib).wait_recv()
            for j, (px, py) in enumerate(chips):
                cp(a, 4 + j, 4 * px + 2 * py + 1 - c, sib).wait_recv()
        for f in first + passed:
            f.wait_send()
        for o in own:
            o.wait()

    return start, finish


def gather_beside(shards):
    shards = list(shards)
    return Exchange(arrays=shards, out_shape=[SDS((N_DEV,) + s.shape, s.dtype) for s in shards],
                    n_sems=len(shards) * (N_DEV - 1), n_local=len(shards), halves=_gather_halves)


def pair_exchange(arrays, name):
    ex = pair_exchange_spec(arrays)
    n = len(ex.arrays)

    def body(*refs):
        start, finish = ex.halves(refs[:n], refs[n:2 * n], *refs[2 * n:])
        start()
        finish()

    return _comm_call(body, ex.arrays, ex.out_shape, ex.n_sems, ex.n_local, name)


def pair_exchange_spec(arrays):
    arrays = list(arrays)
    n = len(arrays)

    def halves(srcs, dsts, send_sems, recv_sems, _local):
        def copies():
            x, y, c, _chips = _place()
            return [pltpu.make_async_remote_copy(
                src_ref=srcs[a].at[2 * k + 1 - c], dst_ref=dsts[a].at[k],
                send_sem=send_sems.at[a * N_CHIP + k], recv_sem=recv_sems.at[a * N_CHIP + k],
                device_id=(x, y, 1 - c), device_id_type=MESH) for k in range(N_CHIP) for a in range(n)]

        def start():
            for p in copies():
                p.start()

        def finish():
            cps = copies()
            for p in cps:
                p.wait_recv()
            for p in cps:
                p.wait_send()

        return start, finish

    return Exchange(arrays=arrays, out_shape=[SDS((N_CHIP,) + a.shape[1:], a.dtype) for a in arrays],
                    n_sems=n * N_CHIP, n_local=1, halves=halves)


def pair_add(parts, got, name):
    _, R, C = parts.shape
    tr = _pick(R, max(16, (1 << 19) // C // 16 * 16), 16)
    core = lax.axis_index("c").astype(jnp.int32).reshape(1)

    def body(c_ref, p_ref, g_ref, o_ref):
        o_ref[...] = (p_ref[...].astype(F32) + g_ref[...].astype(F32)).astype(o_ref.dtype)

    grid_spec = pltpu.PrefetchScalarGridSpec(
        num_scalar_prefetch=1, grid=(N_CHIP, R // tr),
        in_specs=[pl.BlockSpec((1, tr, C), lambda k, i, c_ref: (2 * k + c_ref[0], i, 0)),
                  pl.BlockSpec((1, tr, C), lambda k, i, c_ref: (k, i, 0))],
        out_specs=pl.BlockSpec((1, tr, C), lambda k, i, c_ref: (k, i, 0)))
    return pl.pallas_call(
        body, name=name, grid_spec=grid_spec, out_shape=SDS((N_CHIP, R, C), parts.dtype),
        compiler_params=_cp(("arbitrary", "arbitrary"), 2 * 3 * tr * C * 2 + 3 * tr * C * 4))(core, parts, got)


def chip_exchange_spec(sums, full=(), split=()):
    ns, nf, nsp = len(sums), len(full), len(split)
    n_sem = 3 * ns + (N_DEV - 1) * (nf + nsp)

    def copies(srcs, dsts, send_sems, recv_sems, local_sems):
        x, y, c, chips = _place()
        me, my_chip = 4 * x + 2 * y + c, 2 * x + y
        started, arrivals, own = [], [], []
        for a in range(ns):
            own.append(pltpu.make_async_copy(srcs[a].at[my_chip], dsts[a].at[my_chip], local_sems.at[a]))
            for j, (px, py) in enumerate(chips):
                k = 2 * px + py
                sem = 3 * a + j
                started.append(pltpu.make_async_remote_copy(
                    src_ref=srcs[a].at[k], dst_ref=dsts[a].at[my_chip],
                    send_sem=send_sems.at[sem], recv_sem=recv_sems.at[sem],
                    device_id=(px, py, c), device_id_type=MESH))
                arrivals.append(dict(
                    src_ref=srcs[a].at[my_chip], dst_ref=dsts[a].at[k],
                    send_sem=send_sems.at[sem], recv_sem=recv_sems.at[sem],
                    device_id=(px, py, c), device_id_type=MESH))
        for b in range(nf + nsp):
            a = ns + b
            is_split = b >= nf
            own.append(pltpu.make_async_copy(srcs[a].at[me] if is_split else srcs[a], dsts[a].at[me],
                                             local_sems.at[a]))
            for rel in range(1, N_DEV):
                px, py, pc = x ^ ((rel >> 2) & 1), y ^ ((rel >> 1) & 1), c ^ (rel & 1)
                pidx = 4 * px + 2 * py + pc
                sem = 3 * ns + b * (N_DEV - 1) + rel - 1
                started.append(pltpu.make_async_remote_copy(
                    src_ref=srcs[a].at[pidx] if is_split else srcs[a], dst_ref=dsts[a].at[me],
                    send_sem=send_sems.at[sem], recv_sem=recv_sems.at[sem],
                    device_id=(px, py, pc), device_id_type=MESH))
                arrivals.append(dict(
                    src_ref=srcs[a].at[me] if is_split else srcs[a], dst_ref=dsts[a].at[pidx],
                    send_sem=send_sems.at[sem], recv_sem=recv_sems.at[sem],
                    device_id=(px, py, pc), device_id_type=MESH))
        return own, started, arrivals

    def halves(*refs):
        def start():
            own, started, _ = copies(*refs)
            for o in own:
                o.start()
            for s in started:
                s.start()

        def finish():
            own, started, arrivals = copies(*refs)
            for r in arrivals:
                pltpu.make_async_remote_copy(**r).wait_recv()
            for s in started:
                s.wait_send()
            for o in own:
                o.wait()

        return start, finish

    out_shape = ([SDS(a.shape, a.dtype) for a in sums] + [SDS((N_DEV,) + a.shape, a.dtype) for a in full]
                 + [SDS(a.shape, a.dtype) for a in split])
    return Exchange(arrays=list(sums) + list(full) + list(split), out_shape=out_shape, n_sems=n_sem,
                    n_local=ns + nf + nsp, halves=halves)


def chip_exchange(sums, full, split, name):
    ex = chip_exchange_spec(sums, full, split)
    n = len(ex.arrays)

    def body(*refs):
        start, finish = ex.halves(refs[:n], refs[n:2 * n], *refs[2 * n:])
        start()
        finish()

    return _comm_call(body, ex.arrays, ex.out_shape, ex.n_sems, ex.n_local, name)


class LayerParams(NamedTuple):
    nw: jax.Array
    w_in: jax.Array
    cw: jax.Array
    cb: jax.Array
    par: jax.Array
    par_all: jax.Array
    snw: jax.Array
    w_out: jax.Array


def head_params(dt_bias, a_log, d_skip, dm):
    rows = jnp.stack([dt_bias, a_log, d_skip])
    par_all = jnp.pad(rows, ((0, SUBLANES - 3), (0, LANES - dm.NH)))
    par = jnp.pad(rows.reshape(3, dm.G, dm.R).transpose(1, 0, 2), ((0, 0), (0, SUBLANES - 3), (0, LANES - dm.R)))
    return par, par_all


def layer_fwd(x, p, dm, tag, next_shards=None):
    h = rms_fwd(x, p.nw, f"rms_fwd{tag}")
    gathered = None
    if next_shards is None:
        proj = mm(h, p.w_in, "nn", tm=512, tn=1920, tk=dm.D, name=f"in_proj{tag}", b_outer=True)
    else:
        proj, gathered = mm(h, p.w_in, "nn", tm=512, tn=1920, tk=dm.D, name=f"in_proj_gather{tag}", b_outer=True,
                            exchange=gather_beside(next_shards))
    xa = conv_fwd(proj, p.cw, p.cb, dm, f"conv_fwd{tag}")
    prep = ssd_prep(proj, p.par_all, dm, f"ssd_prep{tag}")
    y, hprev = ssd_fwd(xa, prep, p.par, dm, f"ssd_fwd{tag}")
    o, tot, nblk = attn_fwd(proj, dm, f"attn_fwd{tag}")
    mix = gate_fwd(y, o, proj, p.snw, dm, f"gate_fwd{tag}")
    xn = mm(mix, p.w_out, "nn", tm=512, tn=1024, tk=dm.SW + dm.AW, name=f"out_proj{tag}", res=x)
    return xn, (x, h, proj, xa, prep, y, hprev, o, tot, nblk, mix), gathered


def layer_bwd(dxn, saved, p, dm, tag, parts=None):
    x, h, proj, xa, prep, y, hprev, o, tot, nblk, mix = saved
    exchange = None
    if parts is None:
        dmix = mm(dxn, p.w_out, "nt", tm=512, tn=1024, tk=dm.D, name=f"d_mix{tag}", out_dtype=BF16)
    else:
        dmix, got = mm(dxn, p.w_out, "nt", tm=512, tn=1024, tk=dm.D, name=f"d_mix_exchange{tag}", out_dtype=BF16,
                       exchange=pair_exchange_spec(parts))
        exchange = chip_exchange_spec([pair_add(parts[0], got[0], "pair_add_w_in"),
                                       pair_add(parts[1], got[1], "pair_add_w_out")])
    dw_out = mm(mix, dxn, "tn", tm=1024, tn=dm.D, tk=512, name=f"dw_out{tag}", out_dtype=BF16)
    dyo, dzg, dsnw = gate_bwd(dmix, y, o, proj, p.snw, dm, f"gate_bwd{tag}")
    dq, dk, dv = attn_bwd(proj, dyo, tot, nblk, dm, f"attn_bwd{tag}")
    dxs, db, dc, dac, span, ddtx, gsk = ssd_bwd(xa, prep, p.par, hprev, dyo, dm, f"ssd_bwd{tag}")
    ddt_blk, ghead = ssd_post(proj, p.par_all, dac, span, ddtx, gsk, dm, f"ssd_post{tag}")
    dcv, gconv = conv_bwd_pre(proj, dxs, db, dc, p.cw, p.cb, dm, f"conv_bwd_pre{tag}")
    dxbc = conv_bwd_in(dcv, p.cw, dm, f"conv_bwd_in{tag}")
    dproj = jnp.concatenate([dzg[:, :dm.SW], dxbc, dq, dk, dv, dzg[:, dm.SW:], ddt_blk], axis=1)
    brought = None
    if exchange is None:
        dh = mm(dproj, p.w_in, "nt", tm=512, tn=dm.D, tk=1920, name=f"d_h{tag}")
    else:
        dh, brought = mm(dproj, p.w_in, "nt", tm=512, tn=dm.D, tk=1920, name=f"d_h_exchange{tag}", exchange=exchange)
    dw_in = mm(h, dproj, "tn", tm=dm.D, tn=960, tk=2048, name=f"dw_in{tag}", out_dtype=BF16)
    dx, dnw = rms_bwd(dh, x, p.nw, dxn, f"rms_bwd{tag}")
    small = dict(norm_w=dnw[0], conv_w=gconv[:SSD_CONV], conv_b=gconv[SSD_CONV],
                 dt_bias=ghead[0, :dm.NH], a_log=ghead[1, :dm.NH], d_skip=ghead[2, :dm.NH],
                 ssd_norm_w=dsnw[0, :dm.SW])
    return dx, dw_in, dw_out, small, brought


SMALL = ("norm_w", "conv_b", "dt_bias", "a_log", "d_skip", "ssd_norm_w")


def _to_mine(w, dm):
    a, b = dm.SW + dm.CD, dm.SW + dm.CD + dm.NH
    pad = jnp.zeros((w.shape[0], LANES - dm.NH), w.dtype)
    return jnp.concatenate([w[:, :a], w[:, b:], w[:, a:b], pad], axis=1)


def _from_mine(w, dm):
    a = dm.SW + dm.CD
    return jnp.concatenate([w[:, :a], w[:, dm.dt_off:dm.dt_off + dm.NH], w[:, a:dm.dt_off]], axis=1)


def weights_to_mine(g_in, dm, name):
    _, D, ncol = g_in.shape
    tm = _pick(D, 256, 16)

    def body(g_ref, o_ref):
        full = jnp.concatenate([g_ref[j] for j in range(N_DEV)], axis=1)
        o_ref[...] = _to_mine(full, dm)

    return pl.pallas_call(
        body, name=name, grid=(D // tm,),
        in_specs=[pl.BlockSpec((N_DEV, tm, ncol), lambda i: (0, i, 0))],
        out_specs=pl.BlockSpec((tm, dm.NP), lambda i: (i, 0)), out_shape=SDS((D, dm.NP), g_in.dtype),
        compiler_params=_cp(("arbitrary",), 6 * tm * dm.NP * 2))(g_in)


def grads_from_mine(gw, dm, ncol, name):
    D = gw.shape[0]
    tm = _pick(D, 256, 16)

    def body(g_ref, o_ref):
        full = _from_mine(g_ref[...], dm)
        for j in range(N_DEV):
            o_ref[j] = full[:, j * ncol:(j + 1) * ncol]

    return pl.pallas_call(
        body, name=name, grid=(D // tm,),
        in_specs=[pl.BlockSpec((tm, dm.NP), lambda i: (i, 0))],
        out_specs=pl.BlockSpec((N_DEV, tm, ncol), lambda i: (0, i, 0)), out_shape=SDS((N_DEV, D, ncol), gw.dtype),
        compiler_params=_cp(("arbitrary",), 6 * tm * dm.NP * 2))(gw)


def _pack(pieces):
    flat = jnp.concatenate([p.reshape(-1) for p in pieces])
    rows = -(-flat.shape[0] // LANES)
    rows = -(-rows // SUBLANES) * SUBLANES
    return jnp.pad(flat, (0, rows * LANES - flat.shape[0])).reshape(rows, LANES)


def _unpack(buf, shapes):
    flat, out, at = buf.reshape(-1), [], 0
    for s in shapes:
        n = math.prod(s)
        out.append(flat[at:at + n].reshape(s))
        at += n
    return out


def kernel(x, norm_w, w_in, conv_w, conv_b, dt_bias, a_log, d_skip, ssd_norm_w, w_out, final_norm_w, loss_target, m_norm_w, m_w_in, m_conv_w, m_conv_b, m_dt_bias, m_a_log, m_d_skip, m_ssd_norm_w, m_w_out, m_final_norm_w, v_norm_w, v_w_in, v_conv_w, v_conv_b, v_dt_bias, v_a_log, v_d_skip, v_ssd_norm_w, v_w_out, v_final_norm_w):
    depth, D = norm_w.shape
    L = x.shape[1]
    NH = dt_bias.shape[1]
    SW = NH * SSD_HEAD_DIM
    CD = conv_b.shape[1]
    dm = Dims(L=L, D=D, SW=SW, G=(CD - SW) // (2 * SSD_STATE), AW=w_out.shape[1] * N_DEV - SW)
    ncol, csh, osh = w_in.shape[2], conv_w.shape[2], w_out.shape[1]
    me = 4 * lax.axis_index("x") + 2 * lax.axis_index("y") + lax.axis_index("c")

    shards = [(w_in[l].astype(BF16), w_out[l].astype(BF16), conv_w[l]) for l in range(depth)]

    def layer_params(l, gathered):
        g_in, g_out, g_cw = gathered
        full_cw = g_cw.transpose(1, 0, 2).reshape(SSD_CONV, CD)
        par, par_all = head_params(dt_bias[l], a_log[l], d_skip[l], dm)
        return LayerParams(
            nw=norm_w[l][None], w_in=weights_to_mine(g_in, dm, "weights_to_mine"), cw=full_cw, cb=conv_b[l][None],
            par=par, par_all=par_all,
            snw=ssd_norm_w[l][None], w_out=g_out.reshape(N_DEV * osh, D))

    h = x[0]
    params, saved = [], []
    gathered = gather_weights(list(shards[0]), "gather_weights")
    for l in range(depth):
        params.append(layer_params(l, gathered))
        h, s, gathered = layer_fwd(h, params[l], dm, "", shards[l + 1] if l + 1 < depth else None)
        saved.append(s)
    dh, dfw, ls = loss_head(h, final_norm_w[None], loss_target[0], "loss_head")
    loss = lax.psum(ls[0, 0], ("x", "y", "c"))
    smalls, r_in, r_out = [None] * depth, [None] * depth, [None] * depth
    parts = None
    for l in reversed(range(depth)):
        dh, gw_in, gw_out, smalls[l], brought = layer_bwd(dh, saved[l], params[l], dm, "", parts)
        if brought is not None:
            r_in[l + 1], r_out[l + 1] = brought
        parts = [grads_from_mine(gw_in, dm, ncol, "grads_from_mine"), gw_out.astype(BF16).reshape(N_DEV, osh, D)]
    s_in, s_out = pair_exchange(parts, "pair_exchange")
    pending = [pair_add(parts[0], s_in, "pair_add_w_in"), pair_add(parts[1], s_out, "pair_add_w_out")]
    grad_x = dh[None]
    rep = [jnp.stack([s[k] for s in smalls]) for k in SMALL] + [dfw[0]]
    rep_shapes = [r.shape for r in rep]
    p_rep = _pack(rep)
    p_cw = jnp.stack([s["conv_w"] for s in smalls]).reshape(depth * SSD_CONV, N_DEV, csh).transpose(1, 0, 2)
    r_in[0], r_out[0], r_rep, r_cw = chip_exchange(pending, [p_rep], [p_cw], "chip_exchange")
    r_in, r_out = jnp.concatenate(r_in, axis=1), jnp.concatenate(r_out, axis=1)

    out_in = adamw(r_in, w_in.reshape(depth * D, ncol), m_w_in.reshape(depth * D, ncol),
                   v_w_in.reshape(depth * D, ncol), "adamw_w_in")
    out_out = adamw(r_out, w_out.reshape(depth * osh, D), m_w_out.reshape(depth * osh, D),
                    v_w_out.reshape(depth * osh, D), "adamw_w_out")
    out_cw = adamw(r_cw, conv_w.reshape(depth * SSD_CONV, csh), m_conv_w.reshape(depth * SSD_CONV, csh),
                   v_conv_w.reshape(depth * SSD_CONV, csh), "adamw_conv_w")
    rep_w = [norm_w, conv_b, dt_bias, a_log, d_skip, ssd_norm_w, final_norm_w]
    rep_m = [m_norm_w, m_conv_b, m_dt_bias, m_a_log, m_d_skip, m_ssd_norm_w, m_final_norm_w]
    rep_v = [v_norm_w, v_conv_b, v_dt_bias, v_a_log, v_d_skip, v_ssd_norm_w, v_final_norm_w]
    out_rep = adamw(r_rep, _pack(rep_w), _pack(rep_m), _pack(rep_v), "adamw_replicated")

    outs = {}
    for kind, i in (("grad", 0), ("delta", 1), ("new_m", 2), ("new_v", 3)):
        r = dict(zip(SMALL + ("final_norm_w",), _unpack(out_rep[i], rep_shapes)))
        r["w_in"] = out_in[i].reshape(w_in.shape)
        r["w_out"] = out_out[i].reshape(w_out.shape)
        r["conv_w"] = out_cw[i].reshape(conv_w.shape)
        outs[kind] = r
    order = ("norm_w", "w_in", "conv_w", "conv_b", "dt_bias", "a_log", "d_skip", "ssd_norm_w", "w_out", "final_norm_w")
    return (loss, grad_x, *[outs[k][n] for k in ("grad", "delta", "new_m", "new_v") for n in order])
```
